```python
import math
import jax
import jax.numpy as jnp
from jax import lax
import numpy as np

D_MODEL = 1024
BATCH = 8
SEQ = 8192
DEPTH = 1

MIX_WIDTH = D_MODEL
PLE_DIM = 256
DA_WIDTH = MIX_WIDTH // 2
DA_HEADS = 4
DA_V_DIM = DA_WIDTH // DA_HEADS
DA_HEAD_DIM = DA_V_DIM // 2
ROT_DIM = DA_HEAD_DIM // 4
ROPE_THETA = 500000.0
Q_BLOCK = 128
HG_WIDTH = MIX_WIDTH - DA_WIDTH
HG_HEADS = 4
HG_KEY_DIM = HG_WIDTH // HG_HEADS
HG_VAL_DIM = HG_WIDTH // HG_HEADS
CHUNK = 64
D_FF = 2816
CONV_WIDTH = 3
EPS = 1e-6

IN_SPLIT_SIZES = (
    DA_HEADS * 2 * DA_HEAD_DIM,
    DA_HEADS * 2 * DA_HEAD_DIM,
    DA_HEADS * DA_V_DIM,
    HG_HEADS * HG_KEY_DIM,
    HG_HEADS * HG_KEY_DIM,
    HG_HEADS * HG_KEY_DIM,
    HG_HEADS * HG_VAL_DIM,
    HG_HEADS * HG_VAL_DIM,
)
IN_COLS = sum(IN_SPLIT_SIZES)
IN_SPLIT_POINTS = tuple(int(c) for c in np.cumsum(IN_SPLIT_SIZES)[:-1])

kernel_name = "hybrid_diffattn_hgrn2_convglu_encoder"


def rmsnorm(x, g, eps=EPS):
    xf = x.astype(jnp.float32)
    y = xf * lax.rsqrt(jnp.mean(xf * xf, axis=-1, keepdims=True) + eps)
    return (y * g.astype(jnp.float32)).astype(x.dtype)


def partial_rotary(t, positions):
    half = ROT_DIM // 2
    inv_freq = ROPE_THETA ** (-jnp.arange(half, dtype=jnp.float32) / half)
    ang = positions.astype(jnp.float32)[..., None] * inv_freq
    cos = jnp.cos(ang)[:, :, None, None, :]
    sin = jnp.sin(ang)[:, :, None, None, :]
    tr = t[..., :ROT_DIM].astype(jnp.float32)
    x1, x2 = tr[..., :half], tr[..., half:]
    rot = jnp.concatenate([x1 * cos - x2 * sin, x2 * cos + x1 * sin], axis=-1)
    return jnp.concatenate([rot.astype(t.dtype), t[..., ROT_DIM:]], axis=-1)


def diff_attention(q, k, v, lam_q1, lam_k1, lam_q2, lam_k2, subln_g, lambda_init):
    B, S = q.shape[0], q.shape[1]
    f32 = jnp.float32
    lam = (jnp.exp(jnp.sum(lam_q1.astype(f32) * lam_k1.astype(f32)))
           - jnp.exp(jnp.sum(lam_q2.astype(f32) * lam_k2.astype(f32))) + lambda_init)
    scale = DA_HEAD_DIM ** -0.5
    qb = (q * scale).reshape(B, S // Q_BLOCK, Q_BLOCK, DA_HEADS, 2, DA_HEAD_DIM)
    qb = qb.transpose(1, 0, 2, 3, 4, 5)

    def block(qblk):
        s = jnp.einsum('bqhmd,bkhmd->bhmqk', qblk, k, preferred_element_type=f32)
        a = jax.nn.softmax(s, axis=-1)
        w = a[:, :, 0] - lam * a[:, :, 1]
        return jnp.einsum('bhqk,bkhe->bqhe', w.astype(v.dtype), v)

    o = lax.map(block, qb)
    o = o.transpose(1, 0, 2, 3, 4).reshape(B, S, DA_HEADS, DA_V_DIM)
    o = rmsnorm(o, subln_g) * (1.0 - lambda_init)
    return o.reshape(B, S, DA_HEADS * DA_V_DIM)


def _to_chunks(t):
    z, b, s, h, e = t.shape
    return t.reshape(z, b, s // CHUNK, CHUNK, h, e).transpose(2, 0, 1, 4, 3, 5)


def hgrn2_bidirectional(q, zf, zb, v, g, lb_gamma, norm_g, layer):
    B, S = q.shape[0], q.shape[1]
    f32 = jnp.float32
    H, dk, dv = HG_HEADS, HG_KEY_DIM, HG_VAL_DIM
    lb_all = jnp.cumsum(jax.nn.softmax(lb_gamma.astype(f32), axis=1), axis=1)
    lb = lb_all[:, layer].reshape(2, 1, 1, H, dk)
    q4 = q.reshape(B, S, H, dk).astype(f32)
    v4 = v.reshape(B, S, H, dv).astype(f32)
    z = jnp.stack([zf.reshape(B, S, H, dk), zb.reshape(B, S, H, dk)[:, ::-1]], 0).astype(f32)
    log_f = jnp.log(lb + (1.0 - lb) * jax.nn.sigmoid(z))
    k_in = (1.0 - lb) * jax.nn.sigmoid(-z)
    qd = jnp.stack([q4, q4[:, ::-1]], 0)
    vd = jnp.stack([v4, v4[:, ::-1]], 0)
    xs = (_to_chunks(qd), _to_chunks(k_in), _to_chunks(vd), _to_chunks(log_f))
    tri = jnp.tril(jnp.ones((CHUNK, CHUNK), dtype=bool))[:, :, None]

    def step(state, inp):
        qc, kc, vc, gc = inp
        bcum = jnp.cumsum(gc, axis=-2)
        b_last = bcum[..., -1:, :]
        o_inter = jnp.einsum('zbhtk,zbhkv->zbhtv', qc * jnp.exp(bcum), state)
        diff = bcum[..., :, None, :] - bcum[..., None, :, :]
        decay = jnp.exp(jnp.where(tri, diff, -jnp.inf))
        attn = jnp.einsum('zbhtk,zbhtsk->zbhts', qc, decay * kc[..., None, :, :])
        o_intra = jnp.einsum('zbhts,zbhsv->zbhtv', attn, vc)
        k_dec = kc * jnp.exp(b_last - bcum)
        new_state = (jnp.exp(b_last[..., 0, :])[..., None] * state
                     + jnp.einsum('zbhsk,zbhsv->zbhkv', k_dec, vc))
        return new_state, o_inter + o_intra

    state0 = jnp.zeros((2, B, H, dk, dv), f32)
    _, ys = lax.scan(step, state0, xs)
    ys = ys.transpose(1, 2, 0, 4, 3, 5).reshape(2, B, S, H, dv)
    o = ys[0] + ys[1][:, ::-1]
    o = rmsnorm(o, norm_g) * jax.nn.silu(g.reshape(B, S, H, dv).astype(f32))
    return o.reshape(B, S, H * dv).astype(q.dtype)


def conv_glu(h, w_gate, w_up, conv_w, conv_b, w_down):
    S = h.shape[1]
    a = h @ w_gate
    u = h @ w_up
    pad = CONV_WIDTH // 2
    ap = jnp.pad(a, ((0, 0), (pad, pad), (0, 0)))
    c = conv_b
    for j in range(CONV_WIDTH):
        c = c + ap[:, j:j + S] * conv_w[j]
    return (jax.nn.gelu(c, approximate=False) * u) @ w_down


def setup_inputs(seed: int = 0) -> dict:
    key = jax.random.key(seed)
    ks = jax.random.split(key, 24)
    f32 = jnp.float32

    def nrm(k, shape, scale):
        return jax.random.normal(k, shape, f32) * scale

    def gain(k, shape):
        return 1.0 + 0.05 * jax.random.normal(k, shape, f32)

    x = nrm(ks[0], (BATCH, SEQ, D_MODEL), 1.0)
    p = nrm(ks[1], (DEPTH, BATCH, SEQ, PLE_DIM), 1.0)
    offsets = jax.random.randint(ks[2], (BATCH, 1), 0, 1024, dtype=jnp.int32)
    positions = offsets + jnp.arange(SEQ, dtype=jnp.int32)[None, :]
    return {
        "x": x,
        "p": p,
        "positions": positions,
        "norm_mix_g": gain(ks[3], (DEPTH, D_MODEL)),
        "w_in": nrm(ks[4], (DEPTH, D_MODEL, IN_COLS), D_MODEL ** -0.5),
        "lam_q1": nrm(ks[5], (DEPTH, DA_HEAD_DIM), 0.1),
        "lam_k1": nrm(ks[6], (DEPTH, DA_HEAD_DIM), 0.1),
        "lam_q2": nrm(ks[7], (DEPTH, DA_HEAD_DIM), 0.1),
        "lam_k2": nrm(ks[8], (DEPTH, DA_HEAD_DIM), 0.1),
        "da_subln_g": gain(ks[9], (DEPTH, DA_V_DIM)),
        "hg_lb_gamma": gain(ks[10], (2, DEPTH + 1, HG_HEADS * HG_KEY_DIM)),
        "hg_norm_g": gain(ks[11], (DEPTH, HG_VAL_DIM)),
        "w_out": nrm(ks[12], (DEPTH, MIX_WIDTH, D_MODEL), MIX_WIDTH ** -0.5),
        "norm_ffn_g": gain(ks[13], (DEPTH, D_MODEL)),
        "w_ffn_gate": nrm(ks[14], (DEPTH, D_MODEL, D_FF), D_MODEL ** -0.5),
        "w_ffn_up": nrm(ks[15], (DEPTH, D_MODEL, D_FF), D_MODEL ** -0.5),
        "ffn_conv_w": nrm(ks[16], (DEPTH, CONV_WIDTH, D_FF), CONV_WIDTH ** -0.5),
        "ffn_conv_b": nrm(ks[17], (DEPTH, D_FF), 0.02),
        "w_ffn_down": nrm(ks[18], (DEPTH, D_FF, D_MODEL), D_FF ** -0.5),
        "norm_ple_g": gain(ks[19], (DEPTH, D_MODEL)),
        "w_ple": nrm(ks[20], (DEPTH, PLE_DIM, D_MODEL), PLE_DIM ** -0.5),
        "w_ple_gate": nrm(ks[21], (DEPTH, D_MODEL, D_MODEL), D_MODEL ** -0.5),
        "final_norm_g": gain(ks[22], (D_MODEL,)),
    }


def reference(x, p, positions, norm_mix_g, w_in, lam_q1, lam_k1, lam_q2, lam_k2,
              da_subln_g, hg_lb_gamma, hg_norm_g, w_out, norm_ffn_g, w_ffn_gate,
              w_ffn_up, ffn_conv_w, ffn_conv_b, w_ffn_down, norm_ple_g, w_ple,
              w_ple_gate, final_norm_g):
    B, S = x.shape[0], x.shape[1]
    for i in range(DEPTH):
        h = rmsnorm(x, norm_mix_g[i])
        proj = h @ w_in[i]
        da_q, da_k, da_v, hg_q, hg_zf, hg_zb, hg_v, hg_g = jnp.split(proj, IN_SPLIT_POINTS, axis=-1)
        da_q = partial_rotary(da_q.reshape(B, S, DA_HEADS, 2, DA_HEAD_DIM), positions)
        da_k = partial_rotary(da_k.reshape(B, S, DA_HEADS, 2, DA_HEAD_DIM), positions)
        da_v = da_v.reshape(B, S, DA_HEADS, DA_V_DIM)
        lambda_init = 0.8 - 0.6 * math.exp(-0.3 * i)
        o_da = diff_attention(da_q, da_k, da_v, lam_q1[i], lam_k1[i], lam_q2[i], lam_k2[i],
                              da_subln_g[i], lambda_init)
        o_hg = hgrn2_bidirectional(hg_q, hg_zf, hg_zb, hg_v, hg_g, hg_lb_gamma, hg_norm_g[i], i)
        x = x + jnp.concatenate([o_da, o_hg], axis=-1) @ w_out[i]
        x = x + conv_glu(rmsnorm(x, norm_ffn_g[i]), w_ffn_gate[i], w_ffn_up[i],
                         ffn_conv_w[i], ffn_conv_b[i], w_ffn_down[i])
        gate = jax.nn.sigmoid(rmsnorm(x, norm_ple_g[i]) @ w_ple_gate[i])
        x = x + (p[i] @ w_ple[i]) * gate
    return rmsnorm(x, final_norm_g)
```

```python
import functools
import math

import numpy as np
import jax
import jax.numpy as jnp
from jax import lax
from jax.experimental import pallas as pl
from jax.experimental.pallas import tpu as pltpu

F32 = jnp.float32
BF16 = jnp.bfloat16

EPS = 1e-6
ROPE_THETA = 500000.0
DA_HEADS = 4
DA_HEAD_DIM = 64
DA_V_DIM = 128
ROT_DIM = 16
HG_HEADS = 4
HG_DIM = 128
CONV_WIDTH = 3
HG_CHUNK = 64
HG_SUB = 16
LANES = 128
VMEM_LIMIT = 56 * 1024 * 1024


def _rms(xf, g):
    return xf * lax.rsqrt(jnp.mean(xf * xf, axis=-1, keepdims=True) + EPS) * g


def _sigmoid(z):
    return 1.0 / (1.0 + jnp.exp(-z))


def _gelu(x):
    return 0.5 * x * (1.0 + lax.erf(x * (2.0 ** -0.5)))


def _resident(shape):
    return pl.BlockSpec(shape, lambda *_: (0,) * len(shape), pipeline_mode=pl.Buffered(1))


def _chunk_cumsum(x, reverse):
    n = x.shape[0]
    r = lax.broadcasted_iota(jnp.int32, x.shape, 0) & (HG_CHUNK - 1)
    s = 1
    while s < HG_CHUNK:
        if reverse:
            x = x + jnp.where(r < HG_CHUNK - s, pltpu.roll(x, n - s, axis=0), 0.0)
        else:
            x = x + jnp.where(r >= s, pltpu.roll(x, s, axis=0), 0.0)
        s *= 2
    return x


def _in_proj_kernel(layer, n_slots, x_ref, pos_ref, rope_ref, g_ref, w_ref, gam_ref,
                    qkv_ref, hq_ref, bc_ref, kin_ref, hv_ref, sg_ref):
    h = _rms(x_ref[...], g_ref[...]).astype(BF16)
    sec = qkv_ref.shape[1] // 3

    def proj(i):
        return jnp.dot(h, w_ref[:, i * sec:(i + 1) * sec], preferred_element_type=F32)

    ang = pos_ref[...].astype(F32) * rope_ref[0:1, :]
    cos = jnp.cos(ang)
    sin = jnp.sin(ang)
    sin_lo = sin * rope_ref[1:2, :]
    sin_hi = sin * rope_ref[2:3, :]
    half = ROT_DIM // 2

    def rotary(t, scale):
        outs = []
        for hh in range(sec // LANES):
            blk = t[:, hh * LANES:(hh + 1) * LANES]
            rot = (blk * cos + pltpu.roll(blk, LANES - half, axis=1) * sin_lo
                   + pltpu.roll(blk, half, axis=1) * sin_hi)
            outs.append(rot * scale if scale != 1.0 else rot)
        return jnp.concatenate(outs, axis=1)

    qkv_ref[:, 0:sec] = rotary(proj(0), DA_HEAD_DIM ** -0.5).astype(BF16)
    qkv_ref[:, sec:2 * sec] = rotary(proj(1), 1.0).astype(BF16)
    qkv_ref[:, 2 * sec:3 * sec] = proj(2).astype(BF16)
    hq_ref[...] = proj(3).astype(BF16)

    for d in range(2):
        gam = gam_ref[d * n_slots:(d + 1) * n_slots, :]
        e = jnp.exp(gam - jnp.max(gam, axis=0, keepdims=True))
        lb = jnp.sum(e[0:layer + 1, :], axis=0, keepdims=True) / jnp.sum(e, axis=0, keepdims=True)
        sig = _sigmoid(proj(4 + d))
        logf = jnp.log(lb + (1.0 - lb) * sig)
        bc_ref[:, d * sec:(d + 1) * sec] = _chunk_cumsum(logf, reverse=(d == 1))
        kin_ref[:, d * sec:(d + 1) * sec] = ((1.0 - lb) * (1.0 - sig)).astype(BF16)

    hv_ref[...] = proj(6).astype(BF16)
    gate = proj(7)
    sg_ref[...] = (gate * _sigmoid(gate)).astype(BF16)


def _in_proj(x2d, pos, rope_tab, g, w_bf, gam, layer, tm):
    T, D = x2d.shape
    sec = w_bf.shape[1] // 8
    n_slots = gam.shape[0] // 2
    row = lambda w: pl.BlockSpec((tm, w), lambda i: (i, 0))
    return pl.pallas_call(
        functools.partial(_in_proj_kernel, layer, n_slots),
        grid=(T // tm,),
        in_specs=[row(D), row(1), _resident(rope_tab.shape), _resident(g.shape),
                  _resident(w_bf.shape), _resident(gam.shape)],
        out_specs=[row(3 * sec), row(sec), row(2 * sec), row(2 * sec), row(sec), row(sec)],
        out_shape=[jax.ShapeDtypeStruct((T, 3 * sec), BF16), jax.ShapeDtypeStruct((T, sec), BF16),
                   jax.ShapeDtypeStruct((T, 2 * sec), F32), jax.ShapeDtypeStruct((T, 2 * sec), BF16),
                   jax.ShapeDtypeStruct((T, sec), BF16), jax.ShapeDtypeStruct((T, sec), BF16)],
        compiler_params=pltpu.CompilerParams(dimension_semantics=("parallel",),
                                             vmem_limit_bytes=VMEM_LIMIT),
        name="in_proj",
    )(x2d, pos, rope_tab, g, w_bf, gam)


def _diff_attn_kernel(lambda_init, tk, q_ref, k_ref, v_ref, lam_ref, g_ref, o_ref,
                      acc0_ref, acc1_ref, m0_ref, m1_ref):
    tq = q_ref.shape[0]
    n_kv = k_ref.shape[0] // tk
    q = q_ref[...]
    lane = lax.broadcasted_iota(jnp.int32, q.shape, 1)
    zero = jnp.zeros_like(q)
    q_maps = (jnp.where(lane < DA_HEAD_DIM, q, zero), jnp.where(lane >= DA_HEAD_DIM, q, zero))
    accs = (acc0_ref, acc1_ref)
    ms = (m0_ref, m1_ref)
    for a, m in zip(accs, ms):
        a[...] = jnp.zeros_like(a)
        m[...] = jnp.full_like(m, -jnp.inf)
    ones = jnp.ones((tk, LANES), BF16)

    def body(c, carry):
        r0 = pl.multiple_of(c * tk, tk)
        k_c = k_ref[pl.ds(r0, tk), :]
        v_aug = jnp.concatenate([v_ref[pl.ds(r0, tk), :], ones], axis=1)
        for qm, a, m in zip(q_maps, accs, ms):
            s = lax.dot_general(qm, k_c, (((1,), (1,)), ((), ())), preferred_element_type=F32)
            m_old = m[...]
            m_new = jnp.maximum(m_old, jnp.max(s, axis=-1, keepdims=True))
            p = jnp.exp(s - m_new).astype(BF16)
            a[...] = jnp.exp(m_old - m_new) * a[...] + jnp.dot(p, v_aug, preferred_element_type=F32)
            m[...] = m_new
        return carry

    lax.fori_loop(0, n_kv, body, 0)

    lam_p = lam_ref[...]
    lam = (jnp.exp(jnp.sum(lam_p[0:1, :] * lam_p[1:2, :], axis=-1, keepdims=True))
           - jnp.exp(jnp.sum(lam_p[2:3, :] * lam_p[3:4, :], axis=-1, keepdims=True)) + lambda_init)
    a0 = acc0_ref[...]
    a1 = acc1_ref[...]
    o = a0[:, :LANES] / a0[:, LANES:] - lam * (a1[:, :LANES] / a1[:, LANES:])
    o_ref[...] = (_rms(o, g_ref[...]) * (1.0 - lambda_init)).astype(o_ref.dtype)


def _diff_attn(qkv3, lam_p, g, lambda_init, tq, tk):
    B, S, _ = qkv3.shape
    H = DA_HEADS
    return pl.pallas_call(
        functools.partial(_diff_attn_kernel, lambda_init, tk),
        grid=(B, H, S // tq),
        in_specs=[pl.BlockSpec((None, tq, LANES), lambda b, h, i: (b, i, h)),
                  pl.BlockSpec((None, S, LANES), lambda b, h, i: (b, 0, H + h)),
                  pl.BlockSpec((None, S, LANES), lambda b, h, i: (b, 0, 2 * H + h)),
                  _resident(lam_p.shape), _resident(g.shape)],
        out_specs=pl.BlockSpec((None, tq, LANES), lambda b, h, i: (b, i, h)),
        out_shape=jax.ShapeDtypeStruct((B, S, H * DA_V_DIM), BF16),
        scratch_shapes=[pltpu.VMEM((tq, 2 * LANES), F32), pltpu.VMEM((tq, 2 * LANES), F32),
                        pltpu.VMEM((tq, 1), F32), pltpu.VMEM((tq, 1), F32)],
        compiler_params=pltpu.CompilerParams(
            dimension_semantics=("parallel", "parallel", "parallel"), vmem_limit_bytes=VMEM_LIMIT),
        name="diff_attn",
    )(qkv3, qkv3, qkv3, lam_p, g)


def _hgrn_chunk(q, kin, v, b, state_ref, reverse):
    C, n_sub = HG_CHUNK, HG_CHUNK // HG_SUB
    blk = lambda j: slice(j * HG_SUB, (j + 1) * HG_SUB)
    bound = [b[j * HG_SUB:j * HG_SUB + 1, :] if reverse else b[(j + 1) * HG_SUB - 1:(j + 1) * HG_SUB, :]
             for j in range(n_sub)]
    b_far = bound[0] if reverse else bound[-1]
    k_til = jnp.concatenate([kin[blk(j)] * jnp.exp(bound[j] - b[blk(j)]) for j in range(n_sub)], axis=0)
    rows = [slice(0, (j + 1) * HG_SUB) if reverse else slice(j * HG_SUB, C) for j in range(n_sub)]
    q_til = jnp.concatenate([q[rows[j]] * jnp.exp(b[rows[j]] - bound[j]) for j in range(n_sub)], axis=0)
    offs = np.cumsum([0] + [r.stop - r.start for r in rows])
    scores = lax.dot_general(q_til.astype(BF16), k_til.astype(BF16), (((1,), (1,)), ((), ())),
                             preferred_element_type=F32)
    col = lax.broadcasted_iota(jnp.int32, (HG_SUB, C), 1)
    row_blocks = []
    for i in range(n_sub):
        js = range(i, n_sub) if reverse else range(0, i + 1)
        a_i = jnp.zeros((HG_SUB, C), F32)
        for j in js:
            r0 = int(offs[j]) + (i * HG_SUB if reverse else (i - j) * HG_SUB)
            a_i = jnp.where(col // HG_SUB == j, scores[r0:r0 + HG_SUB, :], a_i)
        row_blocks.append(a_i)
    attn = jnp.concatenate(row_blocks, axis=0)
    t_i = lax.broadcasted_iota(jnp.int32, (C, C), 0)
    s_i = lax.broadcasted_iota(jnp.int32, (C, C), 1)
    attn = jnp.where((t_i <= s_i) if reverse else (t_i >= s_i), attn, 0.0)
    o_intra = jnp.dot(attn.astype(BF16), v, preferred_element_type=F32)
    state_t = state_ref[...]
    o_inter = lax.dot_general((q * jnp.exp(b)).astype(BF16), state_t.astype(BF16),
                              (((1,), (1,)), ((), ())), preferred_element_type=F32)
    k_dec = (kin * jnp.exp(b_far - b)).astype(BF16)
    state_ref[...] = jnp.exp(b_far) * state_t + lax.dot_general(
        v, k_dec, (((0,), (0,)), ((), ())), preferred_element_type=F32)
    return o_intra + o_inter


def _hgrn_kernel(q_f_ref, q_b_ref, v_f_ref, v_b_ref, bc_f_ref, bc_b_ref, kin_f_ref, kin_b_ref,
                 y_f_ref, y_b_ref, st_f_ref, st_b_ref):
    @pl.when(pl.program_id(2) == 0)
    def _():
        st_f_ref[...] = jnp.zeros_like(st_f_ref)
        st_b_ref[...] = jnp.zeros_like(st_b_ref)

    n_chunks = q_f_ref.shape[0] // HG_CHUNK

    def body(c, carry):
        for reverse, q_ref, v_ref, bc_ref, kin_ref, y_ref, st_ref in (
                (False, q_f_ref, v_f_ref, bc_f_ref, kin_f_ref, y_f_ref, st_f_ref),
                (True, q_b_ref, v_b_ref, bc_b_ref, kin_b_ref, y_b_ref, st_b_ref)):
            cc = (n_chunks - 1 - c) if reverse else c
            rs = pl.ds(pl.multiple_of(cc * HG_CHUNK, HG_CHUNK), HG_CHUNK)
            y = _hgrn_chunk(q_ref[rs, :].astype(F32), kin_ref[rs, :].astype(F32), v_ref[rs, :],
                            bc_ref[rs, :], st_ref, reverse)
            y_ref[rs, :] = y.astype(y_ref.dtype)
        return carry

    lax.fori_loop(0, n_chunks, body, 0)


def _hgrn(hq3, hv3, bc3, kin3, seg):
    B, S, W = hq3.shape
    H = W // HG_DIM
    n_seg = S // seg
    fwd = lambda off: pl.BlockSpec((None, seg, HG_DIM), lambda b, h, s: (b, s, off + h))
    bwd = lambda off: pl.BlockSpec((None, seg, HG_DIM), lambda b, h, s: (b, n_seg - 1 - s, off + h))
    return pl.pallas_call(
        _hgrn_kernel,
        grid=(B, H, n_seg),
        in_specs=[fwd(0), bwd(0), fwd(0), bwd(0), fwd(0), bwd(H), fwd(0), bwd(H)],
        out_specs=[fwd(0), bwd(0)],
        out_shape=[jax.ShapeDtypeStruct((B, S, W), BF16), jax.ShapeDtypeStruct((B, S, W), BF16)],
        scratch_shapes=[pltpu.VMEM((HG_DIM, HG_DIM), F32), pltpu.VMEM((HG_DIM, HG_DIM), F32)],
        compiler_params=pltpu.CompilerParams(
            dimension_semantics=("parallel", "parallel", "arbitrary"), vmem_limit_bytes=VMEM_LIMIT),
        name="hgrn2",
    )(hq3, hq3, hv3, hv3, bc3, bc3, kin3, kin3)


def _out_proj_kernel(x_ref, oda_ref, yf_ref, yb_ref, sg_ref, g_ref, w_ref, o_ref):
    half = oda_ref.shape[1]
    y = yf_ref[...].astype(F32) + yb_ref[...].astype(F32)
    sg = sg_ref[...].astype(F32)
    heads = []
    for hh in range(y.shape[1] // HG_DIM):
        cs = slice(hh * HG_DIM, (hh + 1) * HG_DIM)
        heads.append(_rms(y[:, cs], g_ref[...]) * sg[:, cs])
    o_hg = jnp.concatenate(heads, axis=1).astype(BF16)
    o_ref[...] = (x_ref[...]
                  + jnp.dot(oda_ref[...], w_ref[0:half, :], preferred_element_type=F32)
                  + jnp.dot(o_hg, w_ref[half:, :], preferred_element_type=F32))


def _out_proj(x2d, oda, yf, yb, sg, g, w_bf, tm):
    T, D = x2d.shape
    row = lambda w: pl.BlockSpec((tm, w), lambda i: (i, 0))
    return pl.pallas_call(
        _out_proj_kernel,
        grid=(T // tm,),
        in_specs=[row(D), row(oda.shape[1]), row(yf.shape[1]), row(yb.shape[1]), row(sg.shape[1]),
                  _resident(g.shape), _resident(w_bf.shape)],
        out_specs=row(D),
        out_shape=jax.ShapeDtypeStruct((T, D), F32),
        compiler_params=pltpu.CompilerParams(dimension_semantics=("parallel",),
                                             vmem_limit_bytes=VMEM_LIMIT),
        name="out_proj",
    )(x2d, oda, yf, yb, sg, g, w_bf)


HALO_ROWS = 16


def _ffn_ple_kernel(fc, x_ref, halo_ref, p_ref, gf_ref, wg_ref, wu_ref, cw_ref, cb_ref, wd_ref,
                    gp_ref, wpg_ref, wple_ref, gfin_ref, o_ref, h_ref, acc_ref):
    tm = x_ref.shape[0]
    xt = x_ref[...]
    h_ref[0:tm, :] = _rms(xt, gf_ref[...]).astype(BF16)
    h_ref[tm:tm + HALO_ROWS, :] = _rms(halo_ref[...], gf_ref[...]).astype(BF16)
    row = lax.broadcasted_iota(jnp.int32, (tm, fc), 0)
    n_fc = wg_ref.shape[1] // fc
    for c in range(n_fc):
        cs = slice(c * fc, (c + 1) * fc)
        a_ext = jnp.dot(h_ref[...], wg_ref[:, cs], preferred_element_type=F32)
        u = jnp.dot(h_ref[0:tm, :], wu_ref[:, cs], preferred_element_type=F32)
        a = a_ext[0:tm]
        a_prev = jnp.where(row == 0, a_ext[tm:tm + 1], pltpu.roll(a, 1, axis=0))
        a_next = jnp.where(row == tm - 1, a_ext[tm + 1:tm + 2], pltpu.roll(a, tm - 1, axis=0))
        cv = cb_ref[:, cs] + a_prev * cw_ref[0:1, cs] + a * cw_ref[1:2, cs] + a_next * cw_ref[2:3, cs]
        act = (_gelu(cv) * u).astype(BF16)
        d = jnp.dot(act, wd_ref[cs, :], preferred_element_type=F32)
        if c == 0:
            acc_ref[...] = d
        else:
            acc_ref[...] += d
    x2 = xt + acc_ref[...]
    gate = _sigmoid(jnp.dot(_rms(x2, gp_ref[...]).astype(BF16), wpg_ref[...], preferred_element_type=F32))
    ple = jnp.dot(p_ref[...].astype(BF16), wple_ref[...], preferred_element_type=F32)
    o_ref[...] = _rms(x2 + ple * gate, gfin_ref[...])


def _ffn_ple(x1, halo, p2d, gf, wg, wu, cw, cb, wd, gp, wpg, wple, gfin, tm, fc):
    T, D = x1.shape
    row = lambda w: pl.BlockSpec((tm, w), lambda i: (i, 0))
    consts = (gf, wg, wu, cw, cb, wd, gp, wpg, wple, gfin)
    return pl.pallas_call(
        functools.partial(_ffn_ple_kernel, fc),
        grid=(T // tm,),
        in_specs=[row(D), pl.BlockSpec((None, HALO_ROWS, D), lambda i: (i, 0, 0)), row(p2d.shape[1])]
                 + [_resident(c.shape) for c in consts],
        out_specs=row(D),
        out_shape=jax.ShapeDtypeStruct((T, D), F32),
        scratch_shapes=[pltpu.VMEM((tm + HALO_ROWS, D), BF16), pltpu.VMEM((tm, D), F32)],
        compiler_params=pltpu.CompilerParams(dimension_semantics=("parallel",),
                                             vmem_limit_bytes=VMEM_LIMIT),
        name="ffn_ple",
    )(x1, halo, p2d, *consts)


def _conv_halo(x1, tm, seq):
    T, D = x1.shape
    nt = T // tm
    xr = x1.reshape(nt, tm, D)
    zero = jnp.zeros((1, D), x1.dtype)
    prev = jnp.concatenate([zero, xr[:-1, tm - 1, :]], axis=0)
    nxt = jnp.concatenate([xr[1:, 0, :], zero], axis=0)
    start = (jnp.arange(nt) * tm) % seq
    prev = jnp.where((start == 0)[:, None], 0.0, prev)
    nxt = jnp.where((start + tm == seq)[:, None], 0.0, nxt)
    pad = jnp.zeros((nt, HALO_ROWS - 2, D), x1.dtype)
    return jnp.concatenate([prev[:, None, :], nxt[:, None, :], pad], axis=1)


def _rope_table():
    half = ROT_DIM // 2
    inv_freq = (np.float32(ROPE_THETA) ** (-np.arange(half, dtype=np.float32) / np.float32(half))).astype(np.float32)
    d = np.arange(LANES) % DA_HEAD_DIM
    tab = np.zeros((8, LANES), np.float32)
    tab[0] = np.where(d < ROT_DIM, inv_freq[d % half], 0.0)
    tab[1] = np.where(d < half, -1.0, 0.0)
    tab[2] = np.where((d >= half) & (d < ROT_DIM), 1.0, 0.0)
    return jnp.asarray(tab)


def kernel(x, p, positions, norm_mix_g, w_in, lam_q1, lam_k1, lam_q2, lam_k2, da_subln_g, hg_lb_gamma, hg_norm_g, w_out, norm_ffn_g, w_ffn_gate, w_ffn_up, ffn_conv_w, ffn_conv_b, w_ffn_down, norm_ple_g, w_ple, w_ple_gate, final_norm_g):
    B, S, D = x.shape
    T = B * S
    depth = w_in.shape[0]
    tm = min(512, S)
    tq = min(512, S)
    tk = min(512, S)
    seg = min(2048, S)
    fc = 256
    row2 = lambda v: v.reshape(1, -1)
    xc = x.reshape(T, D)
    pos = positions.reshape(T, 1)
    rope_tab = _rope_table()
    for i in range(depth):
        lambda_init = 0.8 - 0.6 * math.exp(-0.3 * i)
        gam = hg_lb_gamma.reshape(-1, hg_lb_gamma.shape[-1])
        qkv, hq, bc, kin, hv, sg = _in_proj(xc, pos, rope_tab, row2(norm_mix_g[i]), w_in[i].astype(BF16),
                                            gam, i, tm)
        lam_p = jnp.stack([lam_q1[i], lam_k1[i], lam_q2[i], lam_k2[i]], axis=0)
        o_da = _diff_attn(qkv.reshape(B, S, -1), lam_p, row2(da_subln_g[i]), lambda_init, tq, tk)
        y_f, y_b = _hgrn(hq.reshape(B, S, -1), hv.reshape(B, S, -1), bc.reshape(B, S, -1),
                         kin.reshape(B, S, -1), seg)
        x1 = _out_proj(xc, o_da.reshape(T, -1), y_f.reshape(T, -1), y_b.reshape(T, -1), sg,
                       row2(hg_norm_g[i]), w_out[i].astype(BF16), tm)
        last = i == depth - 1
        xc = _ffn_ple(x1, _conv_halo(x1, tm, S), p[i].reshape(T, -1), row2(norm_ffn_g[i]),
                      w_ffn_gate[i].astype(BF16), w_ffn_up[i].astype(BF16), ffn_conv_w[i],
                      row2(ffn_conv_b[i]), w_ffn_down[i].astype(BF16), row2(norm_ple_g[i]),
                      w_ple_gate[i].astype(BF16), w_ple[i].astype(BF16), row2(final_norm_g), tm, fc)
        assert last, "multi-layer stacks need the final norm split out of ffn_ple"
    return xc.reshape(B, S, D)
```

```python
import functools
import math

import numpy as np
import jax
import jax.numpy as jnp
from jax import lax
from jax.experimental import pallas as pl
from jax.experimental.pallas import tpu as pltpu

F32 = jnp.float32
BF16 = jnp.bfloat16

EPS = 1e-6
ROPE_THETA = 500000.0
DA_HEADS = 4
DA_HEAD_DIM = 64
DA_V_DIM = 128
ROT_DIM = 16
HG_HEADS = 4
HG_DIM = 128
CONV_WIDTH = 3
HG_CHUNK = 64
HG_SUB = 16
LANES = 128
VMEM_LIMIT = 56 * 1024 * 1024


def _rms(xf, g):
    return xf * lax.rsqrt(jnp.mean(xf * xf, axis=-1, keepdims=True) + EPS) * g


def _sigmoid(z):
    return 1.0 / (1.0 + jnp.exp(-z))


def _gelu(x):
    return 0.5 * x * (1.0 + lax.erf(x * (2.0 ** -0.5)))


def _resident(shape):
    return pl.BlockSpec(shape, lambda *_: (0,) * len(shape), pipeline_mode=pl.Buffered(1))


def _chunk_cumsum(x, reverse):
    n = x.shape[0]
    r = lax.broadcasted_iota(jnp.int32, x.shape, 0) & (HG_CHUNK - 1)
    s = 1
    while s < HG_CHUNK:
        if reverse:
            x = x + jnp.where(r < HG_CHUNK - s, pltpu.roll(x, n - s, axis=0), 0.0)
        else:
            x = x + jnp.where(r >= s, pltpu.roll(x, s, axis=0), 0.0)
        s *= 2
    return x


def _in_proj_kernel(layer, n_slots, x_ref, pos_ref, rope_ref, g_ref, w_ref, gam_ref,
                    qkv_ref, hq_ref, bc_ref, kin_ref, hv_ref, sg_ref):
    h = _rms(x_ref[...], g_ref[...]).astype(BF16)
    sec = qkv_ref.shape[1] // 3

    def proj(i):
        return jnp.dot(h, w_ref[:, i * sec:(i + 1) * sec], preferred_element_type=F32)

    ang = pos_ref[...].astype(F32) * rope_ref[0:1, :]
    cos = jnp.cos(ang)
    sin = jnp.sin(ang)
    sin_lo = sin * rope_ref[1:2, :]
    sin_hi = sin * rope_ref[2:3, :]
    half = ROT_DIM // 2

    def rotary(t, scale):
        outs = []
        for hh in range(sec // LANES):
            blk = t[:, hh * LANES:(hh + 1) * LANES]
            rot = (blk * cos + pltpu.roll(blk, LANES - half, axis=1) * sin_lo
                   + pltpu.roll(blk, half, axis=1) * sin_hi)
            outs.append(rot * scale if scale != 1.0 else rot)
        return jnp.concatenate(outs, axis=1)

    qkv_ref[:, 0:sec] = rotary(proj(0), DA_HEAD_DIM ** -0.5 * math.log2(math.e)).astype(BF16)
    qkv_ref[:, sec:2 * sec] = rotary(proj(1), 1.0).astype(BF16)
    qkv_ref[:, 2 * sec:3 * sec] = proj(2).astype(BF16)
    hq_ref[...] = proj(3).astype(BF16)

    for d in range(2):
        gam = gam_ref[d * n_slots:(d + 1) * n_slots, :]
        e = jnp.exp(gam - jnp.max(gam, axis=0, keepdims=True))
        lb = jnp.sum(e[0:layer + 1, :], axis=0, keepdims=True) / jnp.sum(e, axis=0, keepdims=True)
        sig = _sigmoid(proj(4 + d))
        logf = jnp.log(lb + (1.0 - lb) * sig)
        bc_ref[:, d * sec:(d + 1) * sec] = _chunk_cumsum(logf, reverse=(d == 1))
        kin_ref[:, d * sec:(d + 1) * sec] = ((1.0 - lb) * (1.0 - sig)).astype(BF16)

    hv_ref[...] = proj(6).astype(BF16)
    gate = proj(7)
    sg_ref[...] = (gate * _sigmoid(gate)).astype(BF16)


def _in_proj(x2d, pos, rope_tab, g, w_bf, gam, layer, tm):
    T, D = x2d.shape
    sec = w_bf.shape[1] // 8
    n_slots = gam.shape[0] // 2
    row = lambda w: pl.BlockSpec((tm, w), lambda i: (i, 0))
    return pl.pallas_call(
        functools.partial(_in_proj_kernel, layer, n_slots),
        grid=(T // tm,),
        in_specs=[row(D), row(1), _resident(rope_tab.shape), _resident(g.shape),
                  _resident(w_bf.shape), _resident(gam.shape)],
        out_specs=[row(3 * sec), row(sec), row(2 * sec), row(2 * sec), row(sec), row(sec)],
        out_shape=[jax.ShapeDtypeStruct((T, 3 * sec), BF16), jax.ShapeDtypeStruct((T, sec), BF16),
                   jax.ShapeDtypeStruct((T, 2 * sec), F32), jax.ShapeDtypeStruct((T, 2 * sec), BF16),
                   jax.ShapeDtypeStruct((T, sec), BF16), jax.ShapeDtypeStruct((T, sec), BF16)],
        compiler_params=pltpu.CompilerParams(dimension_semantics=("parallel",),
                                             vmem_limit_bytes=VMEM_LIMIT),
        name="in_proj",
    )(x2d, pos, rope_tab, g, w_bf, gam)


DEN_ROWS = 16


def _diff_attn_kernel(lambda_init, tq, tk, q_ref, k_ref, v_ref, lam_ref, g_ref, o_ref,
                      vt_ref, acc_ref, m_ref, alpha_ref, s_ref, p_ref):
    S = k_ref.shape[0]
    n_kv = S // tk
    n_tasks = (S // tq) * n_kv
    dv = v_ref.shape[1]

    def transpose_v(i, carry):
        r = pl.ds(pl.multiple_of(i * tk, tk), tk)
        vt_ref[0:dv, r] = v_ref[r, :].astype(F32).T.astype(BF16)
        return carry

    lax.fori_loop(0, n_kv, transpose_v, 0)
    vt_ref[dv:dv + DEN_ROWS, :] = jnp.ones((DEN_ROWS, S), BF16)
    acc_ref[...] = jnp.zeros_like(acc_ref)

    lane = lax.broadcasted_iota(jnp.int32, (tq, q_ref.shape[1]), 1)

    def kv_rows(t):
        return pl.ds(pl.multiple_of(lax.rem(t, n_kv) * tk, tk), tk)

    def scores(mp, t, par):
        q = q_ref[pl.ds(pl.multiple_of(lax.div(t, n_kv) * tq, tq), tq), :]
        keep = (lane < DA_HEAD_DIM) if mp == 0 else (lane >= DA_HEAD_DIM)
        q_map = jnp.where(keep, q, jnp.zeros_like(q))
        s_ref[mp, par] = lax.dot_general(k_ref[kv_rows(t), :], q_map, (((1,), (1,)), ((), ())),
                                         preferred_element_type=F32)

    def softmax(mp, t, par):
        m_old = jnp.where(lax.rem(t, n_kv) == 0, -jnp.inf, m_ref[mp])
        m_new = jnp.maximum(m_old, jnp.max(s_ref[mp, par], axis=0, keepdims=True))
        alpha_ref[mp] = jnp.exp2(m_old - m_new)
        m_ref[mp] = m_new
        p_ref[mp, par] = jnp.exp2(s_ref[mp, par] - m_new).astype(BF16)

    def values(mp, t, par):
        acc_ref[mp] = alpha_ref[mp] * acc_ref[mp] + jnp.dot(
            vt_ref[:, kv_rows(t)], p_ref[mp, par], preferred_element_type=F32)

    def finalize(qi):
        lam_p = lam_ref[...]
        lam = (jnp.exp(jnp.sum(lam_p[0:1, :] * lam_p[1:2, :], axis=-1, keepdims=True))
               - jnp.exp(jnp.sum(lam_p[2:3, :] * lam_p[3:4, :], axis=-1, keepdims=True)) + lambda_init)
        a0 = acc_ref[0]
        a1 = acc_ref[1]
        o = a0[0:dv] / a0[dv:dv + 1] - lam * (a1[0:dv] / a1[dv:dv + 1])
        o = o * lax.rsqrt(jnp.mean(o * o, axis=0, keepdims=True) + EPS) * g_ref[...]
        o_ref[pl.ds(pl.multiple_of(qi * tq, tq), tq), :] = (o * (1.0 - lambda_init)).T.astype(o_ref.dtype)

    def half_steps(t, par):
        scores(0, t + 1, 1 - par)
        values(0, t, par)
        softmax(1, t, par)
        scores(1, t + 1, 1 - par)
        values(1, t, par)
        softmax(0, t + 1, 1 - par)

    assert n_kv % 2 == 0 and n_tasks >= 4
    scores(0, 0, 0)
    scores(1, 0, 0)
    softmax(0, 0, 0)

    def body(u, carry):
        t = 2 * u
        half_steps(t, 0)
        half_steps(t + 1, 1)

        @pl.when(lax.rem(t + 1, n_kv) == n_kv - 1)
        def _():
            finalize(lax.div(t + 1, n_kv))

        return carry

    lax.fori_loop(0, n_tasks // 2 - 1, body, 0)
    half_steps(n_tasks - 2, 0)
    values(0, n_tasks - 1, 1)
    softmax(1, n_tasks - 1, 1)
    values(1, n_tasks - 1, 1)
    finalize(S // tq - 1)


def _diff_attn(qkv3, lam_p, g_col, lambda_init, tq, tk):
    B, S, _ = qkv3.shape
    H = DA_HEADS
    col = lambda off: pl.BlockSpec((None, S, LANES), lambda b, h: (b, 0, off + h))
    return pl.pallas_call(
        functools.partial(_diff_attn_kernel, lambda_init, tq, tk),
        grid=(B, H),
        in_specs=[col(0), col(H), col(2 * H), _resident(lam_p.shape), _resident(g_col.shape)],
        out_specs=col(0),
        out_shape=jax.ShapeDtypeStruct((B, S, H * DA_V_DIM), BF16),
        scratch_shapes=[pltpu.VMEM((DA_V_DIM + DEN_ROWS, S), BF16),
                        pltpu.VMEM((2, DA_V_DIM + DEN_ROWS, tq), F32),
                        pltpu.VMEM((2, 1, tq), F32), pltpu.VMEM((2, 1, tq), F32),
                        pltpu.VMEM((2, 2, tk, tq), F32), pltpu.VMEM((2, 2, tk, tq), BF16)],
        compiler_params=pltpu.CompilerParams(
            dimension_semantics=("parallel", "parallel"), vmem_limit_bytes=VMEM_LIMIT),
        name="diff_attn",
    )(qkv3, qkv3, qkv3, lam_p, g_col)


def _hgrn_chunk(q, kin, v, b, state_ref, reverse):
    C, n_sub = HG_CHUNK, HG_CHUNK // HG_SUB
    blk = lambda j: slice(j * HG_SUB, (j + 1) * HG_SUB)
    bound = [b[j * HG_SUB:j * HG_SUB + 1, :] if reverse else b[(j + 1) * HG_SUB - 1:(j + 1) * HG_SUB, :]
             for j in range(n_sub)]
    b_far = bound[0] if reverse else bound[-1]
    k_til = jnp.concatenate([kin[blk(j)] * jnp.exp(bound[j] - b[blk(j)]) for j in range(n_sub)], axis=0)
    rows = [slice(0, (j + 1) * HG_SUB) if reverse else slice(j * HG_SUB, C) for j in range(n_sub)]
    q_til = jnp.concatenate([q[rows[j]] * jnp.exp(b[rows[j]] - bound[j]) for j in range(n_sub)], axis=0)
    offs = np.cumsum([0] + [r.stop - r.start for r in rows])
    scores = lax.dot_general(q_til.astype(BF16), k_til.astype(BF16), (((1,), (1,)), ((), ())),
                             preferred_element_type=F32)
    col = lax.broadcasted_iota(jnp.int32, (HG_SUB, C), 1)
    row_blocks = []
    for i in range(n_sub):
        js = range(i, n_sub) if reverse else range(0, i + 1)
        a_i = jnp.zeros((HG_SUB, C), F32)
        for j in js:
            r0 = int(offs[j]) + (i * HG_SUB if reverse else (i - j) * HG_SUB)
            a_i = jnp.where(col // HG_SUB == j, scores[r0:r0 + HG_SUB, :], a_i)
        row_blocks.append(a_i)
    attn = jnp.concatenate(row_blocks, axis=0)
    t_i = lax.broadcasted_iota(jnp.int32, (C, C), 0)
    s_i = lax.broadcasted_iota(jnp.int32, (C, C), 1)
    attn = jnp.where((t_i <= s_i) if reverse else (t_i >= s_i), attn, 0.0)
    o_intra = jnp.dot(attn.astype(BF16), v, preferred_element_type=F32)
    state_t = state_ref[...]
    o_inter = lax.dot_general((q * jnp.exp(b)).astype(BF16), state_t.astype(BF16),
                              (((1,), (1,)), ((), ())), preferred_element_type=F32)
    k_dec = (kin * jnp.exp(b_far - b)).astype(BF16)
    state_ref[...] = jnp.exp(b_far) * state_t + lax.dot_general(
        v, k_dec, (((0,), (0,)), ((), ())), preferred_element_type=F32)
    return o_intra + o_inter


def _hgrn_kernel(q_f_ref, q_b_ref, v_f_ref, v_b_ref, bc_f_ref, bc_b_ref, kin_f_ref, kin_b_ref,
                 y_f_ref, y_b_ref, st_f_ref, st_b_ref):
    @pl.when(pl.program_id(2) == 0)
    def _():
        st_f_ref[...] = jnp.zeros_like(st_f_ref)
        st_b_ref[...] = jnp.zeros_like(st_b_ref)

    n_chunks = q_f_ref.shape[0] // HG_CHUNK

    def body(c, carry):
        for reverse, q_ref, v_ref, bc_ref, kin_ref, y_ref, st_ref in (
                (False, q_f_ref, v_f_ref, bc_f_ref, kin_f_ref, y_f_ref, st_f_ref),
                (True, q_b_ref, v_b_ref, bc_b_ref, kin_b_ref, y_b_ref, st_b_ref)):
            cc = (n_chunks - 1 - c) if reverse else c
            rs = pl.ds(pl.multiple_of(cc * HG_CHUNK, HG_CHUNK), HG_CHUNK)
            y = _hgrn_chunk(q_ref[rs, :].astype(F32), kin_ref[rs, :].astype(F32), v_ref[rs, :],
                            bc_ref[rs, :], st_ref, reverse)
            y_ref[rs, :] = y.astype(y_ref.dtype)
        return carry

    lax.fori_loop(0, n_chunks, body, 0)


def _hgrn(hq3, hv3, bc3, kin3, seg):
    B, S, W = hq3.shape
    H = W // HG_DIM
    n_seg = S // seg
    fwd = lambda off: pl.BlockSpec((None, seg, HG_DIM), lambda b, h, s: (b, s, off + h))
    bwd = lambda off: pl.BlockSpec((None, seg, HG_DIM), lambda b, h, s: (b, n_seg - 1 - s, off + h))
    return pl.pallas_call(
        _hgrn_kernel,
        grid=(B, H, n_seg),
        in_specs=[fwd(0), bwd(0), fwd(0), bwd(0), fwd(0), bwd(H), fwd(0), bwd(H)],
        out_specs=[fwd(0), bwd(0)],
        out_shape=[jax.ShapeDtypeStruct((B, S, W), BF16), jax.ShapeDtypeStruct((B, S, W), BF16)],
        scratch_shapes=[pltpu.VMEM((HG_DIM, HG_DIM), F32), pltpu.VMEM((HG_DIM, HG_DIM), F32)],
        compiler_params=pltpu.CompilerParams(
            dimension_semantics=("parallel", "parallel", "arbitrary"), vmem_limit_bytes=VMEM_LIMIT),
        name="hgrn2",
    )(hq3, hq3, hv3, hv3, bc3, bc3, kin3, kin3)


def _out_proj_kernel(x_ref, oda_ref, yf_ref, yb_ref, sg_ref, g_ref, w_ref, o_ref):
    half = oda_ref.shape[1]
    y = yf_ref[...].astype(F32) + yb_ref[...].astype(F32)
    sg = sg_ref[...].astype(F32)
    heads = []
    for hh in range(y.shape[1] // HG_DIM):
        cs = slice(hh * HG_DIM, (hh + 1) * HG_DIM)
        heads.append(_rms(y[:, cs], g_ref[...]) * sg[:, cs])
    o_hg = jnp.concatenate(heads, axis=1).astype(BF16)
    o_ref[...] = (x_ref[...]
                  + jnp.dot(oda_ref[...], w_ref[0:half, :], preferred_element_type=F32)
                  + jnp.dot(o_hg, w_ref[half:, :], preferred_element_type=F32))


def _out_proj(x2d, oda, yf, yb, sg, g, w_bf, tm):
    T, D = x2d.shape
    row = lambda w: pl.BlockSpec((tm, w), lambda i: (i, 0))
    return pl.pallas_call(
        _out_proj_kernel,
        grid=(T // tm,),
        in_specs=[row(D), row(oda.shape[1]), row(yf.shape[1]), row(yb.shape[1]), row(sg.shape[1]),
                  _resident(g.shape), _resident(w_bf.shape)],
        out_specs=row(D),
        out_shape=jax.ShapeDtypeStruct((T, D), F32),
        compiler_params=pltpu.CompilerParams(dimension_semantics=("parallel",),
                                             vmem_limit_bytes=VMEM_LIMIT),
        name="out_proj",
    )(x2d, oda, yf, yb, sg, g, w_bf)


HALO_ROWS = 16


def _ffn_ple_kernel(fc, x_ref, halo_ref, p_ref, gf_ref, wg_ref, wu_ref, cw_ref, cb_ref, wd_ref,
                    gp_ref, wpg_ref, wple_ref, gfin_ref, o_ref, h_ref, acc_ref):
    tm = x_ref.shape[0]
    xt = x_ref[...]
    h_ref[0:tm, :] = _rms(xt, gf_ref[...]).astype(BF16)
    h_ref[tm:tm + HALO_ROWS, :] = _rms(halo_ref[...], gf_ref[...]).astype(BF16)
    row = lax.broadcasted_iota(jnp.int32, (tm, fc), 0)
    n_fc = wg_ref.shape[1] // fc
    for c in range(n_fc):
        cs = slice(c * fc, (c + 1) * fc)
        a_ext = jnp.dot(h_ref[...], wg_ref[:, cs], preferred_element_type=F32)
        u = jnp.dot(h_ref[0:tm, :], wu_ref[:, cs], preferred_element_type=F32)
        a = a_ext[0:tm]
        a_prev = jnp.where(row == 0, a_ext[tm:tm + 1], pltpu.roll(a, 1, axis=0))
        a_next = jnp.where(row == tm - 1, a_ext[tm + 1:tm + 2], pltpu.roll(a, tm - 1, axis=0))
        cv = cb_ref[:, cs] + a_prev * cw_ref[0:1, cs] + a * cw_ref[1:2, cs] + a_next * cw_ref[2:3, cs]
        act = (_gelu(cv) * u).astype(BF16)
        d = jnp.dot(act, wd_ref[cs, :], preferred_element_type=F32)
        if c == 0:
            acc_ref[...] = d
        else:
            acc_ref[...] += d
    x2 = xt + acc_ref[...]
    gate = _sigmoid(jnp.dot(_rms(x2, gp_ref[...]).astype(BF16), wpg_ref[...], preferred_element_type=F32))
    ple = jnp.dot(p_ref[...].astype(BF16), wple_ref[...], preferred_element_type=F32)
    o_ref[...] = _rms(x2 + ple * gate, gfin_ref[...])


def _ffn_ple(x1, halo, p2d, gf, wg, wu, cw, cb, wd, gp, wpg, wple, gfin, tm, fc):
    T, D = x1.shape
    row = lambda w: pl.BlockSpec((tm, w), lambda i: (i, 0))
    consts = (gf, wg, wu, cw, cb, wd, gp, wpg, wple, gfin)
    return pl.pallas_call(
        functools.partial(_ffn_ple_kernel, fc),
        grid=(T // tm,),
        in_specs=[row(D), pl.BlockSpec((None, HALO_ROWS, D), lambda i: (i, 0, 0)), row(p2d.shape[1])]
                 + [_resident(c.shape) for c in consts],
        out_specs=row(D),
        out_shape=jax.ShapeDtypeStruct((T, D), F32),
        scratch_shapes=[pltpu.VMEM((tm + HALO_ROWS, D), BF16), pltpu.VMEM((tm, D), F32)],
        compiler_params=pltpu.CompilerParams(dimension_semantics=("parallel",),
                                             vmem_limit_bytes=VMEM_LIMIT),
        name="ffn_ple",
    )(x1, halo, p2d, *consts)


def _conv_halo(x1, tm, seq):
    T, D = x1.shape
    nt = T // tm
    xr = x1.reshape(nt, tm, D)
    zero = jnp.zeros((1, D), x1.dtype)
    prev = jnp.concatenate([zero, xr[:-1, tm - 1, :]], axis=0)
    nxt = jnp.concatenate([xr[1:, 0, :], zero], axis=0)
    start = (jnp.arange(nt) * tm) % seq
    prev = jnp.where((start == 0)[:, None], 0.0, prev)
    nxt = jnp.where((start + tm == seq)[:, None], 0.0, nxt)
    pad = jnp.zeros((nt, HALO_ROWS - 2, D), x1.dtype)
    return jnp.concatenate([prev[:, None, :], nxt[:, None, :], pad], axis=1)


def _rope_table():
    half = ROT_DIM // 2
    inv_freq = (np.float32(ROPE_THETA) ** (-np.arange(half, dtype=np.float32) / np.float32(half))).astype(np.float32)
    d = np.arange(LANES) % DA_HEAD_DIM
    tab = np.zeros((8, LANES), np.float32)
    tab[0] = np.where(d < ROT_DIM, inv_freq[d % half], 0.0)
    tab[1] = np.where(d < half, -1.0, 0.0)
    tab[2] = np.where((d >= half) & (d < ROT_DIM), 1.0, 0.0)
    return jnp.asarray(tab)


def kernel(x, p, positions, norm_mix_g, w_in, lam_q1, lam_k1, lam_q2, lam_k2, da_subln_g, hg_lb_gamma, hg_norm_g, w_out, norm_ffn_g, w_ffn_gate, w_ffn_up, ffn_conv_w, ffn_conv_b, w_ffn_down, norm_ple_g, w_ple, w_ple_gate, final_norm_g):
    B, S, D = x.shape
    T = B * S
    depth = w_in.shape[0]
    tm = min(512, S)
    tq = min(512, S)
    tk = min(512, S)
    seg = min(2048, S)
    fc = 256
    row2 = lambda v: v.reshape(1, -1)
    xc = x.reshape(T, D)
    pos = positions.reshape(T, 1)
    rope_tab = _rope_table()
    for i in range(depth):
        lambda_init = 0.8 - 0.6 * math.exp(-0.3 * i)
        gam = hg_lb_gamma.reshape(-1, hg_lb_gamma.shape[-1])
        qkv, hq, bc, kin, hv, sg = _in_proj(xc, pos, rope_tab, row2(norm_mix_g[i]), w_in[i].astype(BF16),
                                            gam, i, tm)
        lam_p = jnp.stack([lam_q1[i], lam_k1[i], lam_q2[i], lam_k2[i]], axis=0)
        o_da = _diff_attn(qkv.reshape(B, S, -1), lam_p, da_subln_g[i].reshape(-1, 1), lambda_init, tq, tk)
        y_f, y_b = _hgrn(hq.reshape(B, S, -1), hv.reshape(B, S, -1), bc.reshape(B, S, -1),
                         kin.reshape(B, S, -1), seg)
        x1 = _out_proj(xc, o_da.reshape(T, -1), y_f.reshape(T, -1), y_b.reshape(T, -1), sg,
                       row2(hg_norm_g[i]), w_out[i].astype(BF16), tm)
        last = i == depth - 1
        xc = _ffn_ple(x1, _conv_halo(x1, tm, S), p[i].reshape(T, -1), row2(norm_ffn_g[i]),
                      w_ffn_gate[i].astype(BF16), w_ffn_up[i].astype(BF16), ffn_conv_w[i],
                      row2(ffn_conv_b[i]), w_ffn_down[i].astype(BF16), row2(norm_ple_g[i]),
                      w_ple_gate[i].astype(BF16), w_ple[i].astype(BF16), row2(final_norm_g), tm, fc)
        assert last, "multi-layer stacks need the final norm split out of ffn_ple"
    return xc.reshape(B, S, D)
```

```python
import functools
import math

import numpy as np
import jax
import jax.numpy as jnp
from jax import lax
from jax.experimental import pallas as pl
from jax.experimental.pallas import tpu as pltpu

F32 = jnp.float32
BF16 = jnp.bfloat16

EPS = 1e-6
ROPE_THETA = 500000.0
DA_HEADS = 4
DA_HEAD_DIM = 64
DA_V_DIM = 128
ROT_DIM = 16
HG_HEADS = 4
HG_DIM = 128
CONV_WIDTH = 3
HG_CHUNK = 64
HG_SUB = 16
HG_CHUNKS_PER_TRIP = 4
LANES = 128
SUBLANES = 8
VMEM_LIMIT = 56 * 1024 * 1024


def _rms(xf, g):
    return xf * lax.rsqrt(jnp.mean(xf * xf, axis=-1, keepdims=True) + EPS) * g


def _sigmoid(z):
    return 1.0 / (1.0 + jnp.exp(-z))


def _gelu(x):
    return 0.5 * x * (1.0 + lax.erf(x * (2.0 ** -0.5)))


def _resident(shape):
    return pl.BlockSpec(shape, lambda *_: (0,) * len(shape), pipeline_mode=pl.Buffered(1))


def _chunk_cumsum(x, reverse):
    n = x.shape[0]
    r = lax.broadcasted_iota(jnp.int32, x.shape, 0) & (HG_CHUNK - 1)
    s = 1
    while s < SUBLANES:
        if reverse:
            x = x + jnp.where(r < HG_CHUNK - s, pltpu.roll(x, n - s, axis=0), 0.0)
        else:
            x = x + jnp.where(r >= s, pltpu.roll(x, s, axis=0), 0.0)
        s *= 2
    while s < HG_CHUNK:
        parts = []
        for c0 in range(0, n, HG_CHUNK):
            lo, hi = x[c0:c0 + HG_CHUNK - s], x[c0 + s:c0 + HG_CHUNK]
            parts += [lo + hi, x[c0 + HG_CHUNK - s:c0 + HG_CHUNK]] if reverse else [x[c0:c0 + s], hi + lo]
        x = jnp.concatenate(parts, axis=0)
        s *= 2
    return x


def _in_proj_kernel(layer, n_slots, x_ref, pos_ref, freq_ref, place_ref, g_ref, w_ref, gam_ref,
                    qkv_ref, hq_ref, bc_ref, kin_ref, hv_ref, sg_ref):
    h = _rms(x_ref[...], g_ref[...]).astype(BF16)
    sec = qkv_ref.shape[1] // 3

    def proj(i):
        return jnp.dot(h, w_ref[:, i * sec:(i + 1) * sec], preferred_element_type=F32)

    half = ROT_DIM // 2
    ang = freq_ref[...] * pos_ref[...].astype(F32)
    parts = []
    for trig in (jnp.cos(ang), jnp.sin(ang)):
        hi = trig.astype(BF16).astype(F32)
        parts += [hi, trig - hi]
    tab = lax.dot_general(jnp.concatenate(parts, axis=0).astype(BF16), place_ref[...],
                          (((0,), (0,)), ((), ())), preferred_element_type=F32)
    lane_d = lax.broadcasted_iota(jnp.int32, (1, LANES), 1) & (DA_HEAD_DIM - 1)
    cos = tab[:, :LANES] + jnp.where(lane_d >= ROT_DIM, 1.0, 0.0)
    sin = tab[:, LANES:]
    first_half = lane_d < half

    def rotary(t, scale):
        outs = []
        for hh in range(sec // LANES):
            blk = t[:, hh * LANES:(hh + 1) * LANES]
            partner = jnp.where(first_half, pltpu.roll(blk, LANES - half, axis=1),
                                pltpu.roll(blk, half, axis=1))
            rot = blk * cos + partner * sin
            outs.append(rot * scale if scale != 1.0 else rot)
        return jnp.concatenate(outs, axis=1)

    qkv_ref[:, 0:sec] = rotary(proj(0), DA_HEAD_DIM ** -0.5 * math.log2(math.e)).astype(BF16)
    qkv_ref[:, sec:2 * sec] = rotary(proj(1), 1.0).astype(BF16)
    qkv_ref[:, 2 * sec:3 * sec] = proj(2).astype(BF16)
    hq_ref[...] = proj(3).astype(BF16)

    for d in range(2):
        gam = gam_ref[d * n_slots:(d + 1) * n_slots, :]
        e = jnp.exp(gam - jnp.max(gam, axis=0, keepdims=True))
        lb = jnp.sum(e[0:layer + 1, :], axis=0, keepdims=True) / jnp.sum(e, axis=0, keepdims=True)
        sig = _sigmoid(proj(4 + d))
        logf = jnp.log(lb + (1.0 - lb) * sig)
        bc_ref[:, d * sec:(d + 1) * sec] = _chunk_cumsum(logf, reverse=(d == 1))
        kin_ref[:, d * sec:(d + 1) * sec] = ((1.0 - lb) * (1.0 - sig)).astype(BF16)

    hv_ref[...] = proj(6).astype(BF16)
    gate = proj(7)
    sg_ref[...] = (gate * _sigmoid(gate)).astype(BF16)


def _in_proj(x2d, pos, freq, place, g, w_bf, gam, layer, tm):
    T, D = x2d.shape
    sec = w_bf.shape[1] // 8
    n_slots = gam.shape[0] // 2
    row = lambda w: pl.BlockSpec((tm, w), lambda i: (i, 0))
    return pl.pallas_call(
        functools.partial(_in_proj_kernel, layer, n_slots),
        grid=(T // tm,),
        in_specs=[row(D), pl.BlockSpec((None, 1, tm), lambda i: (i, 0, 0)),
                  _resident(freq.shape), _resident(place.shape), _resident(g.shape),
                  _resident(w_bf.shape), _resident(gam.shape)],
        out_specs=[row(3 * sec), row(sec), row(2 * sec), row(2 * sec), row(sec), row(sec)],
        out_shape=[jax.ShapeDtypeStruct((T, 3 * sec), BF16), jax.ShapeDtypeStruct((T, sec), BF16),
                   jax.ShapeDtypeStruct((T, 2 * sec), F32), jax.ShapeDtypeStruct((T, 2 * sec), BF16),
                   jax.ShapeDtypeStruct((T, sec), BF16), jax.ShapeDtypeStruct((T, sec), BF16)],
        compiler_params=pltpu.CompilerParams(dimension_semantics=("parallel",),
                                             vmem_limit_bytes=VMEM_LIMIT),
        name="in_proj",
    )(x2d, pos, freq, place, g, w_bf, gam)


DEN_ROWS = 16
ATTN_TASKS_PER_TRIP = 4


def _diff_attn_kernel(lambda_init, tq, tk, q_ref, k_ref, v_ref, lam_ref, g_ref, o_ref,
                      vt_ref, qt_ref, acc_ref, m_ref, alpha_ref, smax_ref, s_ref, p_ref):
    S = k_ref.shape[0]
    n_kv = S // tk
    n_tasks = (S // tq) * n_kv
    dv = v_ref.shape[1]

    feat = lax.broadcasted_iota(jnp.int32, (q_ref.shape[1], tk), 0)

    def transpose_qv(i, carry):
        r = pl.ds(pl.multiple_of(i * tk, tk), tk)
        vt_ref[0:dv, r] = v_ref[r, :].astype(F32).T.astype(BF16)
        q_t = q_ref[r, :].astype(F32).T
        qt_ref[0, :, r] = jnp.where(feat < DA_HEAD_DIM, q_t, 0.0).astype(BF16)
        qt_ref[1, :, r] = jnp.where(feat >= DA_HEAD_DIM, q_t, 0.0).astype(BF16)
        return carry

    lax.fori_loop(0, n_kv, transpose_qv, 0)
    vt_ref[dv:dv + DEN_ROWS, :] = jnp.ones((DEN_ROWS, S), BF16)
    acc_ref[...] = jnp.zeros_like(acc_ref)

    def kv_rows(t):
        return pl.ds(pl.multiple_of(lax.rem(t, n_kv) * tk, tk), tk)

    def scores(mp, t, par):
        q_cols = pl.ds(pl.multiple_of(lax.div(t, n_kv) * tq, tq), tq)
        s = jnp.dot(k_ref[kv_rows(t), :], qt_ref[mp, :, q_cols], preferred_element_type=F32)
        s_ref[mp, par] = s
        smax_ref[mp, par] = jnp.max(s, axis=0, keepdims=True)

    def softmax(mp, t, par):
        m_old = jnp.where(lax.rem(t, n_kv) == 0, -jnp.inf, m_ref[mp])
        m_new = jnp.maximum(m_old, smax_ref[mp, par])
        alpha_ref[mp] = jnp.exp2(m_old - m_new)
        m_ref[mp] = m_new
        p_ref[mp, par] = jnp.exp2((s_ref[mp, par] - m_new).astype(BF16))

    def values(mp, t, par):
        acc_ref[mp] = alpha_ref[mp] * acc_ref[mp] + jnp.dot(
            vt_ref[:, kv_rows(t)], p_ref[mp, par], preferred_element_type=F32)

    def finalize(qi):
        lam_p = lam_ref[...]
        lam = (jnp.exp(jnp.sum(lam_p[0:1, :] * lam_p[1:2, :], axis=-1, keepdims=True))
               - jnp.exp(jnp.sum(lam_p[2:3, :] * lam_p[3:4, :], axis=-1, keepdims=True)) + lambda_init)
        a0 = acc_ref[0]
        a1 = acc_ref[1]
        o = a0[0:dv] / a0[dv:dv + 1] - lam * (a1[0:dv] / a1[dv:dv + 1])
        o = o * lax.rsqrt(jnp.mean(o * o, axis=0, keepdims=True) + EPS) * g_ref[...]
        o_ref[pl.ds(pl.multiple_of(qi * tq, tq), tq), :] = (o * (1.0 - lambda_init)).T.astype(o_ref.dtype)

    def half_steps(t, par):
        scores(0, t + 1, 1 - par)
        values(0, t, par)
        softmax(1, t, par)
        scores(1, t + 1, 1 - par)
        values(1, t, par)
        softmax(0, t + 1, 1 - par)

    unroll = ATTN_TASKS_PER_TRIP
    assert unroll % 2 == 0 and n_kv % unroll == 0 and n_tasks >= 2 * unroll
    scores(0, 0, 0)
    scores(1, 0, 0)
    softmax(0, 0, 0)

    def body(u, carry):
        t = unroll * u
        for i in range(unroll):
            half_steps(t + i, i % 2)

        @pl.when(lax.rem(t + unroll - 1, n_kv) == n_kv - 1)
        def _():
            finalize(lax.div(t, n_kv))

        return carry

    lax.fori_loop(0, n_tasks // unroll - 1, body, 0)
    for i in range(unroll - 1):
        half_steps(n_tasks - unroll + i, i % 2)
    values(0, n_tasks - 1, 1)
    softmax(1, n_tasks - 1, 1)
    values(1, n_tasks - 1, 1)
    finalize(S // tq - 1)


def _diff_attn(qkv3, lam_p, g_col, lambda_init, tq, tk):
    B, S, _ = qkv3.shape
    H = DA_HEADS
    col = lambda off: pl.BlockSpec((None, S, LANES), lambda b, h: (b, 0, off + h))
    return pl.pallas_call(
        functools.partial(_diff_attn_kernel, lambda_init, tq, tk),
        grid=(B, H),
        in_specs=[col(0), col(H), col(2 * H), _resident(lam_p.shape), _resident(g_col.shape)],
        out_specs=col(0),
        out_shape=jax.ShapeDtypeStruct((B, S, H * DA_V_DIM), BF16),
        scratch_shapes=[pltpu.VMEM((DA_V_DIM + DEN_ROWS, S), BF16),
                        pltpu.VMEM((2, LANES, S), BF16),
                        pltpu.VMEM((2, DA_V_DIM + DEN_ROWS, tq), F32),
                        pltpu.VMEM((2, 1, tq), F32), pltpu.VMEM((2, 1, tq), F32),
                        pltpu.VMEM((2, 2, 1, tq), F32),
                        pltpu.VMEM((2, 2, tk, tq), F32), pltpu.VMEM((2, 2, tk, tq), BF16)],
        compiler_params=pltpu.CompilerParams(
            dimension_semantics=("parallel", "parallel"), vmem_limit_bytes=VMEM_LIMIT),
        name="diff_attn",
    )(qkv3, qkv3, qkv3, lam_p, g_col)


def _hgrn_scores(q, kin, v, b, reverse):
    C, n_sub = HG_CHUNK, HG_CHUNK // HG_SUB
    blk = lambda j: slice(j * HG_SUB, (j + 1) * HG_SUB)
    bound = [b[j * HG_SUB:j * HG_SUB + 1, :] if reverse else b[(j + 1) * HG_SUB - 1:(j + 1) * HG_SUB, :]
             for j in range(n_sub)]
    b_far = bound[0] if reverse else bound[-1]
    k_til = jnp.concatenate([kin[blk(j)] * jnp.exp(bound[j] - b[blk(j)]) for j in range(n_sub)], axis=0)
    rows = [slice(0, (j + 1) * HG_SUB) if reverse else slice(j * HG_SUB, C) for j in range(n_sub)]
    q_til = jnp.concatenate([q[rows[j]] * jnp.exp(b[rows[j]] - bound[j]) for j in range(n_sub)], axis=0)
    offs = np.cumsum([0] + [r.stop - r.start for r in rows])
    scores = lax.dot_general(q_til.astype(BF16), k_til.astype(BF16), (((1,), (1,)), ((), ())),
                             preferred_element_type=F32)
    k_dec = (kin * jnp.exp(b_far - b)).astype(BF16)
    increment = lax.dot_general(v, k_dec, (((0,), (0,)), ((), ())), preferred_element_type=F32)
    return scores, [int(o) for o in offs], (q * jnp.exp(b)).astype(BF16), jnp.exp(b_far), increment


def _hgrn_intra(scores, offs, v, reverse):
    C, n_sub = HG_CHUNK, HG_CHUNK // HG_SUB
    col = lax.broadcasted_iota(jnp.int32, (HG_SUB, C), 1)
    row_blocks = []
    for i in range(n_sub):
        js = range(i, n_sub) if reverse else range(0, i + 1)
        a_i = jnp.zeros((HG_SUB, C), F32)
        for j in js:
            r0 = offs[j] + (i * HG_SUB if reverse else (i - j) * HG_SUB)
            a_i = jnp.where(col // HG_SUB == j, scores[r0:r0 + HG_SUB, :], a_i)
        row_blocks.append(a_i)
    attn = jnp.concatenate(row_blocks, axis=0)
    t_i = lax.broadcasted_iota(jnp.int32, (C, C), 0)
    s_i = lax.broadcasted_iota(jnp.int32, (C, C), 1)
    attn = jnp.where((t_i <= s_i) if reverse else (t_i >= s_i), attn, 0.0)
    return jnp.dot(attn.astype(BF16), v, preferred_element_type=F32)


def _hgrn_kernel(q_f_ref, q_b_ref, v_f_ref, v_b_ref, bc_f_ref, bc_b_ref, kin_f_ref, kin_b_ref,
                 y_f_ref, y_b_ref, st_f_ref, st_b_ref):
    @pl.when(pl.program_id(2) == 0)
    def _():
        st_f_ref[...] = jnp.zeros_like(st_f_ref)
        st_b_ref[...] = jnp.zeros_like(st_b_ref)

    n_chunks = q_f_ref.shape[0] // HG_CHUNK
    directions = ((False, q_f_ref, v_f_ref, bc_f_ref, kin_f_ref, y_f_ref, st_f_ref),
                  (True, q_b_ref, v_b_ref, bc_b_ref, kin_b_ref, y_b_ref, st_b_ref))

    def body(u, carry):
        work = []
        for reverse, q_ref, v_ref, bc_ref, kin_ref, y_ref, st_ref in directions:
            for j in range(HG_CHUNKS_PER_TRIP):
                c = u * HG_CHUNKS_PER_TRIP + j
                cc = (n_chunks - 1 - c) if reverse else c
                rs = pl.ds(pl.multiple_of(cc * HG_CHUNK, HG_CHUNK), HG_CHUNK)
                v = v_ref[rs, :]
                stage1 = _hgrn_scores(q_ref[rs, :].astype(F32), kin_ref[rs, :].astype(F32), v,
                                      bc_ref[rs, :], reverse)
                work.append((reverse, rs, v, y_ref, st_ref, stage1))
        intra = [_hgrn_intra(w[5][0], w[5][1], w[2], w[0]) for w in work]
        states = {}
        for (reverse, rs, v, y_ref, st_ref, stage1), o_intra in zip(work, intra):
            _, _, q_state, decay, increment = stage1
            state_t = states.get(reverse)
            if state_t is None:
                state_t = st_ref[...]
            o_inter = lax.dot_general(q_state, state_t.astype(BF16), (((1,), (1,)), ((), ())),
                                      preferred_element_type=F32)
            states[reverse] = decay * state_t + increment
            y_ref[rs, :] = (o_intra + o_inter).astype(y_ref.dtype)
        for reverse, *_, st_ref in directions:
            st_ref[...] = states[reverse]
        return carry

    assert n_chunks % HG_CHUNKS_PER_TRIP == 0
    lax.fori_loop(0, n_chunks // HG_CHUNKS_PER_TRIP, body, 0)


def _hgrn(hq3, hv3, bc3, kin3, seg):
    B, S, W = hq3.shape
    H = W // HG_DIM
    n_seg = S // seg
    fwd = lambda off: pl.BlockSpec((None, seg, HG_DIM), lambda b, h, s: (b, s, off + h))
    bwd = lambda off: pl.BlockSpec((None, seg, HG_DIM), lambda b, h, s: (b, n_seg - 1 - s, off + h))
    return pl.pallas_call(
        _hgrn_kernel,
        grid=(B, H, n_seg),
        in_specs=[fwd(0), bwd(0), fwd(0), bwd(0), fwd(0), bwd(H), fwd(0), bwd(H)],
        out_specs=[fwd(0), bwd(0)],
        out_shape=[jax.ShapeDtypeStruct((B, S, W), BF16), jax.ShapeDtypeStruct((B, S, W), BF16)],
        scratch_shapes=[pltpu.VMEM((HG_DIM, HG_DIM), F32), pltpu.VMEM((HG_DIM, HG_DIM), F32)],
        compiler_params=pltpu.CompilerParams(
            dimension_semantics=("parallel", "parallel", "arbitrary"), vmem_limit_bytes=VMEM_LIMIT),
        name="hgrn2",
    )(hq3, hq3, hv3, hv3, bc3, bc3, kin3, kin3)


def _out_proj_kernel(x_ref, oda_ref, yf_ref, yb_ref, sg_ref, g_ref, w_ref, o_ref):
    half = oda_ref.shape[1]
    y = yf_ref[...].astype(F32) + yb_ref[...].astype(F32)
    sg = sg_ref[...].astype(F32)
    heads = []
    for hh in range(y.shape[1] // HG_DIM):
        cs = slice(hh * HG_DIM, (hh + 1) * HG_DIM)
        heads.append(_rms(y[:, cs], g_ref[...]) * sg[:, cs])
    o_hg = jnp.concatenate(heads, axis=1).astype(BF16)
    o_ref[...] = (x_ref[...]
                  + jnp.dot(oda_ref[...], w_ref[0:half, :], preferred_element_type=F32)
                  + jnp.dot(o_hg, w_ref[half:, :], preferred_element_type=F32))


def _out_proj(x2d, oda, yf, yb, sg, g, w_bf, tm):
    T, D = x2d.shape
    row = lambda w: pl.BlockSpec((tm, w), lambda i: (i, 0))
    return pl.pallas_call(
        _out_proj_kernel,
        grid=(T // tm,),
        in_specs=[row(D), row(oda.shape[1]), row(yf.shape[1]), row(yb.shape[1]), row(sg.shape[1]),
                  _resident(g.shape), _resident(w_bf.shape)],
        out_specs=row(D),
        out_shape=jax.ShapeDtypeStruct((T, D), F32),
        compiler_params=pltpu.CompilerParams(dimension_semantics=("parallel",),
                                             vmem_limit_bytes=VMEM_LIMIT),
        name="out_proj",
    )(x2d, oda, yf, yb, sg, g, w_bf)


HALO_ROWS = 16


def _ffn_ple_kernel(fc, x_ref, halo_ref, p_ref, gf_ref, wg_ref, wu_ref, cw_ref, cb_ref, wd_ref,
                    gp_ref, wpg_ref, wple_ref, gfin_ref, o_ref, h_ref, acc_ref):
    tm = x_ref.shape[0]
    xt = x_ref[...]
    h_ref[0:tm, :] = _rms(xt, gf_ref[...]).astype(BF16)
    h_ref[tm:tm + HALO_ROWS, :] = _rms(halo_ref[...], gf_ref[...]).astype(BF16)
    row = lax.broadcasted_iota(jnp.int32, (tm, fc), 0)
    n_fc = wg_ref.shape[1] // fc
    for c in range(n_fc):
        cs = slice(c * fc, (c + 1) * fc)
        a_ext = jnp.dot(h_ref[...], wg_ref[:, cs], preferred_element_type=F32)
        u = jnp.dot(h_ref[0:tm, :], wu_ref[:, cs], preferred_element_type=F32)
        a = a_ext[0:tm]
        a_prev = jnp.where(row == 0, a_ext[tm:tm + 1], pltpu.roll(a, 1, axis=0))
        a_next = jnp.where(row == tm - 1, a_ext[tm + 1:tm + 2], pltpu.roll(a, tm - 1, axis=0))
        cv = cb_ref[:, cs] + a_prev * cw_ref[0:1, cs] + a * cw_ref[1:2, cs] + a_next * cw_ref[2:3, cs]
        act = (_gelu(cv) * u).astype(BF16)
        d = jnp.dot(act, wd_ref[cs, :], preferred_element_type=F32)
        if c == 0:
            acc_ref[...] = d
        else:
            acc_ref[...] += d
    x2 = xt + acc_ref[...]
    gate = _sigmoid(jnp.dot(_rms(x2, gp_ref[...]).astype(BF16), wpg_ref[...], preferred_element_type=F32))
    ple = jnp.dot(p_ref[...].astype(BF16), wple_ref[...], preferred_element_type=F32)
    o_ref[...] = _rms(x2 + ple * gate, gfin_ref[...])


def _ffn_ple(x1, halo, p2d, gf, wg, wu, cw, cb, wd, gp, wpg, wple, gfin, tm, fc):
    T, D = x1.shape
    row = lambda w: pl.BlockSpec((tm, w), lambda i: (i, 0))
    consts = (gf, wg, wu, cw, cb, wd, gp, wpg, wple, gfin)
    return pl.pallas_call(
        functools.partial(_ffn_ple_kernel, fc),
        grid=(T // tm,),
        in_specs=[row(D), pl.BlockSpec((None, HALO_ROWS, D), lambda i: (i, 0, 0)), row(p2d.shape[1])]
                 + [_resident(c.shape) for c in consts],
        out_specs=row(D),
        out_shape=jax.ShapeDtypeStruct((T, D), F32),
        scratch_shapes=[pltpu.VMEM((tm + HALO_ROWS, D), BF16), pltpu.VMEM((tm, D), F32)],
        compiler_params=pltpu.CompilerParams(dimension_semantics=("parallel",),
                                             vmem_limit_bytes=VMEM_LIMIT),
        name="ffn_ple",
    )(x1, halo, p2d, *consts)


def _conv_halo(x1, tm, seq):
    T, D = x1.shape
    nt = T // tm
    xr = x1.reshape(nt, tm, D)
    zero = jnp.zeros((1, D), x1.dtype)
    prev = jnp.concatenate([zero, xr[:-1, tm - 1, :]], axis=0)
    nxt = jnp.concatenate([xr[1:, 0, :], zero], axis=0)
    start = (jnp.arange(nt) * tm) % seq
    prev = jnp.where((start == 0)[:, None], 0.0, prev)
    nxt = jnp.where((start + tm == seq)[:, None], 0.0, nxt)
    pad = jnp.zeros((nt, HALO_ROWS - 2, D), x1.dtype)
    return jnp.concatenate([prev[:, None, :], nxt[:, None, :], pad], axis=1)


def _rope_constants():
    half = ROT_DIM // 2
    inv_freq = (np.float32(ROPE_THETA) ** (-np.arange(half, dtype=np.float32) / np.float32(half))).astype(np.float32)
    d = np.arange(LANES) % DA_HEAD_DIM
    place = np.zeros((4 * half, 2 * LANES), np.float32)
    for f in range(half):
        hit = (d < ROT_DIM) & (d % half == f)
        sign = np.where(d < half, -1.0, 1.0)
        place[f, :LANES] = place[half + f, :LANES] = hit
        place[2 * half + f, LANES:] = place[3 * half + f, LANES:] = hit * sign
    return jnp.asarray(inv_freq.reshape(half, 1)), jnp.asarray(place, dtype=BF16)


def kernel(x, p, positions, norm_mix_g, w_in, lam_q1, lam_k1, lam_q2, lam_k2, da_subln_g, hg_lb_gamma, hg_norm_g, w_out, norm_ffn_g, w_ffn_gate, w_ffn_up, ffn_conv_w, ffn_conv_b, w_ffn_down, norm_ple_g, w_ple, w_ple_gate, final_norm_g):
    B, S, D = x.shape
    T = B * S
    depth = w_in.shape[0]
    tm = min(512, S)
    tq = min(512, S)
    tk = min(256, S)
    seg = min(2048, S)
    fc = 256
    row2 = lambda v: v.reshape(1, -1)
    xc = x.reshape(T, D)
    pos = positions.reshape(T // tm, 1, tm)
    rope_freq, rope_place = _rope_constants()
    for i in range(depth):
        lambda_init = 0.8 - 0.6 * math.exp(-0.3 * i)
        gam = hg_lb_gamma.reshape(-1, hg_lb_gamma.shape[-1])
        qkv, hq, bc, kin, hv, sg = _in_proj(xc, pos, rope_freq, rope_place, row2(norm_mix_g[i]), w_in[i].astype(BF16),
                                            gam, i, tm)
        lam_p = jnp.stack([lam_q1[i], lam_k1[i], lam_q2[i], lam_k2[i]], axis=0)
        o_da = _diff_attn(qkv.reshape(B, S, -1), lam_p, da_subln_g[i].reshape(-1, 1), lambda_init, tq, tk)
        y_f, y_b = _hgrn(hq.reshape(B, S, -1), hv.reshape(B, S, -1), bc.reshape(B, S, -1),
                         kin.reshape(B, S, -1), seg)
        x1 = _out_proj(xc, o_da.reshape(T, -1), y_f.reshape(T, -1), y_b.reshape(T, -1), sg,
                       row2(hg_norm_g[i]), w_out[i].astype(BF16), tm)
        last = i == depth - 1
        xc = _ffn_ple(x1, _conv_halo(x1, tm, S), p[i].reshape(T, -1), row2(norm_ffn_g[i]),
                      w_ffn_gate[i].astype(BF16), w_ffn_up[i].astype(BF16), ffn_conv_w[i],
                      row2(ffn_conv_b[i]), w_ffn_down[i].astype(BF16), row2(norm_ple_g[i]),
                      w_ple_gate[i].astype(BF16), w_ple[i].astype(BF16), row2(final_norm_g), tm, fc)
        assert last, "multi-layer stacks need the final norm split out of ffn_ple"
    return xc.reshape(B, S, D)
```

```python
import functools
import math

import numpy as np
import jax
import jax.numpy as jnp
from jax import lax
from jax.experimental import pallas as pl
from jax.experimental.pallas import tpu as pltpu

F32 = jnp.float32
BF16 = jnp.bfloat16

EPS = 1e-6
ROPE_THETA = 500000.0
DA_HEADS = 4
DA_HEAD_DIM = 64
DA_V_DIM = 128
ROT_DIM = 16
HG_HEADS = 4
HG_DIM = 128
CONV_WIDTH = 3
HG_CHUNK = 64
HG_SUB = 16
HG_CHUNKS_PER_TRIP = 4
LANES = 128
SUBLANES = 8
VMEM_LIMIT = 56 * 1024 * 1024


def _rms(xf, g):
    return xf * lax.rsqrt(jnp.mean(xf * xf, axis=-1, keepdims=True) + EPS) * g


def _sigmoid(z):
    return 1.0 / (1.0 + jnp.exp(-z))


def _gelu(x):
    return 0.5 * x * (1.0 + lax.erf(x * (2.0 ** -0.5)))


def _resident(shape):
    return pl.BlockSpec(shape, lambda *_: (0,) * len(shape), pipeline_mode=pl.Buffered(1))


def _chunk_cumsum(x, reverse):
    n = x.shape[0]
    r = lax.broadcasted_iota(jnp.int32, x.shape, 0) & (HG_CHUNK - 1)
    s = 1
    while s < SUBLANES:
        if reverse:
            x = x + jnp.where(r < HG_CHUNK - s, pltpu.roll(x, n - s, axis=0), 0.0)
        else:
            x = x + jnp.where(r >= s, pltpu.roll(x, s, axis=0), 0.0)
        s *= 2
    while s < HG_CHUNK:
        parts = []
        for c0 in range(0, n, HG_CHUNK):
            lo, hi = x[c0:c0 + HG_CHUNK - s], x[c0 + s:c0 + HG_CHUNK]
            parts += [lo + hi, x[c0 + HG_CHUNK - s:c0 + HG_CHUNK]] if reverse else [x[c0:c0 + s], hi + lo]
        x = jnp.concatenate(parts, axis=0)
        s *= 2
    return x


def _in_proj_kernel(layer, n_slots, x_ref, pos_ref, freq_ref, place_ref, g_ref, w_ref, gam_ref,
                    qkv_ref, hq_ref, bc_ref, kin_ref, hv_ref, sg_ref):
    h = _rms(x_ref[...], g_ref[...]).astype(BF16)
    sec = qkv_ref.shape[1] // 3

    def proj(i):
        return jnp.dot(h, w_ref[:, i * sec:(i + 1) * sec], preferred_element_type=F32)

    half = ROT_DIM // 2
    ang = freq_ref[...] * pos_ref[...].astype(F32)
    parts = []
    for trig in (jnp.cos(ang), jnp.sin(ang)):
        hi = trig.astype(BF16).astype(F32)
        parts += [hi, trig - hi]
    tab = lax.dot_general(jnp.concatenate(parts, axis=0).astype(BF16), place_ref[...],
                          (((0,), (0,)), ((), ())), preferred_element_type=F32)
    lane_d = lax.broadcasted_iota(jnp.int32, (1, LANES), 1) & (DA_HEAD_DIM - 1)
    cos = tab[:, :LANES] + jnp.where(lane_d >= ROT_DIM, 1.0, 0.0)
    sin = tab[:, LANES:]
    first_half = lane_d < half

    def rotary(t, scale):
        outs = []
        for hh in range(sec // LANES):
            blk = t[:, hh * LANES:(hh + 1) * LANES]
            partner = jnp.where(first_half, pltpu.roll(blk, LANES - half, axis=1),
                                pltpu.roll(blk, half, axis=1))
            rot = blk * cos + partner * sin
            outs.append(rot * scale if scale != 1.0 else rot)
        return jnp.concatenate(outs, axis=1)

    qkv_ref[:, 0:sec] = rotary(proj(0), DA_HEAD_DIM ** -0.5 * math.log2(math.e)).astype(BF16)
    qkv_ref[:, sec:2 * sec] = rotary(proj(1), 1.0).astype(BF16)
    qkv_ref[:, 2 * sec:3 * sec] = proj(2).astype(BF16)
    hq_ref[...] = proj(3).astype(BF16)

    for d in range(2):
        gam = gam_ref[d * n_slots:(d + 1) * n_slots, :]
        e = jnp.exp(gam - jnp.max(gam, axis=0, keepdims=True))
        lb = jnp.sum(e[0:layer + 1, :], axis=0, keepdims=True) / jnp.sum(e, axis=0, keepdims=True)
        sig = _sigmoid(proj(4 + d))
        logf = jnp.log(lb + (1.0 - lb) * sig)
        bc_ref[:, d * sec:(d + 1) * sec] = _chunk_cumsum(logf, reverse=(d == 1))
        kin_ref[:, d * sec:(d + 1) * sec] = ((1.0 - lb) * (1.0 - sig)).astype(BF16)

    hv_ref[...] = proj(6).astype(BF16)
    gate = proj(7)
    sg_ref[...] = (gate * _sigmoid(gate)).astype(BF16)


def _in_proj(x2d, pos, freq, place, g, w_bf, gam, layer, tm):
    T, D = x2d.shape
    sec = w_bf.shape[1] // 8
    n_slots = gam.shape[0] // 2
    row = lambda w: pl.BlockSpec((tm, w), lambda i: (i, 0))
    return pl.pallas_call(
        functools.partial(_in_proj_kernel, layer, n_slots),
        grid=(T // tm,),
        in_specs=[row(D), pl.BlockSpec((None, 1, tm), lambda i: (i, 0, 0)),
                  _resident(freq.shape), _resident(place.shape), _resident(g.shape),
                  _resident(w_bf.shape), _resident(gam.shape)],
        out_specs=[row(3 * sec), row(sec), row(2 * sec), row(2 * sec), row(sec), row(sec)],
        out_shape=[jax.ShapeDtypeStruct((T, 3 * sec), BF16), jax.ShapeDtypeStruct((T, sec), BF16),
                   jax.ShapeDtypeStruct((T, 2 * sec), F32), jax.ShapeDtypeStruct((T, 2 * sec), BF16),
                   jax.ShapeDtypeStruct((T, sec), BF16), jax.ShapeDtypeStruct((T, sec), BF16)],
        compiler_params=pltpu.CompilerParams(dimension_semantics=("parallel",),
                                             vmem_limit_bytes=VMEM_LIMIT),
        name="in_proj",
    )(x2d, pos, freq, place, g, w_bf, gam)


DEN_ROWS = 16
ATTN_TASKS_PER_TRIP = 4


def _diff_attn_kernel(lambda_init, tq, tk, q_ref, k_ref, v_ref, lam_ref, g_ref, o_ref,
                      vt_ref, qt_ref, acc_ref, m_ref, smax_ref, s_ref):
    S = k_ref.shape[0]
    n_kv = S // tk
    n_tasks = (S // tq) * n_kv
    dv = v_ref.shape[1]

    feat = lax.broadcasted_iota(jnp.int32, (q_ref.shape[1], tk), 0)

    def transpose_qv(i, carry):
        r = pl.ds(pl.multiple_of(i * tk, tk), tk)
        vt_ref[0:dv, r] = v_ref[r, :].astype(F32).T.astype(BF16)
        q_t = q_ref[r, :].astype(F32).T
        qt_ref[0, :, r] = jnp.where(feat < DA_HEAD_DIM, q_t, 0.0).astype(BF16)
        qt_ref[1, :, r] = jnp.where(feat >= DA_HEAD_DIM, q_t, 0.0).astype(BF16)
        return carry

    lax.fori_loop(0, n_kv, transpose_qv, 0)
    vt_ref[dv:dv + DEN_ROWS, :] = jnp.ones((DEN_ROWS, S), BF16)
    acc_ref[...] = jnp.zeros_like(acc_ref)

    def kv_rows(t):
        return pl.ds(pl.multiple_of(lax.rem(t, n_kv) * tk, tk), tk)

    def scores(mp, t, par):
        q_cols = pl.ds(pl.multiple_of(lax.div(t, n_kv) * tq, tq), tq)
        s = jnp.dot(k_ref[kv_rows(t), :], qt_ref[mp, :, q_cols], preferred_element_type=F32)
        s_ref[mp, par] = s
        smax_ref[mp, par] = jnp.max(s, axis=0, keepdims=True)

    def attend(mp, t, par):
        m_old = jnp.where(lax.rem(t, n_kv) == 0, -jnp.inf, m_ref[mp])
        m_new = jnp.maximum(m_old, smax_ref[mp, par])
        m_ref[mp] = m_new
        p = jnp.exp2((s_ref[mp, par] - m_new).astype(BF16))
        acc_ref[mp] = jnp.exp2(m_old - m_new) * acc_ref[mp] + jnp.dot(
            vt_ref[:, kv_rows(t)], p, preferred_element_type=F32)

    def finalize(qi):
        lam_p = lam_ref[...]
        lam = (jnp.exp(jnp.sum(lam_p[0:1, :] * lam_p[1:2, :], axis=-1, keepdims=True))
               - jnp.exp(jnp.sum(lam_p[2:3, :] * lam_p[3:4, :], axis=-1, keepdims=True)) + lambda_init)
        a0 = acc_ref[0]
        a1 = acc_ref[1]
        o = a0[0:dv] / a0[dv:dv + 1] - lam * (a1[0:dv] / a1[dv:dv + 1])
        o = o * lax.rsqrt(jnp.mean(o * o, axis=0, keepdims=True) + EPS) * g_ref[...]
        o_ref[pl.ds(pl.multiple_of(qi * tq, tq), tq), :] = (o * (1.0 - lambda_init)).T.astype(o_ref.dtype)

    def half_steps(t, par):
        scores(0, t + 1, 1 - par)
        attend(0, t, par)
        scores(1, t + 1, 1 - par)
        attend(1, t, par)

    unroll = ATTN_TASKS_PER_TRIP
    assert unroll % 2 == 0 and n_kv % unroll == 0 and n_tasks >= 2 * unroll
    scores(0, 0, 0)
    scores(1, 0, 0)

    def body(u, carry):
        t = unroll * u
        for i in range(unroll):
            half_steps(t + i, i % 2)

        @pl.when(lax.rem(t + unroll - 1, n_kv) == n_kv - 1)
        def _():
            finalize(lax.div(t, n_kv))

        return carry

    lax.fori_loop(0, n_tasks // unroll - 1, body, 0)
    for i in range(unroll - 1):
        half_steps(n_tasks - unroll + i, i % 2)
    attend(0, n_tasks - 1, 1)
    attend(1, n_tasks - 1, 1)
    finalize(S // tq - 1)


def _diff_attn(qkv3, lam_p, g_col, lambda_init, tq, tk):
    B, S, _ = qkv3.shape
    H = DA_HEADS
    col = lambda off: pl.BlockSpec((None, S, LANES), lambda b, h: (b, 0, off + h))
    return pl.pallas_call(
        functools.partial(_diff_attn_kernel, lambda_init, tq, tk),
        grid=(B, H),
        in_specs=[col(0), col(H), col(2 * H), _resident(lam_p.shape), _resident(g_col.shape)],
        out_specs=col(0),
        out_shape=jax.ShapeDtypeStruct((B, S, H * DA_V_DIM), BF16),
        scratch_shapes=[pltpu.VMEM((DA_V_DIM + DEN_ROWS, S), BF16),
                        pltpu.VMEM((2, LANES, S), BF16),
                        pltpu.VMEM((2, DA_V_DIM + DEN_ROWS, tq), F32),
                        pltpu.VMEM((2, 1, tq), F32),
                        pltpu.VMEM((2, 2, 1, tq), F32),
                        pltpu.VMEM((2, 2, tk, tq), F32)],
        compiler_params=pltpu.CompilerParams(
            dimension_semantics=("parallel", "parallel"), vmem_limit_bytes=VMEM_LIMIT),
        name="diff_attn",
    )(qkv3, qkv3, qkv3, lam_p, g_col)


def _hgrn_scores(q, kin, v, b, reverse):
    C, n_sub = HG_CHUNK, HG_CHUNK // HG_SUB
    blk = lambda j: slice(j * HG_SUB, (j + 1) * HG_SUB)
    bound = [b[j * HG_SUB:j * HG_SUB + 1, :] if reverse else b[(j + 1) * HG_SUB - 1:(j + 1) * HG_SUB, :]
             for j in range(n_sub)]
    b_far = bound[0] if reverse else bound[-1]
    k_til = jnp.concatenate([kin[blk(j)] * jnp.exp(bound[j] - b[blk(j)]) for j in range(n_sub)], axis=0)
    rows = [slice(0, (j + 1) * HG_SUB) if reverse else slice(j * HG_SUB, C) for j in range(n_sub)]
    q_til = jnp.concatenate([q[rows[j]] * jnp.exp(b[rows[j]] - bound[j]) for j in range(n_sub)], axis=0)
    offs = np.cumsum([0] + [r.stop - r.start for r in rows])
    scores = lax.dot_general(q_til.astype(BF16), k_til.astype(BF16), (((1,), (1,)), ((), ())),
                             preferred_element_type=F32)
    k_dec = (kin * jnp.exp(b_far - b)).astype(BF16)
    increment = lax.dot_general(v, k_dec, (((0,), (0,)), ((), ())), preferred_element_type=F32)
    return scores, [int(o) for o in offs], (q * jnp.exp(b)).astype(BF16), jnp.exp(b_far), increment


def _hgrn_intra(scores, offs, v, reverse):
    C, n_sub = HG_CHUNK, HG_CHUNK // HG_SUB
    col = lax.broadcasted_iota(jnp.int32, (HG_SUB, C), 1)
    row_blocks = []
    for i in range(n_sub):
        js = range(i, n_sub) if reverse else range(0, i + 1)
        a_i = jnp.zeros((HG_SUB, C), F32)
        for j in js:
            r0 = offs[j] + (i * HG_SUB if reverse else (i - j) * HG_SUB)
            a_i = jnp.where(col // HG_SUB == j, scores[r0:r0 + HG_SUB, :], a_i)
        row_blocks.append(a_i)
    attn = jnp.concatenate(row_blocks, axis=0)
    t_i = lax.broadcasted_iota(jnp.int32, (C, C), 0)
    s_i = lax.broadcasted_iota(jnp.int32, (C, C), 1)
    attn = jnp.where((t_i <= s_i) if reverse else (t_i >= s_i), attn, 0.0)
    return jnp.dot(attn.astype(BF16), v, preferred_element_type=F32)


def _hgrn_kernel(q_f_ref, q_b_ref, v_f_ref, v_b_ref, bc_f_ref, bc_b_ref, kin_f_ref, kin_b_ref,
                 y_f_ref, y_b_ref, st_f_ref, st_b_ref):
    @pl.when(pl.program_id(2) == 0)
    def _():
        st_f_ref[...] = jnp.zeros_like(st_f_ref)
        st_b_ref[...] = jnp.zeros_like(st_b_ref)

    n_chunks = q_f_ref.shape[0] // HG_CHUNK
    directions = ((False, q_f_ref, v_f_ref, bc_f_ref, kin_f_ref, y_f_ref, st_f_ref),
                  (True, q_b_ref, v_b_ref, bc_b_ref, kin_b_ref, y_b_ref, st_b_ref))

    def body(u, carry):
        work = []
        for reverse, q_ref, v_ref, bc_ref, kin_ref, y_ref, st_ref in directions:
            for j in range(HG_CHUNKS_PER_TRIP):
                c = u * HG_CHUNKS_PER_TRIP + j
                cc = (n_chunks - 1 - c) if reverse else c
                rs = pl.ds(pl.multiple_of(cc * HG_CHUNK, HG_CHUNK), HG_CHUNK)
                v = v_ref[rs, :]
                stage1 = _hgrn_scores(q_ref[rs, :].astype(F32), kin_ref[rs, :].astype(F32), v,
                                      bc_ref[rs, :], reverse)
                work.append((reverse, rs, v, y_ref, st_ref, stage1))
        intra = [_hgrn_intra(w[5][0], w[5][1], w[2], w[0]) for w in work]
        states = {}
        for (reverse, rs, v, y_ref, st_ref, stage1), o_intra in zip(work, intra):
            _, _, q_state, decay, increment = stage1
            state_t = states.get(reverse)
            if state_t is None:
                state_t = st_ref[...]
            o_inter = lax.dot_general(q_state, state_t.astype(BF16), (((1,), (1,)), ((), ())),
                                      preferred_element_type=F32)
            states[reverse] = decay * state_t + increment
            y_ref[rs, :] = (o_intra + o_inter).astype(y_ref.dtype)
        for reverse, *_, st_ref in directions:
            st_ref[...] = states[reverse]
        return carry

    assert n_chunks % HG_CHUNKS_PER_TRIP == 0
    lax.fori_loop(0, n_chunks // HG_CHUNKS_PER_TRIP, body, 0)


def _hgrn(hq3, hv3, bc3, kin3, seg):
    B, S, W = hq3.shape
    H = W // HG_DIM
    n_seg = S // seg
    fwd = lambda off: pl.BlockSpec((None, seg, HG_DIM), lambda b, h, s: (b, s, off + h))
    bwd = lambda off: pl.BlockSpec((None, seg, HG_DIM), lambda b, h, s: (b, n_seg - 1 - s, off + h))
    return pl.pallas_call(
        _hgrn_kernel,
        grid=(B, H, n_seg),
        in_specs=[fwd(0), bwd(0), fwd(0), bwd(0), fwd(0), bwd(H), fwd(0), bwd(H)],
        out_specs=[fwd(0), bwd(0)],
        out_shape=[jax.ShapeDtypeStruct((B, S, W), BF16), jax.ShapeDtypeStruct((B, S, W), BF16)],
        scratch_shapes=[pltpu.VMEM((HG_DIM, HG_DIM), F32), pltpu.VMEM((HG_DIM, HG_DIM), F32)],
        compiler_params=pltpu.CompilerParams(
            dimension_semantics=("parallel", "parallel", "arbitrary"), vmem_limit_bytes=VMEM_LIMIT),
        name="hgrn2",
    )(hq3, hq3, hv3, hv3, bc3, bc3, kin3, kin3)


def _out_proj_kernel(x_ref, oda_ref, yf_ref, yb_ref, sg_ref, g_ref, w_ref, o_ref):
    half = oda_ref.shape[1]
    y = yf_ref[...].astype(F32) + yb_ref[...].astype(F32)
    sg = sg_ref[...].astype(F32)
    heads = []
    for hh in range(y.shape[1] // HG_DIM):
        cs = slice(hh * HG_DIM, (hh + 1) * HG_DIM)
        heads.append(_rms(y[:, cs], g_ref[...]) * sg[:, cs])
    o_hg = jnp.concatenate(heads, axis=1).astype(BF16)
    o_ref[...] = (x_ref[...]
                  + jnp.dot(oda_ref[...], w_ref[0:half, :], preferred_element_type=F32)
                  + jnp.dot(o_hg, w_ref[half:, :], preferred_element_type=F32))


def _out_proj(x2d, oda, yf, yb, sg, g, w_bf, tm):
    T, D = x2d.shape
    row = lambda w: pl.BlockSpec((tm, w), lambda i: (i, 0))
    return pl.pallas_call(
        _out_proj_kernel,
        grid=(T // tm,),
        in_specs=[row(D), row(oda.shape[1]), row(yf.shape[1]), row(yb.shape[1]), row(sg.shape[1]),
                  _resident(g.shape), _resident(w_bf.shape)],
        out_specs=row(D),
        out_shape=jax.ShapeDtypeStruct((T, D), F32),
        compiler_params=pltpu.CompilerParams(dimension_semantics=("parallel",),
                                             vmem_limit_bytes=VMEM_LIMIT),
        name="out_proj",
    )(x2d, oda, yf, yb, sg, g, w_bf)


HALO_ROWS = 16


def _ffn_ple_kernel(fc, x_ref, halo_ref, p_ref, gf_ref, wg_ref, wu_ref, cw_ref, cb_ref, wd_ref,
                    gp_ref, wpg_ref, wple_ref, gfin_ref, o_ref, h_ref, acc_ref):
    tm = x_ref.shape[0]
    xt = x_ref[...]
    h_ref[0:tm, :] = _rms(xt, gf_ref[...]).astype(BF16)
    h_ref[tm:tm + HALO_ROWS, :] = _rms(halo_ref[...], gf_ref[...]).astype(BF16)
    row = lax.broadcasted_iota(jnp.int32, (tm, fc), 0)
    n_fc = wg_ref.shape[1] // fc
    for c in range(n_fc):
        cs = slice(c * fc, (c + 1) * fc)
        a_ext = jnp.dot(h_ref[...], wg_ref[:, cs], preferred_element_type=F32)
        u = jnp.dot(h_ref[0:tm, :], wu_ref[:, cs], preferred_element_type=F32)
        a = a_ext[0:tm]
        a_prev = jnp.where(row == 0, a_ext[tm:tm + 1], pltpu.roll(a, 1, axis=0))
        a_next = jnp.where(row == tm - 1, a_ext[tm + 1:tm + 2], pltpu.roll(a, tm - 1, axis=0))
        cv = cb_ref[:, cs] + a_prev * cw_ref[0:1, cs] + a * cw_ref[1:2, cs] + a_next * cw_ref[2:3, cs]
        act = (_gelu(cv) * u).astype(BF16)
        d = jnp.dot(act, wd_ref[cs, :], preferred_element_type=F32)
        if c == 0:
            acc_ref[...] = d
        else:
            acc_ref[...] += d
    x2 = xt + acc_ref[...]
    gate = _sigmoid(jnp.dot(_rms(x2, gp_ref[...]).astype(BF16), wpg_ref[...], preferred_element_type=F32))
    ple = jnp.dot(p_ref[...].astype(BF16), wple_ref[...], preferred_element_type=F32)
    o_ref[...] = _rms(x2 + ple * gate, gfin_ref[...])


def _ffn_ple(x1, halo, p2d, gf, wg, wu, cw, cb, wd, gp, wpg, wple, gfin, tm, fc):
    T, D = x1.shape
    row = lambda w: pl.BlockSpec((tm, w), lambda i: (i, 0))
    consts = (gf, wg, wu, cw, cb, wd, gp, wpg, wple, gfin)
    return pl.pallas_call(
        functools.partial(_ffn_ple_kernel, fc),
        grid=(T // tm,),
        in_specs=[row(D), pl.BlockSpec((None, HALO_ROWS, D), lambda i: (i, 0, 0)), row(p2d.shape[1])]
                 + [_resident(c.shape) for c in consts],
        out_specs=row(D),
        out_shape=jax.ShapeDtypeStruct((T, D), F32),
        scratch_shapes=[pltpu.VMEM((tm + HALO_ROWS, D), BF16), pltpu.VMEM((tm, D), F32)],
        compiler_params=pltpu.CompilerParams(dimension_semantics=("parallel",),
                                             vmem_limit_bytes=VMEM_LIMIT),
        name="ffn_ple",
    )(x1, halo, p2d, *consts)


def _conv_halo(x1, tm, seq):
    T, D = x1.shape
    nt = T // tm
    xr = x1.reshape(nt, tm, D)
    zero = jnp.zeros((1, D), x1.dtype)
    prev = jnp.concatenate([zero, xr[:-1, tm - 1, :]], axis=0)
    nxt = jnp.concatenate([xr[1:, 0, :], zero], axis=0)
    start = (jnp.arange(nt) * tm) % seq
    prev = jnp.where((start == 0)[:, None], 0.0, prev)
    nxt = jnp.where((start + tm == seq)[:, None], 0.0, nxt)
    pad = jnp.zeros((nt, HALO_ROWS - 2, D), x1.dtype)
    return jnp.concatenate([prev[:, None, :], nxt[:, None, :], pad], axis=1)


def _rope_constants():
    half = ROT_DIM // 2
    inv_freq = (np.float32(ROPE_THETA) ** (-np.arange(half, dtype=np.float32) / np.float32(half))).astype(np.float32)
    d = np.arange(LANES) % DA_HEAD_DIM
    place = np.zeros((4 * half, 2 * LANES), np.float32)
    for f in range(half):
        hit = (d < ROT_DIM) & (d % half == f)
        sign = np.where(d < half, -1.0, 1.0)
        place[f, :LANES] = place[half + f, :LANES] = hit
        place[2 * half + f, LANES:] = place[3 * half + f, LANES:] = hit * sign
    return jnp.asarray(inv_freq.reshape(half, 1)), jnp.asarray(place, dtype=BF16)


def kernel(x, p, positions, norm_mix_g, w_in, lam_q1, lam_k1, lam_q2, lam_k2, da_subln_g, hg_lb_gamma, hg_norm_g, w_out, norm_ffn_g, w_ffn_gate, w_ffn_up, ffn_conv_w, ffn_conv_b, w_ffn_down, norm_ple_g, w_ple, w_ple_gate, final_norm_g):
    B, S, D = x.shape
    T = B * S
    depth = w_in.shape[0]
    tm = min(512, S)
    tq = min(512, S)
    tk = min(512, S)
    seg = min(2048, S)
    fc = 256
    row2 = lambda v: v.reshape(1, -1)
    xc = x.reshape(T, D)
    pos = positions.reshape(T // tm, 1, tm)
    rope_freq, rope_place = _rope_constants()
    for i in range(depth):
        lambda_init = 0.8 - 0.6 * math.exp(-0.3 * i)
        gam = hg_lb_gamma.reshape(-1, hg_lb_gamma.shape[-1])
        qkv, hq, bc, kin, hv, sg = _in_proj(xc, pos, rope_freq, rope_place, row2(norm_mix_g[i]), w_in[i].astype(BF16),
                                            gam, i, tm)
        lam_p = jnp.stack([lam_q1[i], lam_k1[i], lam_q2[i], lam_k2[i]], axis=0)
        o_da = _diff_attn(qkv.reshape(B, S, -1), lam_p, da_subln_g[i].reshape(-1, 1), lambda_init, tq, tk)
        y_f, y_b = _hgrn(hq.reshape(B, S, -1), hv.reshape(B, S, -1), bc.reshape(B, S, -1),
                         kin.reshape(B, S, -1), seg)
        x1 = _out_proj(xc, o_da.reshape(T, -1), y_f.reshape(T, -1), y_b.reshape(T, -1), sg,
                       row2(hg_norm_g[i]), w_out[i].astype(BF16), tm)
        last = i == depth - 1
        xc = _ffn_ple(x1, _conv_halo(x1, tm, S), p[i].reshape(T, -1), row2(norm_ffn_g[i]),
                      w_ffn_gate[i].astype(BF16), w_ffn_up[i].astype(BF16), ffn_conv_w[i],
                      row2(ffn_conv_b[i]), w_ffn_down[i].astype(BF16), row2(norm_ple_g[i]),
                      w_ple_gate[i].astype(BF16), w_ple[i].astype(BF16), row2(final_norm_g), tm, fc)
        assert last, "multi-layer stacks need the final norm split out of ffn_ple"
    return xc.reshape(B, S, D)
```

```python
import functools
import math

import numpy as np
import jax
import jax.numpy as jnp
from jax import lax
from jax.experimental import pallas as pl
from jax.experimental.pallas import tpu as pltpu

F32 = jnp.float32
BF16 = jnp.bfloat16

EPS = 1e-6
ROPE_THETA = 500000.0
DA_HEADS = 4
DA_HEAD_DIM = 64
DA_V_DIM = 128
ROT_DIM = 16
HG_HEADS = 4
HG_DIM = 128
CONV_WIDTH = 3
HG_CHUNK = 64
HG_SUB = 16
HG_CHUNKS_PER_TRIP = 4
LANES = 128
SUBLANES = 8
VMEM_LIMIT = 56 * 1024 * 1024


def _rms(xf, g):
    return xf * lax.rsqrt(jnp.mean(xf * xf, axis=-1, keepdims=True) + EPS) * g


def _sigmoid(z):
    return 1.0 / (1.0 + jnp.exp(-z))


def _gelu(x):
    return 0.5 * x * (1.0 + lax.erf(x * (2.0 ** -0.5)))


def _resident(shape):
    return pl.BlockSpec(shape, lambda *_: (0,) * len(shape), pipeline_mode=pl.Buffered(1))


def _chunk_cumsum(x, reverse):
    n = x.shape[0]
    r = lax.broadcasted_iota(jnp.int32, x.shape, 0) & (HG_CHUNK - 1)
    s = 1
    while s < SUBLANES:
        if reverse:
            x = x + jnp.where(r < HG_CHUNK - s, pltpu.roll(x, n - s, axis=0), 0.0)
        else:
            x = x + jnp.where(r >= s, pltpu.roll(x, s, axis=0), 0.0)
        s *= 2
    while s < HG_CHUNK:
        parts = []
        for c0 in range(0, n, HG_CHUNK):
            lo, hi = x[c0:c0 + HG_CHUNK - s], x[c0 + s:c0 + HG_CHUNK]
            parts += [lo + hi, x[c0 + HG_CHUNK - s:c0 + HG_CHUNK]] if reverse else [x[c0:c0 + s], hi + lo]
        x = jnp.concatenate(parts, axis=0)
        s *= 2
    return x


def _in_proj_kernel(layer, n_slots, x_ref, pos_ref, freq_ref, place_ref, g_ref, w_ref, gam_ref,
                    qkv_ref, hq_ref, bc_ref, kin_ref, hv_ref, sg_ref):
    h = _rms(x_ref[...], g_ref[...]).astype(BF16)
    sec = qkv_ref.shape[1] // 3

    proj_all = jnp.dot(h, w_ref[...], preferred_element_type=F32)

    half = ROT_DIM // 2
    ang = freq_ref[...] * pos_ref[...].astype(F32)
    parts = []
    for trig in (jnp.cos(ang), jnp.sin(ang)):
        hi = trig.astype(BF16).astype(F32)
        parts += [hi, trig - hi]
    tab = lax.dot_general(jnp.concatenate(parts, axis=0).astype(BF16), place_ref[...],
                          (((0,), (0,)), ((), ())), preferred_element_type=F32)
    lane_d = lax.broadcasted_iota(jnp.int32, (1, LANES), 1) & (DA_HEAD_DIM - 1)
    cos = tab[:, :LANES] + jnp.where(lane_d >= ROT_DIM, 1.0, 0.0)
    sin = tab[:, LANES:]
    first_half = lane_d < half

    def rotary(t, scale):
        outs = []
        for hh in range(sec // LANES):
            blk = t[:, hh * LANES:(hh + 1) * LANES]
            partner = jnp.where(first_half, pltpu.roll(blk, LANES - half, axis=1),
                                pltpu.roll(blk, half, axis=1))
            rot = blk * cos + partner * sin
            outs.append(rot * scale if scale != 1.0 else rot)
        return jnp.concatenate(outs, axis=1)

    def put_q(t):
        qkv_ref[:, 0:sec] = rotary(t, DA_HEAD_DIM ** -0.5 * math.log2(math.e)).astype(BF16)

    def put_k(t):
        qkv_ref[:, sec:2 * sec] = rotary(t, 1.0).astype(BF16)

    def put_v(t):
        qkv_ref[:, 2 * sec:3 * sec] = t.astype(BF16)

    def put_hq(t):
        hq_ref[...] = t.astype(BF16)

    def put_forget(d, z):
        gam = gam_ref[d * n_slots:(d + 1) * n_slots, :]
        e = jnp.exp(gam - jnp.max(gam, axis=0, keepdims=True))
        lb = jnp.sum(e[0:layer + 1, :], axis=0, keepdims=True) / jnp.sum(e, axis=0, keepdims=True)
        for r0 in range(0, z.shape[0], HG_CHUNK):
            sig = _sigmoid(z[r0:r0 + HG_CHUNK])
            logf = jnp.log(lb + (1.0 - lb) * sig)
            bc_ref[r0:r0 + HG_CHUNK, d * sec:(d + 1) * sec] = _chunk_cumsum(logf, reverse=(d == 1))
            kin_ref[r0:r0 + HG_CHUNK, d * sec:(d + 1) * sec] = ((1.0 - lb) * (1.0 - sig)).astype(BF16)

    def put_hv(t):
        hv_ref[...] = t.astype(BF16)

    def put_gate(t):
        sg_ref[...] = (t * _sigmoid(t)).astype(BF16)

    epilogues = (put_q, put_k, put_v, put_hq, functools.partial(put_forget, 0),
                 functools.partial(put_forget, 1), put_hv, put_gate)
    for i, epilogue in enumerate(epilogues):
        epilogue(proj_all[:, i * sec:(i + 1) * sec])


def _in_proj(x2d, pos, freq, place, g, w_bf, gam, layer, tm):
    T, D = x2d.shape
    sec = w_bf.shape[1] // 8
    n_slots = gam.shape[0] // 2
    row = lambda w: pl.BlockSpec((tm, w), lambda i: (i, 0))
    return pl.pallas_call(
        functools.partial(_in_proj_kernel, layer, n_slots),
        grid=(T // tm,),
        in_specs=[row(D), pl.BlockSpec((None, 1, tm), lambda i: (i, 0, 0)),
                  _resident(freq.shape), _resident(place.shape), _resident(g.shape),
                  _resident(w_bf.shape), _resident(gam.shape)],
        out_specs=[row(3 * sec), row(sec), row(2 * sec), row(2 * sec), row(sec), row(sec)],
        out_shape=[jax.ShapeDtypeStruct((T, 3 * sec), BF16), jax.ShapeDtypeStruct((T, sec), BF16),
                   jax.ShapeDtypeStruct((T, 2 * sec), F32), jax.ShapeDtypeStruct((T, 2 * sec), BF16),
                   jax.ShapeDtypeStruct((T, sec), BF16), jax.ShapeDtypeStruct((T, sec), BF16)],
        compiler_params=pltpu.CompilerParams(dimension_semantics=("parallel",),
                                             vmem_limit_bytes=VMEM_LIMIT),
        name="in_proj",
    )(x2d, pos, freq, place, g, w_bf, gam)


DEN_ROWS = 16
ATTN_TASKS_PER_TRIP = 4


def _diff_attn_kernel(lambda_init, tq, tk, q_ref, k_ref, v_ref, lam_ref, g_ref, o_ref,
                      vt_ref, qt_ref, acc_ref, m_ref, smax_ref, s_ref):
    S = k_ref.shape[0]
    n_kv = S // tk
    n_tasks = (S // tq) * n_kv
    dv = v_ref.shape[1]

    feat = lax.broadcasted_iota(jnp.int32, (q_ref.shape[1], tk), 0)

    def transpose_qv(i, carry):
        r = pl.ds(pl.multiple_of(i * tk, tk), tk)
        vt_ref[0:dv, r] = v_ref[r, :].astype(F32).T.astype(BF16)
        q_t = q_ref[r, :].astype(F32).T
        qt_ref[0, :, r] = jnp.where(feat < DA_HEAD_DIM, q_t, 0.0).astype(BF16)
        qt_ref[1, :, r] = jnp.where(feat >= DA_HEAD_DIM, q_t, 0.0).astype(BF16)
        return carry

    lax.fori_loop(0, n_kv, transpose_qv, 0)
    vt_ref[dv:dv + DEN_ROWS, :] = jnp.ones((DEN_ROWS, S), BF16)
    acc_ref[...] = jnp.zeros_like(acc_ref)

    def kv_rows(t):
        return pl.ds(pl.multiple_of(lax.rem(t, n_kv) * tk, tk), tk)

    def scores(mp, t, par):
        q_cols = pl.ds(pl.multiple_of(lax.div(t, n_kv) * tq, tq), tq)
        s = jnp.dot(k_ref[kv_rows(t), :], qt_ref[mp, :, q_cols], preferred_element_type=F32)
        s_ref[mp, par] = s
        smax_ref[mp, par] = jnp.max(s, axis=0, keepdims=True)

    def attend(mp, t, par):
        m_old = jnp.where(lax.rem(t, n_kv) == 0, -jnp.inf, m_ref[mp])
        m_new = jnp.maximum(m_old, smax_ref[mp, par])
        m_ref[mp] = m_new
        p = jnp.exp2((s_ref[mp, par] - m_new).astype(BF16))
        acc_ref[mp] = jnp.exp2(m_old - m_new) * acc_ref[mp] + jnp.dot(
            vt_ref[:, kv_rows(t)], p, preferred_element_type=F32)

    def finalize(qi):
        lam_p = lam_ref[...]
        lam = (jnp.exp(jnp.sum(lam_p[0:1, :] * lam_p[1:2, :], axis=-1, keepdims=True))
               - jnp.exp(jnp.sum(lam_p[2:3, :] * lam_p[3:4, :], axis=-1, keepdims=True)) + lambda_init)
        a0 = acc_ref[0]
        a1 = acc_ref[1]
        o = a0[0:dv] / a0[dv:dv + 1] - lam * (a1[0:dv] / a1[dv:dv + 1])
        o = o * lax.rsqrt(jnp.mean(o * o, axis=0, keepdims=True) + EPS) * g_ref[...]
        o_ref[pl.ds(pl.multiple_of(qi * tq, tq), tq), :] = (o * (1.0 - lambda_init)).T.astype(o_ref.dtype)

    def half_steps(t, par):
        scores(0, t + 1, 1 - par)
        attend(0, t, par)
        scores(1, t + 1, 1 - par)
        attend(1, t, par)

    unroll = ATTN_TASKS_PER_TRIP
    assert unroll % 2 == 0 and n_kv % unroll == 0 and n_tasks >= 2 * unroll
    scores(0, 0, 0)
    scores(1, 0, 0)

    def body(u, carry):
        t = unroll * u
        for i in range(unroll):
            half_steps(t + i, i % 2)

        @pl.when(lax.rem(t + unroll - 1, n_kv) == n_kv - 1)
        def _():
            finalize(lax.div(t, n_kv))

        return carry

    lax.fori_loop(0, n_tasks // unroll - 1, body, 0)
    for i in range(unroll - 1):
        half_steps(n_tasks - unroll + i, i % 2)
    attend(0, n_tasks - 1, 1)
    attend(1, n_tasks - 1, 1)
    finalize(S // tq - 1)


def _diff_attn(qkv3, lam_p, g_col, lambda_init, tq, tk):
    B, S, _ = qkv3.shape
    H = DA_HEADS
    col = lambda off: pl.BlockSpec((None, S, LANES), lambda b, h: (b, 0, off + h))
    return pl.pallas_call(
        functools.partial(_diff_attn_kernel, lambda_init, tq, tk),
        grid=(B, H),
        in_specs=[col(0), col(H), col(2 * H), _resident(lam_p.shape), _resident(g_col.shape)],
        out_specs=col(0),
        out_shape=jax.ShapeDtypeStruct((B, S, H * DA_V_DIM), BF16),
        scratch_shapes=[pltpu.VMEM((DA_V_DIM + DEN_ROWS, S), BF16),
                        pltpu.VMEM((2, LANES, S), BF16),
                        pltpu.VMEM((2, DA_V_DIM + DEN_ROWS, tq), F32),
                        pltpu.VMEM((2, 1, tq), F32),
                        pltpu.VMEM((2, 2, 1, tq), F32),
                        pltpu.VMEM((2, 2, tk, tq), F32)],
        compiler_params=pltpu.CompilerParams(
            dimension_semantics=("parallel", "parallel"), vmem_limit_bytes=VMEM_LIMIT),
        name="diff_attn",
    )(qkv3, qkv3, qkv3, lam_p, g_col)


def _hgrn_scores(q, kin, v, b, reverse):
    C, n_sub = HG_CHUNK, HG_CHUNK // HG_SUB
    blk = lambda j: slice(j * HG_SUB, (j + 1) * HG_SUB)
    bound = [b[j * HG_SUB:j * HG_SUB + 1, :] if reverse else b[(j + 1) * HG_SUB - 1:(j + 1) * HG_SUB, :]
             for j in range(n_sub)]
    b_far = bound[0] if reverse else bound[-1]
    k_til = jnp.concatenate([kin[blk(j)] * jnp.exp(bound[j] - b[blk(j)]) for j in range(n_sub)], axis=0)
    rows = [slice(0, (j + 1) * HG_SUB) if reverse else slice(j * HG_SUB, C) for j in range(n_sub)]
    q_til = jnp.concatenate([q[rows[j]] * jnp.exp(b[rows[j]] - bound[j]) for j in range(n_sub)], axis=0)
    offs = np.cumsum([0] + [r.stop - r.start for r in rows])
    scores = lax.dot_general(q_til.astype(BF16), k_til.astype(BF16), (((1,), (1,)), ((), ())),
                             preferred_element_type=F32)
    k_dec = (kin * jnp.exp(b_far - b)).astype(BF16)
    increment = lax.dot_general(v, k_dec, (((0,), (0,)), ((), ())), preferred_element_type=F32)
    return scores, [int(o) for o in offs], (q * jnp.exp(b)).astype(BF16), jnp.exp(b_far), increment


def _hgrn_intra(scores, offs, v, reverse):
    C, n_sub = HG_CHUNK, HG_CHUNK // HG_SUB
    col = lax.broadcasted_iota(jnp.int32, (HG_SUB, C), 1)
    row_blocks = []
    for i in range(n_sub):
        js = range(i, n_sub) if reverse else range(0, i + 1)
        a_i = jnp.zeros((HG_SUB, C), F32)
        for j in js:
            r0 = offs[j] + (i * HG_SUB if reverse else (i - j) * HG_SUB)
            a_i = jnp.where(col // HG_SUB == j, scores[r0:r0 + HG_SUB, :], a_i)
        row_blocks.append(a_i)
    attn = jnp.concatenate(row_blocks, axis=0)
    t_i = lax.broadcasted_iota(jnp.int32, (C, C), 0)
    s_i = lax.broadcasted_iota(jnp.int32, (C, C), 1)
    attn = jnp.where((t_i <= s_i) if reverse else (t_i >= s_i), attn, 0.0)
    return jnp.dot(attn.astype(BF16), v, preferred_element_type=F32)


def _hgrn_kernel(q_f_ref, q_b_ref, v_f_ref, v_b_ref, bc_f_ref, bc_b_ref, kin_f_ref, kin_b_ref,
                 y_f_ref, y_b_ref, st_f_ref, st_b_ref):
    @pl.when(pl.program_id(2) == 0)
    def _():
        st_f_ref[...] = jnp.zeros_like(st_f_ref)
        st_b_ref[...] = jnp.zeros_like(st_b_ref)

    n_chunks = q_f_ref.shape[0] // HG_CHUNK
    directions = ((False, q_f_ref, v_f_ref, bc_f_ref, kin_f_ref, y_f_ref, st_f_ref),
                  (True, q_b_ref, v_b_ref, bc_b_ref, kin_b_ref, y_b_ref, st_b_ref))

    def body(u, carry):
        work = []
        for reverse, q_ref, v_ref, bc_ref, kin_ref, y_ref, st_ref in directions:
            for j in range(HG_CHUNKS_PER_TRIP):
                c = u * HG_CHUNKS_PER_TRIP + j
                cc = (n_chunks - 1 - c) if reverse else c
                rs = pl.ds(pl.multiple_of(cc * HG_CHUNK, HG_CHUNK), HG_CHUNK)
                v = v_ref[rs, :]
                stage1 = _hgrn_scores(q_ref[rs, :].astype(F32), kin_ref[rs, :].astype(F32), v,
                                      bc_ref[rs, :], reverse)
                work.append((reverse, rs, v, y_ref, st_ref, stage1))
        intra = [_hgrn_intra(w[5][0], w[5][1], w[2], w[0]) for w in work]
        states = {}
        for (reverse, rs, v, y_ref, st_ref, stage1), o_intra in zip(work, intra):
            _, _, q_state, decay, increment = stage1
            state_t = states.get(reverse)
            if state_t is None:
                state_t = st_ref[...]
            o_inter = lax.dot_general(q_state, state_t.astype(BF16), (((1,), (1,)), ((), ())),
                                      preferred_element_type=F32)
            states[reverse] = decay * state_t + increment
            y_ref[rs, :] = (o_intra + o_inter).astype(y_ref.dtype)
        for reverse, *_, st_ref in directions:
            st_ref[...] = states[reverse]
        return carry

    assert n_chunks % HG_CHUNKS_PER_TRIP == 0
    lax.fori_loop(0, n_chunks // HG_CHUNKS_PER_TRIP, body, 0)


def _hgrn(hq3, hv3, bc3, kin3, seg):
    B, S, W = hq3.shape
    H = W // HG_DIM
    n_seg = S // seg
    fwd = lambda off: pl.BlockSpec((None, seg, HG_DIM), lambda b, h, s: (b, s, off + h))
    bwd = lambda off: pl.BlockSpec((None, seg, HG_DIM), lambda b, h, s: (b, n_seg - 1 - s, off + h))
    return pl.pallas_call(
        _hgrn_kernel,
        grid=(B, H, n_seg),
        in_specs=[fwd(0), bwd(0), fwd(0), bwd(0), fwd(0), bwd(H), fwd(0), bwd(H)],
        out_specs=[fwd(0), bwd(0)],
        out_shape=[jax.ShapeDtypeStruct((B, S, W), BF16), jax.ShapeDtypeStruct((B, S, W), BF16)],
        scratch_shapes=[pltpu.VMEM((HG_DIM, HG_DIM), F32), pltpu.VMEM((HG_DIM, HG_DIM), F32)],
        compiler_params=pltpu.CompilerParams(
            dimension_semantics=("parallel", "parallel", "arbitrary"), vmem_limit_bytes=VMEM_LIMIT),
        name="hgrn2",
    )(hq3, hq3, hv3, hv3, bc3, bc3, kin3, kin3)


def _mixer_residual(x, oda, yf, yb, sg, g_hg, w_ref):
    half = oda.shape[1]
    y = yf.astype(F32) + yb.astype(F32)
    sg = sg.astype(F32)
    heads = []
    for hh in range(y.shape[1] // HG_DIM):
        cs = slice(hh * HG_DIM, (hh + 1) * HG_DIM)
        heads.append(_rms(y[:, cs], g_hg) * sg[:, cs])
    o_hg = jnp.concatenate(heads, axis=1).astype(BF16)
    return (x + jnp.dot(oda, w_ref[0:half, :], preferred_element_type=F32)
            + jnp.dot(o_hg, w_ref[half:, :], preferred_element_type=F32))


HALO_ROWS = 16


def _ffn_ple_kernel(fc, x_ref, oda_ref, yf_ref, yb_ref, sg_ref, xh_ref, odah_ref, yfh_ref, ybh_ref,
                    sgh_ref, p_ref, ghg_ref, wo_ref, gf_ref, wg_ref, wu_ref, cw_ref, cb_ref, wd_ref,
                    gp_ref, wpg_ref, wple_ref, gfin_ref, o_ref, h_ref, acc_ref):
    tm = x_ref.shape[0]
    ext = lambda tile_ref, halo_ref: jnp.concatenate([tile_ref[...], halo_ref[...]], axis=0)
    x1 = _mixer_residual(ext(x_ref, xh_ref), ext(oda_ref, odah_ref), ext(yf_ref, yfh_ref),
                         ext(yb_ref, ybh_ref), ext(sg_ref, sgh_ref), ghg_ref[...], wo_ref)
    xt = x1[0:tm]
    h_ref[...] = _rms(x1, gf_ref[...]).astype(BF16)
    assert sum(fc) == wg_ref.shape[1]
    edges = np.cumsum((0,) + tuple(fc))
    chunks = [slice(int(lo), int(hi)) for lo, hi in zip(edges[:-1], edges[1:])]

    def up_gate(cs):
        return (jnp.dot(h_ref[...], wg_ref[:, cs], preferred_element_type=F32),
                jnp.dot(h_ref[0:tm, :], wu_ref[:, cs], preferred_element_type=F32))

    nxt = up_gate(chunks[0])
    for c, cs in enumerate(chunks):
        a_ext, u = nxt
        if c + 1 < len(chunks):
            nxt = up_gate(chunks[c + 1])
        row = lax.broadcasted_iota(jnp.int32, u.shape, 0)
        a = a_ext[0:tm]
        a_prev = jnp.where(row == 0, a_ext[tm:tm + 1], pltpu.roll(a, 1, axis=0))
        a_next = jnp.where(row == tm - 1, a_ext[tm + 1:tm + 2], pltpu.roll(a, tm - 1, axis=0))
        cv = cb_ref[:, cs] + a_prev * cw_ref[0:1, cs] + a * cw_ref[1:2, cs] + a_next * cw_ref[2:3, cs]
        act = (_gelu(cv) * u).astype(BF16)
        d = jnp.dot(act, wd_ref[cs, :], preferred_element_type=F32)
        if c == 0:
            acc_ref[...] = d
        else:
            acc_ref[...] += d
    x2 = xt + acc_ref[...]
    gate = _sigmoid(jnp.dot(_rms(x2, gp_ref[...]).astype(BF16), wpg_ref[...], preferred_element_type=F32))
    ple = jnp.dot(p_ref[...].astype(BF16), wple_ref[...], preferred_element_type=F32)
    o_ref[...] = _rms(x2 + ple * gate, gfin_ref[...])


def _ffn_ple(streams, p2d, consts, tm, seq, fc):
    T, D = streams[0].shape
    row = lambda w: pl.BlockSpec((tm, w), lambda i: (i, 0))
    halo = lambda w: pl.BlockSpec((None, HALO_ROWS, w), lambda i: (i, 0, 0))
    halos = [_conv_halo(a, tm, seq) for a in streams]
    return pl.pallas_call(
        functools.partial(_ffn_ple_kernel, fc),
        grid=(T // tm,),
        in_specs=[row(a.shape[1]) for a in streams] + [halo(a.shape[1]) for a in streams]
                 + [row(p2d.shape[1])] + [_resident(c.shape) for c in consts],
        out_specs=row(D),
        out_shape=jax.ShapeDtypeStruct((T, D), F32),
        scratch_shapes=[pltpu.VMEM((tm + HALO_ROWS, D), BF16), pltpu.VMEM((tm, D), F32)],
        compiler_params=pltpu.CompilerParams(dimension_semantics=("parallel",),
                                             vmem_limit_bytes=VMEM_LIMIT),
        name="ffn_ple",
    )(*streams, *halos, p2d, *consts)


def _conv_halo(x1, tm, seq):
    T, D = x1.shape
    nt = T // tm
    xr = x1.reshape(nt, tm, D)
    zero = jnp.zeros((1, D), x1.dtype)
    prev = jnp.concatenate([zero, xr[:-1, tm - 1, :]], axis=0)
    nxt = jnp.concatenate([xr[1:, 0, :], zero], axis=0)
    start = (jnp.arange(nt) * tm) % seq
    prev = jnp.where((start == 0)[:, None], 0.0, prev)
    nxt = jnp.where((start + tm == seq)[:, None], 0.0, nxt)
    pad = jnp.zeros((nt, HALO_ROWS - 2, D), x1.dtype)
    return jnp.concatenate([prev[:, None, :], nxt[:, None, :], pad], axis=1)


def _rope_constants():
    half = ROT_DIM // 2
    inv_freq = (np.float32(ROPE_THETA) ** (-np.arange(half, dtype=np.float32) / np.float32(half))).astype(np.float32)
    d = np.arange(LANES) % DA_HEAD_DIM
    place = np.zeros((4 * half, 2 * LANES), np.float32)
    for f in range(half):
        hit = (d < ROT_DIM) & (d % half == f)
        sign = np.where(d < half, -1.0, 1.0)
        place[f, :LANES] = place[half + f, :LANES] = hit
        place[2 * half + f, LANES:] = place[3 * half + f, LANES:] = hit * sign
    return jnp.asarray(inv_freq.reshape(half, 1)), jnp.asarray(place, dtype=BF16)


def kernel(x, p, positions, norm_mix_g, w_in, lam_q1, lam_k1, lam_q2, lam_k2, da_subln_g, hg_lb_gamma, hg_norm_g, w_out, norm_ffn_g, w_ffn_gate, w_ffn_up, ffn_conv_w, ffn_conv_b, w_ffn_down, norm_ple_g, w_ple, w_ple_gate, final_norm_g):
    B, S, D = x.shape
    T = B * S
    depth = w_in.shape[0]
    tm = min(512, S)
    tq = min(512, S)
    tk = min(512, S)
    seg = min(2048, S)
    fc = (768, 768, 768, 512)
    row2 = lambda v: v.reshape(1, -1)
    xc = x.reshape(T, D)
    pos = positions.reshape(T // tm, 1, tm)
    rope_freq, rope_place = _rope_constants()
    for i in range(depth):
        lambda_init = 0.8 - 0.6 * math.exp(-0.3 * i)
        gam = hg_lb_gamma.reshape(-1, hg_lb_gamma.shape[-1])
        qkv, hq, bc, kin, hv, sg = _in_proj(xc, pos, rope_freq, rope_place, row2(norm_mix_g[i]), w_in[i].astype(BF16),
                                            gam, i, tm)
        lam_p = jnp.stack([lam_q1[i], lam_k1[i], lam_q2[i], lam_k2[i]], axis=0)
        o_da = _diff_attn(qkv.reshape(B, S, -1), lam_p, da_subln_g[i].reshape(-1, 1), lambda_init, tq, tk)
        y_f, y_b = _hgrn(hq.reshape(B, S, -1), hv.reshape(B, S, -1), bc.reshape(B, S, -1),
                         kin.reshape(B, S, -1), seg)
        streams = (xc, o_da.reshape(T, -1), y_f.reshape(T, -1), y_b.reshape(T, -1), sg)
        consts = (row2(hg_norm_g[i]), w_out[i].astype(BF16), row2(norm_ffn_g[i]),
                  w_ffn_gate[i].astype(BF16), w_ffn_up[i].astype(BF16), ffn_conv_w[i],
                  row2(ffn_conv_b[i]), w_ffn_down[i].astype(BF16), row2(norm_ple_g[i]),
                  w_ple_gate[i].astype(BF16), w_ple[i].astype(BF16), row2(final_norm_g))
        xc = _ffn_ple(streams, p[i].reshape(T, -1), consts, tm, S, fc)
        assert i == depth - 1, "multi-layer stacks need the final norm split out of ffn_ple"
    return xc.reshape(B, S, D)
```

```python
import functools
import math

import numpy as np
import jax
import jax.numpy as jnp
from jax import lax
from jax.experimental import pallas as pl
from jax.experimental.pallas import tpu as pltpu

F32 = jnp.float32
BF16 = jnp.bfloat16

EPS = 1e-6
ROPE_THETA = 500000.0
DA_HEADS = 4
DA_HEAD_DIM = 64
DA_V_DIM = 128
ROT_DIM = 16
HG_HEADS = 4
HG_DIM = 128
CONV_WIDTH = 3
HG_CHUNK = 64
HG_SUB = 16
HG_CHUNKS_PER_TRIP = 4
LANES = 128
SUBLANES = 8
VMEM_LIMIT = 56 * 1024 * 1024


def _rms(xf, g):
    return xf * lax.rsqrt(jnp.mean(xf * xf, axis=-1, keepdims=True) + EPS) * g


def _sigmoid(z):
    return 0.5 + 0.5 * jnp.tanh(0.5 * z)


def _gelu(x):
    return 0.5 * x * (1.0 + lax.erf(x * (2.0 ** -0.5)))


def _resident(shape):
    return pl.BlockSpec(shape, lambda *_: (0,) * len(shape), pipeline_mode=pl.Buffered(1))


def _chunk_cumsum(x, reverse):
    n = x.shape[0]
    r = lax.broadcasted_iota(jnp.int32, x.shape, 0) & (HG_CHUNK - 1)
    s = 1
    while s < SUBLANES:
        if reverse:
            x = x + jnp.where(r < HG_CHUNK - s, pltpu.roll(x, n - s, axis=0), 0.0)
        else:
            x = x + jnp.where(r >= s, pltpu.roll(x, s, axis=0), 0.0)
        s *= 2
    while s < HG_CHUNK:
        parts = []
        for c0 in range(0, n, HG_CHUNK):
            lo, hi = x[c0:c0 + HG_CHUNK - s], x[c0 + s:c0 + HG_CHUNK]
            parts += [lo + hi, x[c0 + HG_CHUNK - s:c0 + HG_CHUNK]] if reverse else [x[c0:c0 + s], hi + lo]
        x = jnp.concatenate(parts, axis=0)
        s *= 2
    return x


def _in_proj_kernel(layer, n_slots, x_ref, pos_ref, freq_ref, place_ref, g_ref, w_ref, gam_ref,
                    qkv_ref, hq_ref, bc_ref, kin_ref, hv_ref, sg_ref):
    h = _rms(x_ref[...], g_ref[...]).astype(BF16)
    sec = qkv_ref.shape[1] // 3

    proj_all = jnp.dot(h, w_ref[...], preferred_element_type=F32)

    half = ROT_DIM // 2
    ang = freq_ref[...] * pos_ref[...].astype(F32)
    parts = []
    for trig in (jnp.cos(ang), jnp.sin(ang)):
        hi = trig.astype(BF16).astype(F32)
        parts += [hi, trig - hi]
    tab = lax.dot_general(jnp.concatenate(parts, axis=0).astype(BF16), place_ref[...],
                          (((0,), (0,)), ((), ())), preferred_element_type=F32)
    lane_d = lax.broadcasted_iota(jnp.int32, (1, LANES), 1) & (DA_HEAD_DIM - 1)
    cos = tab[:, :LANES] + jnp.where(lane_d >= ROT_DIM, 1.0, 0.0)
    sin = tab[:, LANES:]
    first_half = lane_d < half

    def rotary(t, scale):
        outs = []
        for hh in range(sec // LANES):
            blk = t[:, hh * LANES:(hh + 1) * LANES]
            partner = jnp.where(first_half, pltpu.roll(blk, LANES - half, axis=1),
                                pltpu.roll(blk, half, axis=1))
            rot = blk * cos + partner * sin
            outs.append(rot * scale if scale != 1.0 else rot)
        return jnp.concatenate(outs, axis=1)

    def put_q(t):
        qkv_ref[:, 0:sec] = rotary(t, DA_HEAD_DIM ** -0.5 * math.log2(math.e)).astype(BF16)

    def put_k(t):
        qkv_ref[:, sec:2 * sec] = rotary(t, 1.0).astype(BF16)

    def put_v(t):
        qkv_ref[:, 2 * sec:3 * sec] = t.astype(BF16)

    def put_hq(t):
        hq_ref[...] = t.astype(BF16)

    def put_forget(d, z):
        gam = gam_ref[d * n_slots:(d + 1) * n_slots, :]
        e = jnp.exp(gam - jnp.max(gam, axis=0, keepdims=True))
        lb = jnp.sum(e[0:layer + 1, :], axis=0, keepdims=True) / jnp.sum(e, axis=0, keepdims=True)
        half_span = 0.5 * (1.0 - lb)
        mid = 0.5 * (1.0 + lb)
        for r0 in range(0, z.shape[0], HG_CHUNK):
            w = half_span * jnp.tanh(0.5 * z[r0:r0 + HG_CHUNK])
            bc_ref[r0:r0 + HG_CHUNK, d * sec:(d + 1) * sec] = _chunk_cumsum(jnp.log2(mid + w),
                                                                             reverse=(d == 1))
            kin_ref[r0:r0 + HG_CHUNK, d * sec:(d + 1) * sec] = (half_span - w).astype(BF16)

    def put_hv(t):
        hv_ref[...] = t.astype(BF16)

    def put_gate(t):
        sg_ref[...] = (t * _sigmoid(t)).astype(BF16)

    epilogues = (put_q, put_k, put_v, put_hq, functools.partial(put_forget, 0),
                 functools.partial(put_forget, 1), put_hv, put_gate)
    for i, epilogue in enumerate(epilogues):
        epilogue(proj_all[:, i * sec:(i + 1) * sec])


def _in_proj(x2d, pos, freq, place, g, w_bf, gam, layer, tm):
    T, D = x2d.shape
    sec = w_bf.shape[1] // 8
    n_slots = gam.shape[0] // 2
    row = lambda w: pl.BlockSpec((tm, w), lambda i: (i, 0))
    return pl.pallas_call(
        functools.partial(_in_proj_kernel, layer, n_slots),
        grid=(T // tm,),
        in_specs=[row(D), pl.BlockSpec((None, 1, tm), lambda i: (i, 0, 0)),
                  _resident(freq.shape), _resident(place.shape), _resident(g.shape),
                  _resident(w_bf.shape), _resident(gam.shape)],
        out_specs=[row(3 * sec), row(sec), row(2 * sec), row(2 * sec), row(sec), row(sec)],
        out_shape=[jax.ShapeDtypeStruct((T, 3 * sec), BF16), jax.ShapeDtypeStruct((T, sec), BF16),
                   jax.ShapeDtypeStruct((T, 2 * sec), F32), jax.ShapeDtypeStruct((T, 2 * sec), BF16),
                   jax.ShapeDtypeStruct((T, sec), BF16), jax.ShapeDtypeStruct((T, sec), BF16)],
        compiler_params=pltpu.CompilerParams(dimension_semantics=("parallel",),
                                             vmem_limit_bytes=VMEM_LIMIT),
        name="in_proj",
    )(x2d, pos, freq, place, g, w_bf, gam)


DEN_ROWS = 16
ATTN_TASKS_PER_TRIP = 2


def _diff_attn_kernel(lambda_init, tq, tk, q_ref, k_ref, v_ref, lam_ref, g_ref, o_ref,
                      vt_ref, qt_ref, acc_ref, m_ref, smax_ref, s_ref):
    S = k_ref.shape[0]
    n_kv = S // tk
    n_tasks = (S // tq) * n_kv
    dv = v_ref.shape[1]

    feat = lax.broadcasted_iota(jnp.int32, (q_ref.shape[1], tk), 0)

    def transpose_qv(i, carry):
        r = pl.ds(pl.multiple_of(i * tk, tk), tk)
        vt_ref[0:dv, r] = v_ref[r, :].astype(F32).T.astype(BF16)
        q_t = q_ref[r, :].astype(F32).T
        qt_ref[0, :, r] = jnp.where(feat < DA_HEAD_DIM, q_t, 0.0).astype(BF16)
        qt_ref[1, :, r] = jnp.where(feat >= DA_HEAD_DIM, q_t, 0.0).astype(BF16)
        return carry

    lax.fori_loop(0, n_kv, transpose_qv, 0)
    vt_ref[dv:dv + DEN_ROWS, :] = jnp.ones((DEN_ROWS, S), BF16)
    acc_ref[...] = jnp.zeros_like(acc_ref)

    def kv_rows(t):
        return pl.ds(pl.multiple_of(lax.rem(t, n_kv) * tk, tk), tk)

    def scores(mp, t, par):
        q_cols = pl.ds(pl.multiple_of(lax.div(t, n_kv) * tq, tq), tq)
        s = jnp.dot(k_ref[kv_rows(t), :], qt_ref[mp, :, q_cols], preferred_element_type=F32)
        s_ref[mp, par] = s
        smax_ref[mp, par] = jnp.max(s, axis=0, keepdims=True)

    def attend(mp, t, par):
        m_old = jnp.where(lax.rem(t, n_kv) == 0, -jnp.inf, m_ref[mp])
        m_new = jnp.maximum(m_old, smax_ref[mp, par])
        m_ref[mp] = m_new
        p = jnp.exp2((s_ref[mp, par] - m_new).astype(BF16))
        acc_ref[mp] = jnp.exp2(m_old - m_new) * acc_ref[mp] + jnp.dot(
            vt_ref[:, kv_rows(t)], p, preferred_element_type=F32)

    def finalize(qi):
        lam_p = lam_ref[...]
        lam = (jnp.exp(jnp.sum(lam_p[0:1, :] * lam_p[1:2, :], axis=-1, keepdims=True))
               - jnp.exp(jnp.sum(lam_p[2:3, :] * lam_p[3:4, :], axis=-1, keepdims=True)) + lambda_init)
        a0 = acc_ref[0]
        a1 = acc_ref[1]
        o = a0[0:dv] / a0[dv:dv + 1] - lam * (a1[0:dv] / a1[dv:dv + 1])
        o = o * lax.rsqrt(jnp.mean(o * o, axis=0, keepdims=True) + EPS) * g_ref[...]
        o_ref[pl.ds(pl.multiple_of(qi * tq, tq), tq), :] = (o * (1.0 - lambda_init)).T.astype(o_ref.dtype)

    def half_steps(t, par):
        scores(0, t + 1, 1 - par)
        attend(0, t, par)
        scores(1, t + 1, 1 - par)
        attend(1, t, par)

    unroll = ATTN_TASKS_PER_TRIP
    assert unroll % 2 == 0 and n_kv % unroll == 0 and n_tasks >= 2 * unroll
    scores(0, 0, 0)
    scores(1, 0, 0)

    def body(u, carry):
        t = unroll * u
        for i in range(unroll):
            half_steps(t + i, i % 2)

        @pl.when(lax.rem(t + unroll - 1, n_kv) == n_kv - 1)
        def _():
            finalize(lax.div(t, n_kv))

        return carry

    lax.fori_loop(0, n_tasks // unroll - 1, body, 0)
    for i in range(unroll - 1):
        half_steps(n_tasks - unroll + i, i % 2)
    attend(0, n_tasks - 1, 1)
    attend(1, n_tasks - 1, 1)
    finalize(S // tq - 1)


def _diff_attn(qkv3, lam_p, g_col, lambda_init, tq, tk):
    B, S, _ = qkv3.shape
    H = DA_HEADS
    col = lambda off: pl.BlockSpec((None, S, LANES), lambda b, h: (b, 0, off + h))
    return pl.pallas_call(
        functools.partial(_diff_attn_kernel, lambda_init, tq, tk),
        grid=(B, H),
        in_specs=[col(0), col(H), col(2 * H), _resident(lam_p.shape), _resident(g_col.shape)],
        out_specs=col(0),
        out_shape=jax.ShapeDtypeStruct((B, S, H * DA_V_DIM), BF16),
        scratch_shapes=[pltpu.VMEM((DA_V_DIM + DEN_ROWS, S), BF16),
                        pltpu.VMEM((2, LANES, S), BF16),
                        pltpu.VMEM((2, DA_V_DIM + DEN_ROWS, tq), F32),
                        pltpu.VMEM((2, 1, tq), F32),
                        pltpu.VMEM((2, 2, 1, tq), F32),
                        pltpu.VMEM((2, 2, tk, tq), F32)],
        compiler_params=pltpu.CompilerParams(
            dimension_semantics=("parallel", "parallel"), vmem_limit_bytes=VMEM_LIMIT),
        name="diff_attn",
    )(qkv3, qkv3, qkv3, lam_p, g_col)


def _hgrn_scores(q, kin, v, b, reverse):
    C, n_sub = HG_CHUNK, HG_CHUNK // HG_SUB
    blk = lambda j: slice(j * HG_SUB, (j + 1) * HG_SUB)
    bound = [b[j * HG_SUB:j * HG_SUB + 1, :] if reverse else b[(j + 1) * HG_SUB - 1:(j + 1) * HG_SUB, :]
             for j in range(n_sub)]
    b_far = bound[0] if reverse else bound[-1]
    decay = lambda e: jnp.exp2(e).astype(BF16)
    k_til = jnp.concatenate([kin[blk(j)] * decay(bound[j] - b[blk(j)]) for j in range(n_sub)], axis=0)
    rows = [slice(0, (j + 1) * HG_SUB) if reverse else slice(j * HG_SUB, C) for j in range(n_sub)]
    q_til = jnp.concatenate([q[rows[j]] * decay(b[rows[j]] - bound[j]) for j in range(n_sub)], axis=0)
    offs = np.cumsum([0] + [r.stop - r.start for r in rows])
    scores = lax.dot_general(q_til, k_til, (((1,), (1,)), ((), ())),
                             preferred_element_type=F32)
    increment = lax.dot_general(v, kin * decay(b_far - b), (((0,), (0,)), ((), ())),
                                preferred_element_type=F32)
    return scores, [int(o) for o in offs], q * decay(b), jnp.exp2(b_far), increment


def _hgrn_intra(scores, offs, v, reverse):
    C, n_sub = HG_CHUNK, HG_CHUNK // HG_SUB
    col = lax.broadcasted_iota(jnp.int32, (HG_SUB, C), 1)
    row_blocks = []
    for i in range(n_sub):
        js = range(i, n_sub) if reverse else range(0, i + 1)
        a_i = jnp.zeros((HG_SUB, C), F32)
        for j in js:
            r0 = offs[j] + (i * HG_SUB if reverse else (i - j) * HG_SUB)
            a_i = jnp.where(col // HG_SUB == j, scores[r0:r0 + HG_SUB, :], a_i)
        row_blocks.append(a_i)
    attn = jnp.concatenate(row_blocks, axis=0)
    t_i = lax.broadcasted_iota(jnp.int32, (C, C), 0)
    s_i = lax.broadcasted_iota(jnp.int32, (C, C), 1)
    attn = jnp.where((t_i <= s_i) if reverse else (t_i >= s_i), attn, 0.0)
    return jnp.dot(attn.astype(BF16), v, preferred_element_type=F32)


def _hgrn_kernel(q_f_ref, q_b_ref, v_f_ref, v_b_ref, bc_f_ref, bc_b_ref, kin_f_ref, kin_b_ref,
                 y_f_ref, y_b_ref, st_f_ref, st_b_ref):
    @pl.when(pl.program_id(2) == 0)
    def _():
        st_f_ref[...] = jnp.zeros_like(st_f_ref)
        st_b_ref[...] = jnp.zeros_like(st_b_ref)

    n_chunks = q_f_ref.shape[0] // HG_CHUNK
    directions = ((False, q_f_ref, v_f_ref, bc_f_ref, kin_f_ref, y_f_ref, st_f_ref),
                  (True, q_b_ref, v_b_ref, bc_b_ref, kin_b_ref, y_b_ref, st_b_ref))

    def body(u, carry):
        work = []
        for reverse, q_ref, v_ref, bc_ref, kin_ref, y_ref, st_ref in directions:
            for j in range(HG_CHUNKS_PER_TRIP):
                c = u * HG_CHUNKS_PER_TRIP + j
                cc = (n_chunks - 1 - c) if reverse else c
                rs = pl.ds(pl.multiple_of(cc * HG_CHUNK, HG_CHUNK), HG_CHUNK)
                v = v_ref[rs, :]
                stage1 = _hgrn_scores(q_ref[rs, :], kin_ref[rs, :], v, bc_ref[rs, :], reverse)
                work.append((reverse, rs, v, y_ref, st_ref, stage1))
        intra = [_hgrn_intra(w[5][0], w[5][1], w[2], w[0]) for w in work]
        states = {}
        for (reverse, rs, v, y_ref, st_ref, stage1), o_intra in zip(work, intra):
            _, _, q_state, decay, increment = stage1
            state_t = states.get(reverse)
            if state_t is None:
                state_t = st_ref[...]
            o_inter = lax.dot_general(q_state, state_t.astype(BF16), (((1,), (1,)), ((), ())),
                                      preferred_element_type=F32)
            states[reverse] = decay * state_t + increment
            y_ref[rs, :] = (o_intra + o_inter).astype(y_ref.dtype)
        for reverse, *_, st_ref in directions:
            st_ref[...] = states[reverse]
        return carry

    assert n_chunks % HG_CHUNKS_PER_TRIP == 0
    lax.fori_loop(0, n_chunks // HG_CHUNKS_PER_TRIP, body, 0)


def _hgrn(hq3, hv3, bc3, kin3, seg):
    B, S, W = hq3.shape
    H = W // HG_DIM
    n_seg = S // seg
    fwd = lambda off: pl.BlockSpec((None, seg, HG_DIM), lambda b, h, s: (b, s, off + h))
    bwd = lambda off: pl.BlockSpec((None, seg, HG_DIM), lambda b, h, s: (b, n_seg - 1 - s, off + h))
    return pl.pallas_call(
        _hgrn_kernel,
        grid=(B, H, n_seg),
        in_specs=[fwd(0), bwd(0), fwd(0), bwd(0), fwd(0), bwd(H), fwd(0), bwd(H)],
        out_specs=[fwd(0), bwd(0)],
        out_shape=[jax.ShapeDtypeStruct((B, S, W), BF16), jax.ShapeDtypeStruct((B, S, W), BF16)],
        scratch_shapes=[pltpu.VMEM((HG_DIM, HG_DIM), F32), pltpu.VMEM((HG_DIM, HG_DIM), F32)],
        compiler_params=pltpu.CompilerParams(
            dimension_semantics=("parallel", "parallel", "arbitrary"), vmem_limit_bytes=VMEM_LIMIT),
        name="hgrn2",
    )(hq3, hq3, hv3, hv3, bc3, bc3, kin3, kin3)


def _mixer_residual(x, oda, yf, yb, sg, g_hg, w_ref):
    half = oda.shape[1]
    y = yf.astype(F32) + yb.astype(F32)
    sg = sg.astype(F32)
    heads = []
    for hh in range(y.shape[1] // HG_DIM):
        cs = slice(hh * HG_DIM, (hh + 1) * HG_DIM)
        heads.append(_rms(y[:, cs], g_hg) * sg[:, cs])
    o_hg = jnp.concatenate(heads, axis=1).astype(BF16)
    return (x + jnp.dot(oda, w_ref[0:half, :], preferred_element_type=F32)
            + jnp.dot(o_hg, w_ref[half:, :], preferred_element_type=F32))


HALO_ROWS = 16


def _ffn_ple_kernel(fc, x_ref, oda_ref, yf_ref, yb_ref, sg_ref, xh_ref, odah_ref, yfh_ref, ybh_ref,
                    sgh_ref, p_ref, ghg_ref, wo_ref, gf_ref, wg_ref, wu_ref, cw_ref, cb_ref, wd_ref,
                    gp_ref, wpg_ref, wple_ref, gfin_ref, o_ref, h_ref, acc_ref):
    tm = x_ref.shape[0]
    ext = lambda tile_ref, halo_ref: jnp.concatenate([tile_ref[...], halo_ref[...]], axis=0)
    x1 = _mixer_residual(ext(x_ref, xh_ref), ext(oda_ref, odah_ref), ext(yf_ref, yfh_ref),
                         ext(yb_ref, ybh_ref), ext(sg_ref, sgh_ref), ghg_ref[...], wo_ref)
    xt = x1[0:tm]
    h_ref[...] = _rms(x1, gf_ref[...]).astype(BF16)
    assert sum(fc) == wg_ref.shape[1]
    edges = np.cumsum((0,) + tuple(fc))
    chunks = [slice(int(lo), int(hi)) for lo, hi in zip(edges[:-1], edges[1:])]

    def up_gate(cs):
        return (jnp.dot(h_ref[...], wg_ref[:, cs], preferred_element_type=F32),
                jnp.dot(h_ref[0:tm, :], wu_ref[:, cs], preferred_element_type=F32))

    nxt = up_gate(chunks[0])
    for c, cs in enumerate(chunks):
        a_ext, u = nxt
        if c + 1 < len(chunks):
            nxt = up_gate(chunks[c + 1])
        row = lax.broadcasted_iota(jnp.int32, u.shape, 0)
        a = a_ext[0:tm]
        a_prev = jnp.where(row == 0, a_ext[tm:tm + 1], pltpu.roll(a, 1, axis=0))
        a_next = jnp.where(row == tm - 1, a_ext[tm + 1:tm + 2], pltpu.roll(a, tm - 1, axis=0))
        cv = cb_ref[:, cs] + a_prev * cw_ref[0:1, cs] + a * cw_ref[1:2, cs] + a_next * cw_ref[2:3, cs]
        act = (_gelu(cv) * u).astype(BF16)
        d = jnp.dot(act, wd_ref[cs, :], preferred_element_type=F32)
        if c == 0:
            acc_ref[...] = d
        else:
            acc_ref[...] += d
    x2 = xt + acc_ref[...]
    gate = _sigmoid(jnp.dot(_rms(x2, gp_ref[...]).astype(BF16), wpg_ref[...], preferred_element_type=F32))
    ple = jnp.dot(p_ref[...].astype(BF16), wple_ref[...], preferred_element_type=F32)
    o_ref[...] = _rms(x2 + ple * gate, gfin_ref[...])


def _ffn_ple(streams, p2d, consts, tm, seq, fc):
    T, D = streams[0].shape
    row = lambda w: pl.BlockSpec((tm, w), lambda i: (i, 0))
    halo = lambda w: pl.BlockSpec((None, HALO_ROWS, w), lambda i: (i, 0, 0))
    halos = [_conv_halo(a, tm, seq) for a in streams]
    return pl.pallas_call(
        functools.partial(_ffn_ple_kernel, fc),
        grid=(T // tm,),
        in_specs=[row(a.shape[1]) for a in streams] + [halo(a.shape[1]) for a in streams]
                 + [row(p2d.shape[1])] + [_resident(c.shape) for c in consts],
        out_specs=row(D),
        out_shape=jax.ShapeDtypeStruct((T, D), F32),
        scratch_shapes=[pltpu.VMEM((tm + HALO_ROWS, D), BF16), pltpu.VMEM((tm, D), F32)],
        compiler_params=pltpu.CompilerParams(dimension_semantics=("parallel",),
                                             vmem_limit_bytes=VMEM_LIMIT),
        name="ffn_ple",
    )(*streams, *halos, p2d, *consts)


def _conv_halo(x1, tm, seq):
    T, D = x1.shape
    nt = T // tm
    xr = x1.reshape(nt, tm, D)
    zero = jnp.zeros((1, D), x1.dtype)
    prev = jnp.concatenate([zero, xr[:-1, tm - 1, :]], axis=0)
    nxt = jnp.concatenate([xr[1:, 0, :], zero], axis=0)
    start = (jnp.arange(nt) * tm) % seq
    prev = jnp.where((start == 0)[:, None], 0.0, prev)
    nxt = jnp.where((start + tm == seq)[:, None], 0.0, nxt)
    pad = jnp.zeros((nt, HALO_ROWS - 2, D), x1.dtype)
    return jnp.concatenate([prev[:, None, :], nxt[:, None, :], pad], axis=1)


def _rope_constants():
    half = ROT_DIM // 2
    inv_freq = (np.float32(ROPE_THETA) ** (-np.arange(half, dtype=np.float32) / np.float32(half))).astype(np.float32)
    d = np.arange(LANES) % DA_HEAD_DIM
    place = np.zeros((4 * half, 2 * LANES), np.float32)
    for f in range(half):
        hit = (d < ROT_DIM) & (d % half == f)
        sign = np.where(d < half, -1.0, 1.0)
        place[f, :LANES] = place[half + f, :LANES] = hit
        place[2 * half + f, LANES:] = place[3 * half + f, LANES:] = hit * sign
    return jnp.asarray(inv_freq.reshape(half, 1)), jnp.asarray(place, dtype=BF16)


def kernel(x, p, positions, norm_mix_g, w_in, lam_q1, lam_k1, lam_q2, lam_k2, da_subln_g, hg_lb_gamma, hg_norm_g, w_out, norm_ffn_g, w_ffn_gate, w_ffn_up, ffn_conv_w, ffn_conv_b, w_ffn_down, norm_ple_g, w_ple, w_ple_gate, final_norm_g):
    B, S, D = x.shape
    T = B * S
    depth = w_in.shape[0]
    tm = min(512, S)
    tq = min(512, S)
    tk = min(1024, S)
    seg = min(2048, S)
    fc = (768, 768, 768, 512)
    row2 = lambda v: v.reshape(1, -1)
    xc = x.reshape(T, D)
    pos = positions.reshape(T // tm, 1, tm)
    rope_freq, rope_place = _rope_constants()
    for i in range(depth):
        lambda_init = 0.8 - 0.6 * math.exp(-0.3 * i)
        gam = hg_lb_gamma.reshape(-1, hg_lb_gamma.shape[-1])
        qkv, hq, bc, kin, hv, sg = _in_proj(xc, pos, rope_freq, rope_place, row2(norm_mix_g[i]), w_in[i].astype(BF16),
                                            gam, i, tm)
        lam_p = jnp.stack([lam_q1[i], lam_k1[i], lam_q2[i], lam_k2[i]], axis=0)
        o_da = _diff_attn(qkv.reshape(B, S, -1), lam_p, da_subln_g[i].reshape(-1, 1), lambda_init, tq, tk)
        y_f, y_b = _hgrn(hq.reshape(B, S, -1), hv.reshape(B, S, -1), bc.reshape(B, S, -1),
                         kin.reshape(B, S, -1), seg)
        streams = (xc, o_da.reshape(T, -1), y_f.reshape(T, -1), y_b.reshape(T, -1), sg)
        consts = (row2(hg_norm_g[i]), w_out[i].astype(BF16), row2(norm_ffn_g[i]),
                  w_ffn_gate[i].astype(BF16), w_ffn_up[i].astype(BF16), ffn_conv_w[i],
                  row2(ffn_conv_b[i]), w_ffn_down[i].astype(BF16), row2(norm_ple_g[i]),
                  w_ple_gate[i].astype(BF16), w_ple[i].astype(BF16), row2(final_norm_g))
        xc = _ffn_ple(streams, p[i].reshape(T, -1), consts, tm, S, fc)
        assert i == depth - 1, "multi-layer stacks need the final norm split out of ffn_ple"
    return xc.reshape(B, S, D)
```

```python
import functools
import math

import numpy as np
import jax
import jax.numpy as jnp
from jax import lax
from jax.experimental import pallas as pl
from jax.experimental.pallas import tpu as pltpu

F32 = jnp.float32
BF16 = jnp.bfloat16

EPS = 1e-6
ROPE_THETA = 500000.0
DA_HEADS = 4
DA_HEAD_DIM = 64
DA_V_DIM = 128
ROT_DIM = 16
HG_HEADS = 4
HG_DIM = 128
CONV_WIDTH = 3
HG_CHUNK = 64
HG_SUB = 16
HG_CHUNKS_PER_TRIP = 4
LANES = 128
SUBLANES = 8
VMEM_LIMIT = 56 * 1024 * 1024


def _rms(xf, g):
    return xf * lax.rsqrt(jnp.mean(xf * xf, axis=-1, keepdims=True) + EPS) * g


def _sigmoid(z):
    return 0.5 + 0.5 * jnp.tanh(0.5 * z)


def _gelu(x):
    return 0.5 * x * (1.0 + lax.erf(x * (2.0 ** -0.5)))


def _resident(shape):
    return pl.BlockSpec(shape, lambda *_: (0,) * len(shape), pipeline_mode=pl.Buffered(1))


def _chunk_cumsum(x, reverse):
    n = x.shape[0]
    r = lax.broadcasted_iota(jnp.int32, x.shape, 0) & (HG_CHUNK - 1)
    s = 1
    while s < SUBLANES:
        if reverse:
            x = x + jnp.where(r < HG_CHUNK - s, pltpu.roll(x, n - s, axis=0), 0.0)
        else:
            x = x + jnp.where(r >= s, pltpu.roll(x, s, axis=0), 0.0)
        s *= 2
    while s < HG_CHUNK:
        parts = []
        for c0 in range(0, n, HG_CHUNK):
            lo, hi = x[c0:c0 + HG_CHUNK - s], x[c0 + s:c0 + HG_CHUNK]
            parts += [lo + hi, x[c0 + HG_CHUNK - s:c0 + HG_CHUNK]] if reverse else [x[c0:c0 + s], hi + lo]
        x = jnp.concatenate(parts, axis=0)
        s *= 2
    return x


def _in_proj_kernel(layer, n_slots, x_ref, pos_ref, freq_ref, place_ref, g_ref, w_ref, gam_ref,
                    qkv_ref, hq_ref, bc_ref, kin_ref, hv_ref, sg_ref):
    h = _rms(x_ref[...], g_ref[...]).astype(BF16)
    sec = qkv_ref.shape[1] // 3

    proj_all = jnp.dot(h, w_ref[...], preferred_element_type=F32)

    half = ROT_DIM // 2
    ang = freq_ref[...] * pos_ref[...].astype(F32)
    parts = []
    for trig in (jnp.cos(ang), jnp.sin(ang)):
        hi = trig.astype(BF16).astype(F32)
        parts += [hi, trig - hi]
    tab = lax.dot_general(jnp.concatenate(parts, axis=0).astype(BF16), place_ref[...],
                          (((0,), (0,)), ((), ())), preferred_element_type=F32)
    lane_d = lax.broadcasted_iota(jnp.int32, (1, LANES), 1) & (DA_HEAD_DIM - 1)
    cos = tab[:, :LANES] + jnp.where(lane_d >= ROT_DIM, 1.0, 0.0)
    sin = tab[:, LANES:]
    first_half = lane_d < half

    def rotary(t, scale):
        outs = []
        for hh in range(sec // LANES):
            blk = t[:, hh * LANES:(hh + 1) * LANES]
            partner = jnp.where(first_half, pltpu.roll(blk, LANES - half, axis=1),
                                pltpu.roll(blk, half, axis=1))
            rot = blk * cos + partner * sin
            outs.append(rot * scale if scale != 1.0 else rot)
        return jnp.concatenate(outs, axis=1)

    def put_q(t):
        qkv_ref[:, 0:sec] = rotary(t, DA_HEAD_DIM ** -0.5 * math.log2(math.e)).astype(BF16)

    def put_k(t):
        qkv_ref[:, sec:2 * sec] = rotary(t, 1.0).astype(BF16)

    def put_v(t):
        qkv_ref[:, 2 * sec:3 * sec] = t.astype(BF16)

    def put_hq(t):
        hq_ref[...] = t.astype(BF16)

    def put_forget(d, z):
        gam = gam_ref[d * n_slots:(d + 1) * n_slots, :]
        e = jnp.exp(gam - jnp.max(gam, axis=0, keepdims=True))
        lb = jnp.sum(e[0:layer + 1, :], axis=0, keepdims=True) / jnp.sum(e, axis=0, keepdims=True)
        half_span = 0.5 * (1.0 - lb)
        mid = 0.5 * (1.0 + lb)
        for r0 in range(0, z.shape[0], HG_CHUNK):
            w = half_span * jnp.tanh(0.5 * z[r0:r0 + HG_CHUNK])
            bc_ref[r0:r0 + HG_CHUNK, d * sec:(d + 1) * sec] = _chunk_cumsum(jnp.log2(mid + w),
                                                                             reverse=(d == 1))
            kin_ref[r0:r0 + HG_CHUNK, d * sec:(d + 1) * sec] = (half_span - w).astype(BF16)

    def put_hv(t):
        hv_ref[...] = t.astype(BF16)

    def put_gate(t):
        sg_ref[...] = (t * _sigmoid(t)).astype(BF16)

    epilogues = (put_q, put_k, put_v, put_hq, functools.partial(put_forget, 0),
                 functools.partial(put_forget, 1), put_hv, put_gate)
    for i, epilogue in enumerate(epilogues):
        epilogue(proj_all[:, i * sec:(i + 1) * sec])


def _in_proj(x2d, pos, freq, place, g, w_bf, gam, layer, tm):
    T, D = x2d.shape
    sec = w_bf.shape[1] // 8
    n_slots = gam.shape[0] // 2
    row = lambda w: pl.BlockSpec((tm, w), lambda i: (i, 0))
    return pl.pallas_call(
        functools.partial(_in_proj_kernel, layer, n_slots),
        grid=(T // tm,),
        in_specs=[row(D), pl.BlockSpec((None, 1, tm), lambda i: (i, 0, 0)),
                  _resident(freq.shape), _resident(place.shape), _resident(g.shape),
                  _resident(w_bf.shape), _resident(gam.shape)],
        out_specs=[row(3 * sec), row(sec), row(2 * sec), row(2 * sec), row(sec), row(sec)],
        out_shape=[jax.ShapeDtypeStruct((T, 3 * sec), BF16), jax.ShapeDtypeStruct((T, sec), BF16),
                   jax.ShapeDtypeStruct((T, 2 * sec), F32), jax.ShapeDtypeStruct((T, 2 * sec), BF16),
                   jax.ShapeDtypeStruct((T, sec), BF16), jax.ShapeDtypeStruct((T, sec), BF16)],
        compiler_params=pltpu.CompilerParams(dimension_semantics=("parallel",),
                                             vmem_limit_bytes=VMEM_LIMIT),
        name="in_proj",
    )(x2d, pos, freq, place, g, w_bf, gam)


DEN_ROWS = 16
ATTN_SEED_KEYS = 16
ATTN_MAX_JUMP = 100.0


def _diff_attn_kernel(lambda_init, tq, tk, q_ref, k_ref, v_ref, lam_ref, g_ref, o_ref,
                      vt_ref, qt_ref, acc_ref, s_ref):
    S = k_ref.shape[0]
    n_kv = S // tk
    dv = v_ref.shape[1]

    feat = lax.broadcasted_iota(jnp.int32, (q_ref.shape[1], tk), 0)

    def transpose_qv(i, carry):
        r = pl.ds(pl.multiple_of(i * tk, tk), tk)
        vt_ref[0:dv, r] = v_ref[r, :].astype(F32).T.astype(BF16)
        q_t = q_ref[r, :].astype(F32).T
        qt_ref[0, :, r] = jnp.where(feat < DA_HEAD_DIM, q_t, 0.0).astype(BF16)
        qt_ref[1, :, r] = jnp.where(feat >= DA_HEAD_DIM, q_t, 0.0).astype(BF16)
        return carry

    lax.fori_loop(0, n_kv, transpose_qv, 0)
    vt_ref[dv:dv + DEN_ROWS, :] = jnp.ones((DEN_ROWS, S), BF16)

    def scores(mp, rows, q_cols):
        return jnp.dot(k_ref[rows, :], qt_ref[mp, :, q_cols], preferred_element_type=F32)

    def one_pass_tile(q_cols):
        m = [jnp.max(scores(mp, slice(0, ATTN_SEED_KEYS), q_cols), axis=0, keepdims=True)
             for mp in range(2)]
        jump = jnp.zeros_like(m[0])
        chunk = lambda c: slice(c * tk, (c + 1) * tk)
        s_next = [scores(0, chunk(0), q_cols), scores(1, chunk(0), q_cols)]
        for c in range(n_kv):
            for mp in range(2):
                s = s_next[mp]
                m_prev = m[mp]
                m[mp] = jnp.maximum(m_prev, jnp.max(s, axis=0, keepdims=True))
                p = jnp.exp2((s - m_prev).astype(BF16))
                pv = jnp.dot(vt_ref[:, chunk(c)], p, preferred_element_type=F32)
                if c + 1 < n_kv:
                    s_next[mp] = scores(mp, chunk(c + 1), q_cols)
                rescale = jnp.exp2(m_prev - m[mp])
                acc_ref[mp] = rescale * (pv if c == 0 else acc_ref[mp] + pv)
                jump = jnp.maximum(jump, m[mp] - m_prev)
        return jnp.max(jump)

    def two_pass_tile(q_cols):
        def chunk_step(c, m):
            rows = pl.ds(pl.multiple_of(c * tk, tk), tk)
            m_out = []
            for mp in range(2):
                s_ref[...] = scores(mp, rows, q_cols)
                m_new = jnp.maximum(m[mp], jnp.max(s_ref[...], axis=0, keepdims=True))
                p = jnp.exp2((s_ref[...] - m_new).astype(BF16))
                acc_ref[mp] = jnp.exp2(m[mp] - m_new) * acc_ref[mp] + jnp.dot(
                    vt_ref[:, rows], p, preferred_element_type=F32)
                m_out.append(m_new)
            return tuple(m_out)

        acc_ref[...] = jnp.zeros_like(acc_ref)
        start = jnp.full((1, tq), -jnp.inf, F32)
        lax.fori_loop(0, n_kv, chunk_step, (start, start))

    def finalize(qi):
        lam_p = lam_ref[...]
        lam = (jnp.exp(jnp.sum(lam_p[0:1, :] * lam_p[1:2, :], axis=-1, keepdims=True))
               - jnp.exp(jnp.sum(lam_p[2:3, :] * lam_p[3:4, :], axis=-1, keepdims=True)) + lambda_init)
        a0 = acc_ref[0]
        a1 = acc_ref[1]
        o = a0[0:dv] / a0[dv:dv + 1] - lam * (a1[0:dv] / a1[dv:dv + 1])
        o = o * lax.rsqrt(jnp.mean(o * o, axis=0, keepdims=True) + EPS) * g_ref[...]
        o_ref[pl.ds(pl.multiple_of(qi * tq, tq), tq), :] = (o * (1.0 - lambda_init)).T.astype(o_ref.dtype)

    def query_tile(qi, carry):
        q_cols = pl.ds(pl.multiple_of(qi * tq, tq), tq)
        worst_jump = one_pass_tile(q_cols)

        @pl.when(jnp.logical_not(worst_jump <= ATTN_MAX_JUMP))
        def _():
            two_pass_tile(q_cols)

        finalize(qi)
        return carry

    lax.fori_loop(0, S // tq, query_tile, 0)


def _diff_attn(qkv3, lam_p, g_col, lambda_init, tq, tk):
    B, S, _ = qkv3.shape
    H = DA_HEADS
    col = lambda off: pl.BlockSpec((None, S, LANES), lambda b, h: (b, 0, off + h))
    return pl.pallas_call(
        functools.partial(_diff_attn_kernel, lambda_init, tq, tk),
        grid=(B, H),
        in_specs=[col(0), col(H), col(2 * H), _resident(lam_p.shape), _resident(g_col.shape)],
        out_specs=col(0),
        out_shape=jax.ShapeDtypeStruct((B, S, H * DA_V_DIM), BF16),
        scratch_shapes=[pltpu.VMEM((DA_V_DIM + DEN_ROWS, S), BF16),
                        pltpu.VMEM((2, LANES, S), BF16),
                        pltpu.VMEM((2, DA_V_DIM + DEN_ROWS, tq), F32),
                        pltpu.VMEM((tk, tq), F32)],
        compiler_params=pltpu.CompilerParams(
            dimension_semantics=("parallel", "parallel"), vmem_limit_bytes=VMEM_LIMIT),
        name="diff_attn",
    )(qkv3, qkv3, qkv3, lam_p, g_col)


def _hgrn_scores(q, kin, v, b, reverse):
    C, n_sub = HG_CHUNK, HG_CHUNK // HG_SUB
    blk = lambda j: slice(j * HG_SUB, (j + 1) * HG_SUB)
    bound = [b[j * HG_SUB:j * HG_SUB + 1, :] if reverse else b[(j + 1) * HG_SUB - 1:(j + 1) * HG_SUB, :]
             for j in range(n_sub)]
    b_far = bound[0] if reverse else bound[-1]
    decay = lambda e: jnp.exp2(e).astype(BF16)
    k_til = jnp.concatenate([kin[blk(j)] * decay(bound[j] - b[blk(j)]) for j in range(n_sub)], axis=0)
    rows = [slice(0, (j + 1) * HG_SUB) if reverse else slice(j * HG_SUB, C) for j in range(n_sub)]
    q_til = jnp.concatenate([q[rows[j]] * decay(b[rows[j]] - bound[j]) for j in range(n_sub)], axis=0)
    offs = np.cumsum([0] + [r.stop - r.start for r in rows])
    scores = lax.dot_general(q_til, k_til, (((1,), (1,)), ((), ())),
                             preferred_element_type=F32)
    increment = lax.dot_general(v, kin * decay(b_far - b), (((0,), (0,)), ((), ())),
                                preferred_element_type=F32)
    return scores, [int(o) for o in offs], q * decay(b), jnp.exp2(b_far), increment


def _hgrn_intra(scores, offs, v, reverse):
    C, n_sub = HG_CHUNK, HG_CHUNK // HG_SUB
    col = lax.broadcasted_iota(jnp.int32, (HG_SUB, C), 1)
    row_blocks = []
    for i in range(n_sub):
        js = range(i, n_sub) if reverse else range(0, i + 1)
        a_i = jnp.zeros((HG_SUB, C), F32)
        for j in js:
            r0 = offs[j] + (i * HG_SUB if reverse else (i - j) * HG_SUB)
            a_i = jnp.where(col // HG_SUB == j, scores[r0:r0 + HG_SUB, :], a_i)
        row_blocks.append(a_i)
    attn = jnp.concatenate(row_blocks, axis=0)
    t_i = lax.broadcasted_iota(jnp.int32, (C, C), 0)
    s_i = lax.broadcasted_iota(jnp.int32, (C, C), 1)
    attn = jnp.where((t_i <= s_i) if reverse else (t_i >= s_i), attn, 0.0)
    return jnp.dot(attn.astype(BF16), v, preferred_element_type=F32)


def _hgrn_kernel(q_f_ref, q_b_ref, v_f_ref, v_b_ref, bc_f_ref, bc_b_ref, kin_f_ref, kin_b_ref,
                 y_f_ref, y_b_ref, st_f_ref, st_b_ref):
    @pl.when(pl.program_id(2) == 0)
    def _():
        st_f_ref[...] = jnp.zeros_like(st_f_ref)
        st_b_ref[...] = jnp.zeros_like(st_b_ref)

    n_chunks = q_f_ref.shape[0] // HG_CHUNK
    directions = ((False, q_f_ref, v_f_ref, bc_f_ref, kin_f_ref, y_f_ref, st_f_ref),
                  (True, q_b_ref, v_b_ref, bc_b_ref, kin_b_ref, y_b_ref, st_b_ref))

    def body(u, carry):
        work = []
        for reverse, q_ref, v_ref, bc_ref, kin_ref, y_ref, st_ref in directions:
            for j in range(HG_CHUNKS_PER_TRIP):
                c = u * HG_CHUNKS_PER_TRIP + j
                cc = (n_chunks - 1 - c) if reverse else c
                rs = pl.ds(pl.multiple_of(cc * HG_CHUNK, HG_CHUNK), HG_CHUNK)
                v = v_ref[rs, :]
                stage1 = _hgrn_scores(q_ref[rs, :], kin_ref[rs, :], v, bc_ref[rs, :], reverse)
                work.append((reverse, rs, v, y_ref, st_ref, stage1))
        intra = [_hgrn_intra(w[5][0], w[5][1], w[2], w[0]) for w in work]
        states = {}
        for (reverse, rs, v, y_ref, st_ref, stage1), o_intra in zip(work, intra):
            _, _, q_state, decay, increment = stage1
            state_t = states.get(reverse)
            if state_t is None:
                state_t = st_ref[...]
            o_inter = lax.dot_general(q_state, state_t.astype(BF16), (((1,), (1,)), ((), ())),
                                      preferred_element_type=F32)
            states[reverse] = decay * state_t + increment
            y_ref[rs, :] = (o_intra + o_inter).astype(y_ref.dtype)
        for reverse, *_, st_ref in directions:
            st_ref[...] = states[reverse]
        return carry

    assert n_chunks % HG_CHUNKS_PER_TRIP == 0
    lax.fori_loop(0, n_chunks // HG_CHUNKS_PER_TRIP, body, 0)


def _hgrn(hq3, hv3, bc3, kin3, seg):
    B, S, W = hq3.shape
    H = W // HG_DIM
    n_seg = S // seg
    fwd = lambda off: pl.BlockSpec((None, seg, HG_DIM), lambda b, h, s: (b, s, off + h))
    bwd = lambda off: pl.BlockSpec((None, seg, HG_DIM), lambda b, h, s: (b, n_seg - 1 - s, off + h))
    return pl.pallas_call(
        _hgrn_kernel,
        grid=(B, H, n_seg),
        in_specs=[fwd(0), bwd(0), fwd(0), bwd(0), fwd(0), bwd(H), fwd(0), bwd(H)],
        out_specs=[fwd(0), bwd(0)],
        out_shape=[jax.ShapeDtypeStruct((B, S, W), BF16), jax.ShapeDtypeStruct((B, S, W), BF16)],
        scratch_shapes=[pltpu.VMEM((HG_DIM, HG_DIM), F32), pltpu.VMEM((HG_DIM, HG_DIM), F32)],
        compiler_params=pltpu.CompilerParams(
            dimension_semantics=("parallel", "parallel", "arbitrary"), vmem_limit_bytes=VMEM_LIMIT),
        name="hgrn2",
    )(hq3, hq3, hv3, hv3, bc3, bc3, kin3, kin3)


def _mixer_residual(x, oda, yf, yb, sg, g_hg, w_ref):
    half = oda.shape[1]
    y = yf.astype(F32) + yb.astype(F32)
    sg = sg.astype(F32)
    heads = []
    for hh in range(y.shape[1] // HG_DIM):
        cs = slice(hh * HG_DIM, (hh + 1) * HG_DIM)
        heads.append(_rms(y[:, cs], g_hg) * sg[:, cs])
    o_hg = jnp.concatenate(heads, axis=1).astype(BF16)
    return (x + jnp.dot(oda, w_ref[0:half, :], preferred_element_type=F32)
            + jnp.dot(o_hg, w_ref[half:, :], preferred_element_type=F32))


HALO_ROWS = 16


def _ffn_ple_kernel(fc, x_ref, oda_ref, yf_ref, yb_ref, sg_ref, xh_ref, odah_ref, yfh_ref, ybh_ref,
                    sgh_ref, p_ref, ghg_ref, wo_ref, gf_ref, wg_ref, wu_ref, cw_ref, cb_ref, wd_ref,
                    gp_ref, wpg_ref, wple_ref, gfin_ref, o_ref, h_ref, acc_ref):
    tm = x_ref.shape[0]
    ext = lambda tile_ref, halo_ref: jnp.concatenate([tile_ref[...], halo_ref[...]], axis=0)
    x1 = _mixer_residual(ext(x_ref, xh_ref), ext(oda_ref, odah_ref), ext(yf_ref, yfh_ref),
                         ext(yb_ref, ybh_ref), ext(sg_ref, sgh_ref), ghg_ref[...], wo_ref)
    xt = x1[0:tm]
    h_ref[...] = _rms(x1, gf_ref[...]).astype(BF16)
    assert sum(fc) == wg_ref.shape[1]
    edges = np.cumsum((0,) + tuple(fc))
    chunks = [slice(int(lo), int(hi)) for lo, hi in zip(edges[:-1], edges[1:])]

    def up_gate(cs):
        return (jnp.dot(h_ref[...], wg_ref[:, cs], preferred_element_type=F32),
                jnp.dot(h_ref[0:tm, :], wu_ref[:, cs], preferred_element_type=F32))

    nxt = up_gate(chunks[0])
    for c, cs in enumerate(chunks):
        a_ext, u = nxt
        if c + 1 < len(chunks):
            nxt = up_gate(chunks[c + 1])
        row = lax.broadcasted_iota(jnp.int32, u.shape, 0)
        a = a_ext[0:tm]
        a_prev = jnp.where(row == 0, a_ext[tm:tm + 1], pltpu.roll(a, 1, axis=0))
        a_next = jnp.where(row == tm - 1, a_ext[tm + 1:tm + 2], pltpu.roll(a, tm - 1, axis=0))
        cv = cb_ref[:, cs] + a_prev * cw_ref[0:1, cs] + a * cw_ref[1:2, cs] + a_next * cw_ref[2:3, cs]
        act = (_gelu(cv) * u).astype(BF16)
        d = jnp.dot(act, wd_ref[cs, :], preferred_element_type=F32)
        if c == 0:
            acc_ref[...] = d
        else:
            acc_ref[...] += d
    x2 = xt + acc_ref[...]
    gate = _sigmoid(jnp.dot(_rms(x2, gp_ref[...]).astype(BF16), wpg_ref[...], preferred_element_type=F32))
    ple = jnp.dot(p_ref[...].astype(BF16), wple_ref[...], preferred_element_type=F32)
    o_ref[...] = _rms(x2 + ple * gate, gfin_ref[...])


def _ffn_ple(streams, p2d, consts, tm, seq, fc):
    T, D = streams[0].shape
    row = lambda w: pl.BlockSpec((tm, w), lambda i: (i, 0))
    halo = lambda w: pl.BlockSpec((None, HALO_ROWS, w), lambda i: (i, 0, 0))
    halos = [_conv_halo(a, tm, seq) for a in streams]
    return pl.pallas_call(
        functools.partial(_ffn_ple_kernel, fc),
        grid=(T // tm,),
        in_specs=[row(a.shape[1]) for a in streams] + [halo(a.shape[1]) for a in streams]
                 + [row(p2d.shape[1])] + [_resident(c.shape) for c in consts],
        out_specs=row(D),
        out_shape=jax.ShapeDtypeStruct((T, D), F32),
        scratch_shapes=[pltpu.VMEM((tm + HALO_ROWS, D), BF16), pltpu.VMEM((tm, D), F32)],
        compiler_params=pltpu.CompilerParams(dimension_semantics=("parallel",),
                                             vmem_limit_bytes=VMEM_LIMIT),
        name="ffn_ple",
    )(*streams, *halos, p2d, *consts)


def _conv_halo(x1, tm, seq):
    T, D = x1.shape
    nt = T // tm
    xr = x1.reshape(nt, tm, D)
    zero = jnp.zeros((1, D), x1.dtype)
    prev = jnp.concatenate([zero, xr[:-1, tm - 1, :]], axis=0)
    nxt = jnp.concatenate([xr[1:, 0, :], zero], axis=0)
    start = (jnp.arange(nt) * tm) % seq
    prev = jnp.where((start == 0)[:, None], 0.0, prev)
    nxt = jnp.where((start + tm == seq)[:, None], 0.0, nxt)
    pad = jnp.zeros((nt, HALO_ROWS - 2, D), x1.dtype)
    return jnp.concatenate([prev[:, None, :], nxt[:, None, :], pad], axis=1)


def _rope_constants():
    half = ROT_DIM // 2
    inv_freq = (np.float32(ROPE_THETA) ** (-np.arange(half, dtype=np.float32) / np.float32(half))).astype(np.float32)
    d = np.arange(LANES) % DA_HEAD_DIM
    place = np.zeros((4 * half, 2 * LANES), np.float32)
    for f in range(half):
        hit = (d < ROT_DIM) & (d % half == f)
        sign = np.where(d < half, -1.0, 1.0)
        place[f, :LANES] = place[half + f, :LANES] = hit
        place[2 * half + f, LANES:] = place[3 * half + f, LANES:] = hit * sign
    return jnp.asarray(inv_freq.reshape(half, 1)), jnp.asarray(place, dtype=BF16)


def kernel(x, p, positions, norm_mix_g, w_in, lam_q1, lam_k1, lam_q2, lam_k2, da_subln_g, hg_lb_gamma, hg_norm_g, w_out, norm_ffn_g, w_ffn_gate, w_ffn_up, ffn_conv_w, ffn_conv_b, w_ffn_down, norm_ple_g, w_ple, w_ple_gate, final_norm_g):
    B, S, D = x.shape
    T = B * S
    depth = w_in.shape[0]
    tm = min(512, S)
    tq = min(512, S)
    tk = min(1024, S)
    seg = min(2048, S)
    fc = (768, 768, 768, 512)
    row2 = lambda v: v.reshape(1, -1)
    xc = x.reshape(T, D)
    pos = positions.reshape(T // tm, 1, tm)
    rope_freq, rope_place = _rope_constants()
    for i in range(depth):
        lambda_init = 0.8 - 0.6 * math.exp(-0.3 * i)
        gam = hg_lb_gamma.reshape(-1, hg_lb_gamma.shape[-1])
        qkv, hq, bc, kin, hv, sg = _in_proj(xc, pos, rope_freq, rope_place, row2(norm_mix_g[i]), w_in[i].astype(BF16),
                                            gam, i, tm)
        lam_p = jnp.stack([lam_q1[i], lam_k1[i], lam_q2[i], lam_k2[i]], axis=0)
        o_da = _diff_attn(qkv.reshape(B, S, -1), lam_p, da_subln_g[i].reshape(-1, 1), lambda_init, tq, tk)
        y_f, y_b = _hgrn(hq.reshape(B, S, -1), hv.reshape(B, S, -1), bc.reshape(B, S, -1),
                         kin.reshape(B, S, -1), seg)
        streams = (xc, o_da.reshape(T, -1), y_f.reshape(T, -1), y_b.reshape(T, -1), sg)
        consts = (row2(hg_norm_g[i]), w_out[i].astype(BF16), row2(norm_ffn_g[i]),
                  w_ffn_gate[i].astype(BF16), w_ffn_up[i].astype(BF16), ffn_conv_w[i],
                  row2(ffn_conv_b[i]), w_ffn_down[i].astype(BF16), row2(norm_ple_g[i]),
                  w_ple_gate[i].astype(BF16), w_ple[i].astype(BF16), row2(final_norm_g))
        xc = _ffn_ple(streams, p[i].reshape(T, -1), consts, tm, S, fc)
        assert i == depth - 1, "multi-layer stacks need the final norm split out of ffn_ple"
    return xc.reshape(B, S, D)
```

```python
import functools
import math

import numpy as np
import jax
import jax.numpy as jnp
from jax import lax
from jax.experimental import pallas as pl
from jax.experimental.pallas import tpu as pltpu

F32 = jnp.float32
BF16 = jnp.bfloat16

EPS = 1e-6
ROPE_THETA = 500000.0
DA_HEADS = 4
DA_HEAD_DIM = 64
DA_V_DIM = 128
ROT_DIM = 16
HG_HEADS = 4
HG_DIM = 128
CONV_WIDTH = 3
HG_CHUNK = 64
HG_SUB = 16
HG_CHUNKS_PER_TRIP = 4
LANES = 128
SUBLANES = 8
VMEM_LIMIT = 56 * 1024 * 1024


def _rms(xf, g):
    return xf * lax.rsqrt(jnp.mean(xf * xf, axis=-1, keepdims=True) + EPS) * g


def _sigmoid(z):
    return 0.5 + 0.5 * jnp.tanh(0.5 * z)


def _gelu(x):
    return 0.5 * x * (1.0 + lax.erf(x * (2.0 ** -0.5)))


def _resident(shape):
    return pl.BlockSpec(shape, lambda *_: (0,) * len(shape), pipeline_mode=pl.Buffered(1))


def _chunk_cumsum(x, reverse):
    n = x.shape[0]
    r = lax.broadcasted_iota(jnp.int32, x.shape, 0) & (HG_CHUNK - 1)
    s = 1
    while s < SUBLANES:
        if reverse:
            x = x + jnp.where(r < HG_CHUNK - s, pltpu.roll(x, n - s, axis=0), 0.0)
        else:
            x = x + jnp.where(r >= s, pltpu.roll(x, s, axis=0), 0.0)
        s *= 2
    while s < HG_CHUNK:
        parts = []
        for c0 in range(0, n, HG_CHUNK):
            lo, hi = x[c0:c0 + HG_CHUNK - s], x[c0 + s:c0 + HG_CHUNK]
            parts += [lo + hi, x[c0 + HG_CHUNK - s:c0 + HG_CHUNK]] if reverse else [x[c0:c0 + s], hi + lo]
        x = jnp.concatenate(parts, axis=0)
        s *= 2
    return x


def _in_proj_kernel(layer, n_slots, x_ref, pos_ref, freq_ref, place_ref, g_ref, w_ref, gam_ref,
                    qkv_ref, hq_ref, bc_ref, kin_ref, hv_ref, sg_ref):
    h = _rms(x_ref[...], g_ref[...]).astype(BF16)
    sec = qkv_ref.shape[1] // 3

    proj_all = jnp.dot(h, w_ref[...], preferred_element_type=F32)

    half = ROT_DIM // 2
    ang = freq_ref[...] * pos_ref[...].astype(F32)
    parts = []
    for trig in (jnp.cos(ang), jnp.sin(ang)):
        hi = trig.astype(BF16).astype(F32)
        parts += [hi, trig - hi]
    tab = lax.dot_general(jnp.concatenate(parts, axis=0).astype(BF16), place_ref[...],
                          (((0,), (0,)), ((), ())), preferred_element_type=F32)
    lane_d = lax.broadcasted_iota(jnp.int32, (1, LANES), 1) & (DA_HEAD_DIM - 1)
    cos = tab[:, :LANES] + jnp.where(lane_d >= ROT_DIM, 1.0, 0.0)
    sin = tab[:, LANES:]
    first_half = lane_d < half

    def rotary(t, scale):
        outs = []
        for hh in range(sec // LANES):
            blk = t[:, hh * LANES:(hh + 1) * LANES]
            partner = jnp.where(first_half, pltpu.roll(blk, LANES - half, axis=1),
                                pltpu.roll(blk, half, axis=1))
            rot = blk * cos + partner * sin
            outs.append(rot * scale if scale != 1.0 else rot)
        return jnp.concatenate(outs, axis=1)

    def put_q(t):
        qkv_ref[:, 0:sec] = rotary(t, DA_HEAD_DIM ** -0.5 * math.log2(math.e)).astype(BF16)

    def put_k(t):
        qkv_ref[:, sec:2 * sec] = rotary(t, 1.0).astype(BF16)

    def put_v(t):
        qkv_ref[:, 2 * sec:3 * sec] = t.astype(BF16)

    def put_hq(t):
        hq_ref[...] = t.astype(BF16)

    def put_forget(d, z):
        gam = gam_ref[d * n_slots:(d + 1) * n_slots, :]
        e = jnp.exp(gam - jnp.max(gam, axis=0, keepdims=True))
        lb = jnp.sum(e[0:layer + 1, :], axis=0, keepdims=True) / jnp.sum(e, axis=0, keepdims=True)
        half_span = 0.5 * (1.0 - lb)
        mid = 0.5 * (1.0 + lb)
        for r0 in range(0, z.shape[0], HG_CHUNK):
            w = half_span * jnp.tanh(0.5 * z[r0:r0 + HG_CHUNK])
            bc_ref[r0:r0 + HG_CHUNK, d * sec:(d + 1) * sec] = _chunk_cumsum(jnp.log2(mid + w),
                                                                             reverse=(d == 1))
            kin_ref[r0:r0 + HG_CHUNK, d * sec:(d + 1) * sec] = (half_span - w).astype(BF16)

    def put_hv(t):
        hv_ref[...] = t.astype(BF16)

    def put_gate(t):
        sg_ref[...] = (t * _sigmoid(t)).astype(BF16)

    epilogues = (put_q, put_k, put_v, put_hq, functools.partial(put_forget, 0),
                 functools.partial(put_forget, 1), put_hv, put_gate)
    for i, epilogue in enumerate(epilogues):
        epilogue(proj_all[:, i * sec:(i + 1) * sec])


def _in_proj(x2d, pos, freq, place, g, w_bf, gam, layer, tm):
    T, D = x2d.shape
    sec = w_bf.shape[1] // 8
    n_slots = gam.shape[0] // 2
    row = lambda w: pl.BlockSpec((tm, w), lambda i: (i, 0))
    return pl.pallas_call(
        functools.partial(_in_proj_kernel, layer, n_slots),
        grid=(T // tm,),
        in_specs=[row(D), pl.BlockSpec((None, 1, tm), lambda i: (i, 0, 0)),
                  _resident(freq.shape), _resident(place.shape), _resident(g.shape),
                  _resident(w_bf.shape), _resident(gam.shape)],
        out_specs=[row(3 * sec), row(sec), row(2 * sec), row(2 * sec), row(sec), row(sec)],
        out_shape=[jax.ShapeDtypeStruct((T, 3 * sec), BF16), jax.ShapeDtypeStruct((T, sec), BF16),
                   jax.ShapeDtypeStruct((T, 2 * sec), F32), jax.ShapeDtypeStruct((T, 2 * sec), BF16),
                   jax.ShapeDtypeStruct((T, sec), BF16), jax.ShapeDtypeStruct((T, sec), BF16)],
        compiler_params=pltpu.CompilerParams(dimension_semantics=("parallel",),
                                             vmem_limit_bytes=VMEM_LIMIT),
        name="in_proj",
    )(x2d, pos, freq, place, g, w_bf, gam)


DEN_ROWS = 16
ATTN_SEED_KEYS = 16
ATTN_MAX_JUMP = 100.0


def _diff_attn_kernel(lambda_init, tq, tk, q_ref, k_ref, v_ref, lam_ref, g_ref, o_ref,
                      vt_ref, qt_ref, acc_ref, s_ref, seed_ref, first_ref):
    S = k_ref.shape[0]
    n_kv = S // tk
    dv = v_ref.shape[1]

    feat = lax.broadcasted_iota(jnp.int32, (q_ref.shape[1], tk), 0)

    def transpose_qv(i, carry):
        r = pl.ds(pl.multiple_of(i * tk, tk), tk)
        vt_ref[0:dv, r] = v_ref[r, :].astype(F32).T.astype(BF16)
        q_t = q_ref[r, :].astype(F32).T
        qt_ref[0, :, r] = jnp.where(feat < DA_HEAD_DIM, q_t, 0.0).astype(BF16)
        qt_ref[1, :, r] = jnp.where(feat >= DA_HEAD_DIM, q_t, 0.0).astype(BF16)
        return carry

    lax.fori_loop(0, n_kv, transpose_qv, 0)
    vt_ref[dv:dv + DEN_ROWS, :] = jnp.ones((DEN_ROWS, S), BF16)

    def scores(mp, rows, q_cols):
        return jnp.dot(k_ref[rows, :], qt_ref[mp, :, q_cols], preferred_element_type=F32)

    n_tiles = S // tq
    chunk = lambda c: slice(c * tk, (c + 1) * tk)
    tile_cols = lambda qi: pl.ds(pl.multiple_of(qi * tq, tq), tq)

    def open_tile(mp, q_cols):
        seed_ref[mp] = jnp.max(scores(mp, slice(0, ATTN_SEED_KEYS), q_cols), axis=0, keepdims=True)
        first_ref[mp] = scores(mp, chunk(0), q_cols)

    def one_pass_tile(q_cols, next_cols, acc, after_first_chunk):
        m = [seed_ref[0], seed_ref[1]]
        jump = jnp.zeros_like(m[0])
        s_cur = [first_ref[0], first_ref[1]]
        for c in range(n_kv):
            for mp in range(2):
                s = s_cur[mp]
                if c + 1 < n_kv:
                    s_cur[mp] = scores(mp, chunk(c + 1), q_cols)
                else:
                    open_tile(mp, next_cols)
                m_prev = m[mp]
                m[mp] = jnp.maximum(m_prev, jnp.max(s, axis=0, keepdims=True))
                p = jnp.exp2((s - m_prev).astype(BF16))
                pv = jnp.dot(vt_ref[:, chunk(c)], p, preferred_element_type=F32)
                rescale = jnp.exp2(m_prev - m[mp])
                acc[mp] = rescale * (pv if c == 0 else acc[mp] + pv)
                jump = jnp.maximum(jump, m[mp] - m_prev)
            if c == 0:
                after_first_chunk()
        return jnp.max(jump)

    def two_pass_tile(q_cols, acc):
        def chunk_step(c, m):
            rows = pl.ds(pl.multiple_of(c * tk, tk), tk)
            m_out = []
            for mp in range(2):
                s_ref[...] = scores(mp, rows, q_cols)
                m_new = jnp.maximum(m[mp], jnp.max(s_ref[...], axis=0, keepdims=True))
                p = jnp.exp2((s_ref[...] - m_new).astype(BF16))
                acc[mp] = jnp.exp2(m[mp] - m_new) * acc[mp] + jnp.dot(
                    vt_ref[:, rows], p, preferred_element_type=F32)
                m_out.append(m_new)
            return tuple(m_out)

        for mp in range(2):
            acc[mp] = jnp.zeros(acc.shape[1:], F32)
        start = jnp.full((1, tq), -jnp.inf, F32)
        lax.fori_loop(0, n_kv, chunk_step, (start, start))

    def finalize(qi, acc):
        lam_p = lam_ref[...]
        lam = (jnp.exp(jnp.sum(lam_p[0:1, :] * lam_p[1:2, :], axis=-1, keepdims=True))
               - jnp.exp(jnp.sum(lam_p[2:3, :] * lam_p[3:4, :], axis=-1, keepdims=True)) + lambda_init)
        a0 = acc[0]
        a1 = acc[1]
        o = a0[0:dv] / a0[dv:dv + 1] - lam * (a1[0:dv] / a1[dv:dv + 1])
        o = o * lax.rsqrt(jnp.mean(o * o, axis=0, keepdims=True) + EPS) * g_ref[...]
        o_ref[tile_cols(qi), :] = (o * (1.0 - lambda_init)).T.astype(o_ref.dtype)

    acc_ref[1] = jnp.ones(acc_ref.shape[1:], F32)
    for mp in range(2):
        open_tile(mp, tile_cols(0))

    def query_tile(qi, carry):
        slot = lax.rem(qi, 2)
        q_cols = tile_cols(qi)
        worst_jump = one_pass_tile(
            q_cols, tile_cols(jnp.minimum(qi + 1, n_tiles - 1)), acc_ref.at[slot],
            after_first_chunk=lambda: finalize(jnp.maximum(qi - 1, 0), acc_ref.at[1 - slot]))

        @pl.when(jnp.logical_not(worst_jump <= ATTN_MAX_JUMP))
        def _():
            two_pass_tile(q_cols, acc_ref.at[slot])

        return carry

    lax.fori_loop(0, n_tiles, query_tile, 0)
    finalize(n_tiles - 1, acc_ref.at[(n_tiles - 1) % 2])


def _diff_attn(qkv3, lam_p, g_col, lambda_init, tq, tk):
    B, S, _ = qkv3.shape
    H = DA_HEADS
    col = lambda off: pl.BlockSpec((None, S, LANES), lambda b, h: (b, 0, off + h))
    return pl.pallas_call(
        functools.partial(_diff_attn_kernel, lambda_init, tq, tk),
        grid=(B, H),
        in_specs=[col(0), col(H), col(2 * H), _resident(lam_p.shape), _resident(g_col.shape)],
        out_specs=col(0),
        out_shape=jax.ShapeDtypeStruct((B, S, H * DA_V_DIM), BF16),
        scratch_shapes=[pltpu.VMEM((DA_V_DIM + DEN_ROWS, S), BF16),
                        pltpu.VMEM((2, LANES, S), BF16),
                        pltpu.VMEM((2, 2, DA_V_DIM + DEN_ROWS, tq), F32),
                        pltpu.VMEM((tk, tq), F32),
                        pltpu.VMEM((2, 1, tq), F32),
                        pltpu.VMEM((2, tk, tq), F32)],
        compiler_params=pltpu.CompilerParams(
            dimension_semantics=("parallel", "parallel"), vmem_limit_bytes=VMEM_LIMIT),
        name="diff_attn",
    )(qkv3, qkv3, qkv3, lam_p, g_col)


def _hgrn_scores(q, kin, v, b, reverse):
    C, n_sub = HG_CHUNK, HG_CHUNK // HG_SUB
    blk = lambda j: slice(j * HG_SUB, (j + 1) * HG_SUB)
    bound = [b[j * HG_SUB:j * HG_SUB + 1, :] if reverse else b[(j + 1) * HG_SUB - 1:(j + 1) * HG_SUB, :]
             for j in range(n_sub)]
    b_far = bound[0] if reverse else bound[-1]
    decay = lambda e: jnp.exp2(e).astype(BF16)
    k_til = jnp.concatenate([kin[blk(j)] * decay(bound[j] - b[blk(j)]) for j in range(n_sub)], axis=0)
    rows = [slice(0, (j + 1) * HG_SUB) if reverse else slice(j * HG_SUB, C) for j in range(n_sub)]
    q_til = jnp.concatenate([q[rows[j]] * decay(b[rows[j]] - bound[j]) for j in range(n_sub)], axis=0)
    offs = np.cumsum([0] + [r.stop - r.start for r in rows])
    scores = lax.dot_general(q_til, k_til, (((1,), (1,)), ((), ())),
                             preferred_element_type=F32)
    increment = lax.dot_general(v, kin * decay(b_far - b), (((0,), (0,)), ((), ())),
                                preferred_element_type=F32)
    return scores, [int(o) for o in offs], q * decay(b), jnp.exp2(b_far), increment


def _hgrn_intra(scores, offs, v, reverse):
    C, n_sub = HG_CHUNK, HG_CHUNK // HG_SUB
    col = lax.broadcasted_iota(jnp.int32, (HG_SUB, C), 1)
    row_blocks = []
    for i in range(n_sub):
        js = range(i, n_sub) if reverse else range(0, i + 1)
        a_i = jnp.zeros((HG_SUB, C), F32)
        for j in js:
            r0 = offs[j] + (i * HG_SUB if reverse else (i - j) * HG_SUB)
            a_i = jnp.where(col // HG_SUB == j, scores[r0:r0 + HG_SUB, :], a_i)
        row_blocks.append(a_i)
    attn = jnp.concatenate(row_blocks, axis=0)
    t_i = lax.broadcasted_iota(jnp.int32, (C, C), 0)
    s_i = lax.broadcasted_iota(jnp.int32, (C, C), 1)
    attn = jnp.where((t_i <= s_i) if reverse else (t_i >= s_i), attn, 0.0)
    return jnp.dot(attn.astype(BF16), v, preferred_element_type=F32)


def _hgrn_kernel(q_f_ref, q_b_ref, v_f_ref, v_b_ref, bc_f_ref, bc_b_ref, kin_f_ref, kin_b_ref,
                 y_f_ref, y_b_ref, st_f_ref, st_b_ref):
    @pl.when(pl.program_id(2) == 0)
    def _():
        st_f_ref[...] = jnp.zeros_like(st_f_ref)
        st_b_ref[...] = jnp.zeros_like(st_b_ref)

    n_chunks = q_f_ref.shape[0] // HG_CHUNK
    directions = ((False, q_f_ref, v_f_ref, bc_f_ref, kin_f_ref, y_f_ref, st_f_ref),
                  (True, q_b_ref, v_b_ref, bc_b_ref, kin_b_ref, y_b_ref, st_b_ref))

    def body(u, carry):
        work = []
        for reverse, q_ref, v_ref, bc_ref, kin_ref, y_ref, st_ref in directions:
            for j in range(HG_CHUNKS_PER_TRIP):
                c = u * HG_CHUNKS_PER_TRIP + j
                cc = (n_chunks - 1 - c) if reverse else c
                rs = pl.ds(pl.multiple_of(cc * HG_CHUNK, HG_CHUNK), HG_CHUNK)
                v = v_ref[rs, :]
                stage1 = _hgrn_scores(q_ref[rs, :], kin_ref[rs, :], v, bc_ref[rs, :], reverse)
                work.append((reverse, rs, v, y_ref, st_ref, stage1))
        intra = [_hgrn_intra(w[5][0], w[5][1], w[2], w[0]) for w in work]
        states = {}
        for (reverse, rs, v, y_ref, st_ref, stage1), o_intra in zip(work, intra):
            _, _, q_state, decay, increment = stage1
            state_t = states.get(reverse)
            if state_t is None:
                state_t = st_ref[...]
            o_inter = lax.dot_general(q_state, state_t.astype(BF16), (((1,), (1,)), ((), ())),
                                      preferred_element_type=F32)
            states[reverse] = decay * state_t + increment
            y_ref[rs, :] = (o_intra + o_inter).astype(y_ref.dtype)
        for reverse, *_, st_ref in directions:
            st_ref[...] = states[reverse]
        return carry

    assert n_chunks % HG_CHUNKS_PER_TRIP == 0
    lax.fori_loop(0, n_chunks // HG_CHUNKS_PER_TRIP, body, 0)


def _hgrn(hq3, hv3, bc3, kin3, seg):
    B, S, W = hq3.shape
    H = W // HG_DIM
    n_seg = S // seg
    fwd = lambda off: pl.BlockSpec((None, seg, HG_DIM), lambda b, h, s: (b, s, off + h))
    bwd = lambda off: pl.BlockSpec((None, seg, HG_DIM), lambda b, h, s: (b, n_seg - 1 - s, off + h))
    return pl.pallas_call(
        _hgrn_kernel,
        grid=(B, H, n_seg),
        in_specs=[fwd(0), bwd(0), fwd(0), bwd(0), fwd(0), bwd(H), fwd(0), bwd(H)],
        out_specs=[fwd(0), bwd(0)],
        out_shape=[jax.ShapeDtypeStruct((B, S, W), BF16), jax.ShapeDtypeStruct((B, S, W), BF16)],
        scratch_shapes=[pltpu.VMEM((HG_DIM, HG_DIM), F32), pltpu.VMEM((HG_DIM, HG_DIM), F32)],
        compiler_params=pltpu.CompilerParams(
            dimension_semantics=("parallel", "parallel", "arbitrary"), vmem_limit_bytes=VMEM_LIMIT),
        name="hgrn2",
    )(hq3, hq3, hv3, hv3, bc3, bc3, kin3, kin3)


def _mixer_residual(x, oda, yf, yb, sg, g_hg, w_ref):
    half = oda.shape[1]
    y = yf.astype(F32) + yb.astype(F32)
    sg = sg.astype(F32)
    heads = []
    for hh in range(y.shape[1] // HG_DIM):
        cs = slice(hh * HG_DIM, (hh + 1) * HG_DIM)
        heads.append(_rms(y[:, cs], g_hg) * sg[:, cs])
    o_hg = jnp.concatenate(heads, axis=1).astype(BF16)
    return (x + jnp.dot(oda, w_ref[0:half, :], preferred_element_type=F32)
            + jnp.dot(o_hg, w_ref[half:, :], preferred_element_type=F32))


HALO_ROWS = 16


def _ffn_ple_kernel(fc, x_ref, oda_ref, yf_ref, yb_ref, sg_ref, xh_ref, odah_ref, yfh_ref, ybh_ref,
                    sgh_ref, p_ref, ghg_ref, wo_ref, gf_ref, wg_ref, wu_ref, cw_ref, cb_ref, wd_ref,
                    gp_ref, wpg_ref, wple_ref, gfin_ref, o_ref, h_ref, acc_ref):
    tm = x_ref.shape[0]
    ext = lambda tile_ref, halo_ref: jnp.concatenate([tile_ref[...], halo_ref[...]], axis=0)
    x1 = _mixer_residual(ext(x_ref, xh_ref), ext(oda_ref, odah_ref), ext(yf_ref, yfh_ref),
                         ext(yb_ref, ybh_ref), ext(sg_ref, sgh_ref), ghg_ref[...], wo_ref)
    xt = x1[0:tm]
    h_ref[...] = _rms(x1, gf_ref[...]).astype(BF16)
    assert sum(fc) == wg_ref.shape[1]
    edges = np.cumsum((0,) + tuple(fc))
    chunks = [slice(int(lo), int(hi)) for lo, hi in zip(edges[:-1], edges[1:])]

    def up_gate(cs):
        return (jnp.dot(h_ref[...], wg_ref[:, cs], preferred_element_type=F32),
                jnp.dot(h_ref[0:tm, :], wu_ref[:, cs], preferred_element_type=F32))

    nxt = up_gate(chunks[0])
    for c, cs in enumerate(chunks):
        a_ext, u = nxt
        if c + 1 < len(chunks):
            nxt = up_gate(chunks[c + 1])
        row = lax.broadcasted_iota(jnp.int32, u.shape, 0)
        a = a_ext[0:tm]
        a_prev = jnp.where(row == 0, a_ext[tm:tm + 1], pltpu.roll(a, 1, axis=0))
        a_next = jnp.where(row == tm - 1, a_ext[tm + 1:tm + 2], pltpu.roll(a, tm - 1, axis=0))
        cv = cb_ref[:, cs] + a_prev * cw_ref[0:1, cs] + a * cw_ref[1:2, cs] + a_next * cw_ref[2:3, cs]
        act = (_gelu(cv) * u).astype(BF16)
        d = jnp.dot(act, wd_ref[cs, :], preferred_element_type=F32)
        if c == 0:
            acc_ref[...] = d
        else:
            acc_ref[...] += d
    x2 = xt + acc_ref[...]
    gate = _sigmoid(jnp.dot(_rms(x2, gp_ref[...]).astype(BF16), wpg_ref[...], preferred_element_type=F32))
    ple = jnp.dot(p_ref[...].astype(BF16), wple_ref[...], preferred_element_type=F32)
    o_ref[...] = _rms(x2 + ple * gate, gfin_ref[...])


def _ffn_ple(streams, p2d, consts, tm, seq, fc):
    T, D = streams[0].shape
    row = lambda w: pl.BlockSpec((tm, w), lambda i: (i, 0))
    halo = lambda w: pl.BlockSpec((None, HALO_ROWS, w), lambda i: (i, 0, 0))
    halos = [_conv_halo(a, tm, seq) for a in streams]
    return pl.pallas_call(
        functools.partial(_ffn_ple_kernel, fc),
        grid=(T // tm,),
        in_specs=[row(a.shape[1]) for a in streams] + [halo(a.shape[1]) for a in streams]
                 + [row(p2d.shape[1])] + [_resident(c.shape) for c in consts],
        out_specs=row(D),
        out_shape=jax.ShapeDtypeStruct((T, D), F32),
        scratch_shapes=[pltpu.VMEM((tm + HALO_ROWS, D), BF16), pltpu.VMEM((tm, D), F32)],
        compiler_params=pltpu.CompilerParams(dimension_semantics=("parallel",),
                                             vmem_limit_bytes=VMEM_LIMIT),
        name="ffn_ple",
    )(*streams, *halos, p2d, *consts)


def _conv_halo(x1, tm, seq):
    T, D = x1.shape
    nt = T // tm
    xr = x1.reshape(nt, tm, D)
    zero = jnp.zeros((1, D), x1.dtype)
    prev = jnp.concatenate([zero, xr[:-1, tm - 1, :]], axis=0)
    nxt = jnp.concatenate([xr[1:, 0, :], zero], axis=0)
    start = (jnp.arange(nt) * tm) % seq
    prev = jnp.where((start == 0)[:, None], 0.0, prev)
    nxt = jnp.where((start + tm == seq)[:, None], 0.0, nxt)
    pad = jnp.zeros((nt, HALO_ROWS - 2, D), x1.dtype)
    return jnp.concatenate([prev[:, None, :], nxt[:, None, :], pad], axis=1)


def _rope_constants():
    half = ROT_DIM // 2
    inv_freq = (np.float32(ROPE_THETA) ** (-np.arange(half, dtype=np.float32) / np.float32(half))).astype(np.float32)
    d = np.arange(LANES) % DA_HEAD_DIM
    place = np.zeros((4 * half, 2 * LANES), np.float32)
    for f in range(half):
        hit = (d < ROT_DIM) & (d % half == f)
        sign = np.where(d < half, -1.0, 1.0)
        place[f, :LANES] = place[half + f, :LANES] = hit
        place[2 * half + f, LANES:] = place[3 * half + f, LANES:] = hit * sign
    return jnp.asarray(inv_freq.reshape(half, 1)), jnp.asarray(place, dtype=BF16)


def kernel(x, p, positions, norm_mix_g, w_in, lam_q1, lam_k1, lam_q2, lam_k2, da_subln_g, hg_lb_gamma, hg_norm_g, w_out, norm_ffn_g, w_ffn_gate, w_ffn_up, ffn_conv_w, ffn_conv_b, w_ffn_down, norm_ple_g, w_ple, w_ple_gate, final_norm_g):
    B, S, D = x.shape
    T = B * S
    depth = w_in.shape[0]
    tm = min(512, S)
    tq = min(512, S)
    tk = min(1024, S)
    seg = min(2048, S)
    fc = (768, 768, 768, 512)
    row2 = lambda v: v.reshape(1, -1)
    xc = x.reshape(T, D)
    pos = positions.reshape(T // tm, 1, tm)
    rope_freq, rope_place = _rope_constants()
    for i in range(depth):
        lambda_init = 0.8 - 0.6 * math.exp(-0.3 * i)
        gam = hg_lb_gamma.reshape(-1, hg_lb_gamma.shape[-1])
        qkv, hq, bc, kin, hv, sg = _in_proj(xc, pos, rope_freq, rope_place, row2(norm_mix_g[i]), w_in[i].astype(BF16),
                                            gam, i, tm)
        lam_p = jnp.stack([lam_q1[i], lam_k1[i], lam_q2[i], lam_k2[i]], axis=0)
        o_da = _diff_attn(qkv.reshape(B, S, -1), lam_p, da_subln_g[i].reshape(-1, 1), lambda_init, tq, tk)
        y_f, y_b = _hgrn(hq.reshape(B, S, -1), hv.reshape(B, S, -1), bc.reshape(B, S, -1),
                         kin.reshape(B, S, -1), seg)
        streams = (xc, o_da.reshape(T, -1), y_f.reshape(T, -1), y_b.reshape(T, -1), sg)
        consts = (row2(hg_norm_g[i]), w_out[i].astype(BF16), row2(norm_ffn_g[i]),
                  w_ffn_gate[i].astype(BF16), w_ffn_up[i].astype(BF16), ffn_conv_w[i],
                  row2(ffn_conv_b[i]), w_ffn_down[i].astype(BF16), row2(norm_ple_g[i]),
                  w_ple_gate[i].astype(BF16), w_ple[i].astype(BF16), row2(final_norm_g))
        xc = _ffn_ple(streams, p[i].reshape(T, -1), consts, tm, S, fc)
        assert i == depth - 1, "multi-layer stacks need the final norm split out of ffn_ple"
    return xc.reshape(B, S, D)
```

```python
import functools
import math

import numpy as np
import jax
import jax.numpy as jnp
from jax import lax
from jax.experimental import pallas as pl
from jax.experimental.pallas import tpu as pltpu

F32 = jnp.float32
BF16 = jnp.bfloat16

EPS = 1e-6
ROPE_THETA = 500000.0
DA_HEADS = 4
DA_HEAD_DIM = 64
DA_V_DIM = 128
ROT_DIM = 16
HG_HEADS = 4
HG_DIM = 128
CONV_WIDTH = 3
HG_CHUNK = 64
HG_SUB = 16
HG_CHUNKS_PER_TRIP = 4
LANES = 128
SUBLANES = 8
VMEM_LIMIT = 56 * 1024 * 1024


def _rms(xf, g):
    return xf * lax.rsqrt(jnp.mean(xf * xf, axis=-1, keepdims=True) + EPS) * g


def _sigmoid(z):
    return 0.5 + 0.5 * jnp.tanh(0.5 * z)


def _gelu(x):
    return 0.5 * x * (1.0 + lax.erf(x * (2.0 ** -0.5)))


def _resident(shape):
    return pl.BlockSpec(shape, lambda *_: (0,) * len(shape), pipeline_mode=pl.Buffered(1))


def _chunk_cumsum(x, reverse):
    n = x.shape[0]
    r = lax.broadcasted_iota(jnp.int32, x.shape, 0) & (HG_CHUNK - 1)
    s = 1
    while s < SUBLANES:
        if reverse:
            x = x + jnp.where(r < HG_CHUNK - s, pltpu.roll(x, n - s, axis=0), 0.0)
        else:
            x = x + jnp.where(r >= s, pltpu.roll(x, s, axis=0), 0.0)
        s *= 2
    while s < HG_CHUNK:
        parts = []
        for c0 in range(0, n, HG_CHUNK):
            lo, hi = x[c0:c0 + HG_CHUNK - s], x[c0 + s:c0 + HG_CHUNK]
            parts += [lo + hi, x[c0 + HG_CHUNK - s:c0 + HG_CHUNK]] if reverse else [x[c0:c0 + s], hi + lo]
        x = jnp.concatenate(parts, axis=0)
        s *= 2
    return x


def _in_proj_kernel(layer, n_slots, x_ref, pos_ref, freq_ref, place_ref, g_ref, w_ref, gam_ref,
                    qkv_ref, hq_ref, bc_ref, kin_ref, hv_ref, sg_ref):
    h = _rms(x_ref[...], g_ref[...]).astype(BF16)
    sec = qkv_ref.shape[1] // 3

    proj_all = jnp.dot(h, w_ref[...], preferred_element_type=F32)

    half = ROT_DIM // 2
    ang = freq_ref[...] * pos_ref[...].astype(F32)
    parts = []
    for trig in (jnp.cos(ang), jnp.sin(ang)):
        hi = trig.astype(BF16).astype(F32)
        parts += [hi, trig - hi]
    tab = lax.dot_general(jnp.concatenate(parts, axis=0).astype(BF16), place_ref[...],
                          (((0,), (0,)), ((), ())), preferred_element_type=F32)
    lane_d = lax.broadcasted_iota(jnp.int32, (1, LANES), 1) & (DA_HEAD_DIM - 1)
    cos = tab[:, :LANES] + jnp.where(lane_d >= ROT_DIM, 1.0, 0.0)
    sin = tab[:, LANES:]
    first_half = lane_d < half

    def rotary(t, scale):
        outs = []
        for hh in range(sec // LANES):
            blk = t[:, hh * LANES:(hh + 1) * LANES]
            partner = jnp.where(first_half, pltpu.roll(blk, LANES - half, axis=1),
                                pltpu.roll(blk, half, axis=1))
            rot = blk * cos + partner * sin
            outs.append(rot * scale if scale != 1.0 else rot)
        return jnp.concatenate(outs, axis=1)

    def put_q(t):
        qkv_ref[:, 0:sec] = rotary(t, DA_HEAD_DIM ** -0.5 * math.log2(math.e)).astype(BF16)

    def put_k(t):
        qkv_ref[:, sec:2 * sec] = rotary(t, 1.0).astype(BF16)

    def put_v(t):
        qkv_ref[:, 2 * sec:3 * sec] = t.astype(BF16)

    def put_hq(t):
        hq_ref[...] = t.astype(BF16)

    def put_forget(d, z):
        gam = gam_ref[d * n_slots:(d + 1) * n_slots, :]
        e = jnp.exp(gam - jnp.max(gam, axis=0, keepdims=True))
        lb = jnp.sum(e[0:layer + 1, :], axis=0, keepdims=True) / jnp.sum(e, axis=0, keepdims=True)
        half_span = 0.5 * (1.0 - lb)
        mid = 0.5 * (1.0 + lb)
        for r0 in range(0, z.shape[0], HG_CHUNK):
            w = half_span * jnp.tanh(0.5 * z[r0:r0 + HG_CHUNK])
            bc_ref[r0:r0 + HG_CHUNK, d * sec:(d + 1) * sec] = _chunk_cumsum(jnp.log2(mid + w),
                                                                             reverse=(d == 1))
            kin_ref[r0:r0 + HG_CHUNK, d * sec:(d + 1) * sec] = (half_span - w).astype(BF16)

    def put_hv(t):
        hv_ref[...] = t.astype(BF16)

    def put_gate(t):
        sg_ref[...] = (t * _sigmoid(t)).astype(BF16)

    epilogues = (put_q, put_k, put_v, put_hq, functools.partial(put_forget, 0),
                 functools.partial(put_forget, 1), put_hv, put_gate)
    for i, epilogue in enumerate(epilogues):
        epilogue(proj_all[:, i * sec:(i + 1) * sec])


def _in_proj(x2d, pos, freq, place, g, w_bf, gam, layer, tm):
    T, D = x2d.shape
    sec = w_bf.shape[1] // 8
    n_slots = gam.shape[0] // 2
    row = lambda w: pl.BlockSpec((tm, w), lambda i: (i, 0))
    return pl.pallas_call(
        functools.partial(_in_proj_kernel, layer, n_slots),
        grid=(T // tm,),
        in_specs=[row(D), pl.BlockSpec((None, 1, tm), lambda i: (i, 0, 0)),
                  _resident(freq.shape), _resident(place.shape), _resident(g.shape),
                  _resident(w_bf.shape), _resident(gam.shape)],
        out_specs=[row(3 * sec), row(sec), row(2 * sec), row(2 * sec), row(sec), row(sec)],
        out_shape=[jax.ShapeDtypeStruct((T, 3 * sec), BF16), jax.ShapeDtypeStruct((T, sec), BF16),
                   jax.ShapeDtypeStruct((T, 2 * sec), F32), jax.ShapeDtypeStruct((T, 2 * sec), BF16),
                   jax.ShapeDtypeStruct((T, sec), BF16), jax.ShapeDtypeStruct((T, sec), BF16)],
        compiler_params=pltpu.CompilerParams(dimension_semantics=("parallel",),
                                             vmem_limit_bytes=VMEM_LIMIT),
        name="in_proj",
    )(x2d, pos, freq, place, g, w_bf, gam)


DEN_ROWS = 16
ATTN_SEED_KEYS = 16
ATTN_MAX_JUMP = 100.0
ATTN_TILES_PER_TRIP = 2


def _diff_attn_kernel(lambda_init, tq, tk, q_ref, k_ref, v_ref, lam_ref, g_ref, o_ref,
                      vt_ref, qt_ref, acc_ref, s_ref):
    S = k_ref.shape[0]
    n_kv = S // tk
    dv = v_ref.shape[1]

    feat = lax.broadcasted_iota(jnp.int32, (q_ref.shape[1], tk), 0)

    def transpose_qv(i, carry):
        r = pl.ds(pl.multiple_of(i * tk, tk), tk)
        vt_ref[0:dv, r] = v_ref[r, :].astype(F32).T.astype(BF16)
        q_t = q_ref[r, :].astype(F32).T
        qt_ref[0, :, r] = jnp.where(feat < DA_HEAD_DIM, q_t, 0.0).astype(BF16)
        qt_ref[1, :, r] = jnp.where(feat >= DA_HEAD_DIM, q_t, 0.0).astype(BF16)
        return carry

    lax.fori_loop(0, n_kv, transpose_qv, 0)
    vt_ref[dv:dv + DEN_ROWS, :] = jnp.ones((DEN_ROWS, S), BF16)

    tile_cols = lambda qi: pl.ds(pl.multiple_of(qi * tq, tq), tq)

    def scores(mp, rows, q_cols):
        return jnp.dot(k_ref[rows, :], qt_ref[mp, :, q_cols], preferred_element_type=F32)

    def one_pass_tiles(tiles):
        chunk = lambda c: slice(c * tk, (c + 1) * tk)
        cols = [tile_cols(qi) for qi, _ in tiles]
        seeds = [[jnp.max(scores(mp, slice(0, ATTN_SEED_KEYS), qc), axis=0, keepdims=True)
                  for mp in range(2)] for qc in cols]
        steps = [(t, c) for t in range(len(tiles)) for c in range(n_kv)]
        jump = jnp.zeros_like(seeds[0][0])
        s_next = [scores(mp, chunk(0), cols[0]) for mp in range(2)]
        for n, (t, c) in enumerate(steps):
            qi, acc = tiles[t]
            if c == 0:
                m = list(seeds[t])
            for mp in range(2):
                s = s_next[mp]
                m_prev = m[mp]
                m[mp] = jnp.maximum(m_prev, jnp.max(s, axis=0, keepdims=True))
                p = jnp.exp2((s - m_prev).astype(BF16))
                pv = jnp.dot(vt_ref[:, chunk(c)], p, preferred_element_type=F32)
                if n + 1 < len(steps):
                    t_next, c_next = steps[n + 1]
                    s_next[mp] = scores(mp, chunk(c_next), cols[t_next])
                rescale = jnp.exp2(m_prev - m[mp])
                acc[mp] = rescale * (pv if c == 0 else acc[mp] + pv)
                jump = jnp.maximum(jump, m[mp] - m_prev)
            if c == n_kv - 1 and t + 1 < len(tiles):
                finalize(qi, acc)
        return jnp.max(jump)

    def two_pass_tile(qi, acc):
        q_cols = tile_cols(qi)

        def chunk_step(c, m):
            rows = pl.ds(pl.multiple_of(c * tk, tk), tk)
            m_out = []
            for mp in range(2):
                s_ref[...] = scores(mp, rows, q_cols)
                m_new = jnp.maximum(m[mp], jnp.max(s_ref[...], axis=0, keepdims=True))
                p = jnp.exp2((s_ref[...] - m_new).astype(BF16))
                acc[mp] = jnp.exp2(m[mp] - m_new) * acc[mp] + jnp.dot(
                    vt_ref[:, rows], p, preferred_element_type=F32)
                m_out.append(m_new)
            return tuple(m_out)

        for mp in range(2):
            acc[mp] = jnp.zeros(acc.shape[1:], F32)
        start = jnp.full((1, tq), -jnp.inf, F32)
        lax.fori_loop(0, n_kv, chunk_step, (start, start))

    def finalize(qi, acc):
        lam_p = lam_ref[...]
        lam = (jnp.exp(jnp.sum(lam_p[0:1, :] * lam_p[1:2, :], axis=-1, keepdims=True))
               - jnp.exp(jnp.sum(lam_p[2:3, :] * lam_p[3:4, :], axis=-1, keepdims=True)) + lambda_init)
        a0 = acc[0]
        a1 = acc[1]
        o = a0[0:dv] / a0[dv:dv + 1] - lam * (a1[0:dv] / a1[dv:dv + 1])
        o = o * lax.rsqrt(jnp.mean(o * o, axis=0, keepdims=True) + EPS) * g_ref[...]
        o_ref[tile_cols(qi), :] = (o * (1.0 - lambda_init)).T.astype(o_ref.dtype)

    n_slots = acc_ref.shape[0]
    assert (S // tq) % n_slots == 0

    def query_tiles(u, carry):
        tiles = [(u * n_slots + t, acc_ref.at[t]) for t in range(n_slots)]
        worst_jump = one_pass_tiles(tiles)

        @pl.when(jnp.logical_not(worst_jump <= ATTN_MAX_JUMP))
        def _():
            for t, (qi, acc) in enumerate(tiles):
                two_pass_tile(qi, acc)
                if t + 1 < n_slots:
                    finalize(qi, acc)

        finalize(*tiles[-1])
        return carry

    lax.fori_loop(0, S // tq // n_slots, query_tiles, 0)


def _diff_attn(qkv3, lam_p, g_col, lambda_init, tq, tk):
    B, S, _ = qkv3.shape
    H = DA_HEADS
    col = lambda off: pl.BlockSpec((None, S, LANES), lambda b, h: (b, 0, off + h))
    return pl.pallas_call(
        functools.partial(_diff_attn_kernel, lambda_init, tq, tk),
        grid=(B, H),
        in_specs=[col(0), col(H), col(2 * H), _resident(lam_p.shape), _resident(g_col.shape)],
        out_specs=col(0),
        out_shape=jax.ShapeDtypeStruct((B, S, H * DA_V_DIM), BF16),
        scratch_shapes=[pltpu.VMEM((DA_V_DIM + DEN_ROWS, S), BF16),
                        pltpu.VMEM((2, LANES, S), BF16),
                        pltpu.VMEM((ATTN_TILES_PER_TRIP, 2, DA_V_DIM + DEN_ROWS, tq), F32),
                        pltpu.VMEM((tk, tq), F32)],
        compiler_params=pltpu.CompilerParams(
            dimension_semantics=("parallel", "parallel"), vmem_limit_bytes=VMEM_LIMIT),
        name="diff_attn",
    )(qkv3, qkv3, qkv3, lam_p, g_col)


def _hgrn_scores(q, kin, v, b, reverse):
    C, n_sub = HG_CHUNK, HG_CHUNK // HG_SUB
    blk = lambda j: slice(j * HG_SUB, (j + 1) * HG_SUB)
    bound = [b[j * HG_SUB:j * HG_SUB + 1, :] if reverse else b[(j + 1) * HG_SUB - 1:(j + 1) * HG_SUB, :]
             for j in range(n_sub)]
    b_far = bound[0] if reverse else bound[-1]
    decay = lambda e: jnp.exp2(e).astype(BF16)
    k_til = jnp.concatenate([kin[blk(j)] * decay(bound[j] - b[blk(j)]) for j in range(n_sub)], axis=0)
    rows = [slice(0, (j + 1) * HG_SUB) if reverse else slice(j * HG_SUB, C) for j in range(n_sub)]
    q_til = jnp.concatenate([q[rows[j]] * decay(b[rows[j]] - bound[j]) for j in range(n_sub)], axis=0)
    offs = np.cumsum([0] + [r.stop - r.start for r in rows])
    scores = lax.dot_general(q_til, k_til, (((1,), (1,)), ((), ())),
                             preferred_element_type=F32)
    increment = lax.dot_general(v, kin * decay(b_far - b), (((0,), (0,)), ((), ())),
                                preferred_element_type=F32)
    return scores, [int(o) for o in offs], q * decay(b), jnp.exp2(b_far), increment


def _hgrn_intra(scores, offs, v, reverse):
    C, n_sub = HG_CHUNK, HG_CHUNK // HG_SUB
    col = lax.broadcasted_iota(jnp.int32, (HG_SUB, C), 1)
    row_blocks = []
    for i in range(n_sub):
        js = range(i, n_sub) if reverse else range(0, i + 1)
        a_i = jnp.zeros((HG_SUB, C), F32)
        for j in js:
            r0 = offs[j] + (i * HG_SUB if reverse else (i - j) * HG_SUB)
            a_i = jnp.where(col // HG_SUB == j, scores[r0:r0 + HG_SUB, :], a_i)
        row_blocks.append(a_i)
    attn = jnp.concatenate(row_blocks, axis=0)
    t_i = lax.broadcasted_iota(jnp.int32, (C, C), 0)
    s_i = lax.broadcasted_iota(jnp.int32, (C, C), 1)
    attn = jnp.where((t_i <= s_i) if reverse else (t_i >= s_i), attn, 0.0)
    return jnp.dot(attn.astype(BF16), v, preferred_element_type=F32)


def _hgrn_kernel(q_f_ref, q_b_ref, v_f_ref, v_b_ref, bc_f_ref, bc_b_ref, kin_f_ref, kin_b_ref,
                 y_f_ref, y_b_ref, st_f_ref, st_b_ref):
    @pl.when(pl.program_id(2) == 0)
    def _():
        st_f_ref[...] = jnp.zeros_like(st_f_ref)
        st_b_ref[...] = jnp.zeros_like(st_b_ref)

    n_chunks = q_f_ref.shape[0] // HG_CHUNK
    directions = ((False, q_f_ref, v_f_ref, bc_f_ref, kin_f_ref, y_f_ref, st_f_ref),
                  (True, q_b_ref, v_b_ref, bc_b_ref, kin_b_ref, y_b_ref, st_b_ref))

    def body(u, carry):
        work = []
        for reverse, q_ref, v_ref, bc_ref, kin_ref, y_ref, st_ref in directions:
            for j in range(HG_CHUNKS_PER_TRIP):
                c = u * HG_CHUNKS_PER_TRIP + j
                cc = (n_chunks - 1 - c) if reverse else c
                rs = pl.ds(pl.multiple_of(cc * HG_CHUNK, HG_CHUNK), HG_CHUNK)
                v = v_ref[rs, :]
                stage1 = _hgrn_scores(q_ref[rs, :], kin_ref[rs, :], v, bc_ref[rs, :], reverse)
                work.append((reverse, rs, v, y_ref, st_ref, stage1))
        intra = [_hgrn_intra(w[5][0], w[5][1], w[2], w[0]) for w in work]
        states = {}
        for (reverse, rs, v, y_ref, st_ref, stage1), o_intra in zip(work, intra):
            _, _, q_state, decay, increment = stage1
            state_t = states.get(reverse)
            if state_t is None:
                state_t = st_ref[...]
            o_inter = lax.dot_general(q_state, state_t.astype(BF16), (((1,), (1,)), ((), ())),
                                      preferred_element_type=F32)
            states[reverse] = decay * state_t + increment
            y_ref[rs, :] = (o_intra + o_inter).astype(y_ref.dtype)
        for reverse, *_, st_ref in directions:
            st_ref[...] = states[reverse]
        return carry

    assert n_chunks % HG_CHUNKS_PER_TRIP == 0
    lax.fori_loop(0, n_chunks // HG_CHUNKS_PER_TRIP, body, 0)


def _hgrn(hq3, hv3, bc3, kin3, seg):
    B, S, W = hq3.shape
    H = W // HG_DIM
    n_seg = S // seg
    fwd = lambda off: pl.BlockSpec((None, seg, HG_DIM), lambda b, h, s: (b, s, off + h))
    bwd = lambda off: pl.BlockSpec((None, seg, HG_DIM), lambda b, h, s: (b, n_seg - 1 - s, off + h))
    return pl.pallas_call(
        _hgrn_kernel,
        grid=(B, H, n_seg),
        in_specs=[fwd(0), bwd(0), fwd(0), bwd(0), fwd(0), bwd(H), fwd(0), bwd(H)],
        out_specs=[fwd(0), bwd(0)],
        out_shape=[jax.ShapeDtypeStruct((B, S, W), BF16), jax.ShapeDtypeStruct((B, S, W), BF16)],
        scratch_shapes=[pltpu.VMEM((HG_DIM, HG_DIM), F32), pltpu.VMEM((HG_DIM, HG_DIM), F32)],
        compiler_params=pltpu.CompilerParams(
            dimension_semantics=("parallel", "parallel", "arbitrary"), vmem_limit_bytes=VMEM_LIMIT),
        name="hgrn2",
    )(hq3, hq3, hv3, hv3, bc3, bc3, kin3, kin3)


def _mixer_residual(x, oda, yf, yb, sg, g_hg, w_ref):
    half = oda.shape[1]
    y = yf.astype(F32) + yb.astype(F32)
    sg = sg.astype(F32)
    heads = []
    for hh in range(y.shape[1] // HG_DIM):
        cs = slice(hh * HG_DIM, (hh + 1) * HG_DIM)
        heads.append(_rms(y[:, cs], g_hg) * sg[:, cs])
    o_hg = jnp.concatenate(heads, axis=1).astype(BF16)
    return (x + jnp.dot(oda, w_ref[0:half, :], preferred_element_type=F32)
            + jnp.dot(o_hg, w_ref[half:, :], preferred_element_type=F32))


HALO_ROWS = 16


def _ffn_ple_kernel(fc, x_ref, oda_ref, yf_ref, yb_ref, sg_ref, xh_ref, odah_ref, yfh_ref, ybh_ref,
                    sgh_ref, p_ref, ghg_ref, wo_ref, gf_ref, wg_ref, wu_ref, cw_ref, cb_ref, wd_ref,
                    gp_ref, wpg_ref, wple_ref, gfin_ref, o_ref, h_ref, acc_ref):
    tm = x_ref.shape[0]
    ext = lambda tile_ref, halo_ref: jnp.concatenate([tile_ref[...], halo_ref[...]], axis=0)
    x1 = _mixer_residual(ext(x_ref, xh_ref), ext(oda_ref, odah_ref), ext(yf_ref, yfh_ref),
                         ext(yb_ref, ybh_ref), ext(sg_ref, sgh_ref), ghg_ref[...], wo_ref)
    xt = x1[0:tm]
    h_ref[...] = _rms(x1, gf_ref[...]).astype(BF16)
    assert sum(fc) == wg_ref.shape[1]
    edges = np.cumsum((0,) + tuple(fc))
    chunks = [slice(int(lo), int(hi)) for lo, hi in zip(edges[:-1], edges[1:])]

    def up_gate(cs):
        return (jnp.dot(h_ref[...], wg_ref[:, cs], preferred_element_type=F32),
                jnp.dot(h_ref[0:tm, :], wu_ref[:, cs], preferred_element_type=F32))

    nxt = up_gate(chunks[0])
    for c, cs in enumerate(chunks):
        a_ext, u = nxt
        if c + 1 < len(chunks):
            nxt = up_gate(chunks[c + 1])
        row = lax.broadcasted_iota(jnp.int32, u.shape, 0)
        a = a_ext[0:tm]
        a_prev = jnp.where(row == 0, a_ext[tm:tm + 1], pltpu.roll(a, 1, axis=0))
        a_next = jnp.where(row == tm - 1, a_ext[tm + 1:tm + 2], pltpu.roll(a, tm - 1, axis=0))
        cv = cb_ref[:, cs] + a_prev * cw_ref[0:1, cs] + a * cw_ref[1:2, cs] + a_next * cw_ref[2:3, cs]
        act = (_gelu(cv) * u).astype(BF16)
        d = jnp.dot(act, wd_ref[cs, :], preferred_element_type=F32)
        if c == 0:
            acc_ref[...] = d
        else:
            acc_ref[...] += d
    x2 = xt + acc_ref[...]
    gate = _sigmoid(jnp.dot(_rms(x2, gp_ref[...]).astype(BF16), wpg_ref[...], preferred_element_type=F32))
    ple = jnp.dot(p_ref[...].astype(BF16), wple_ref[...], preferred_element_type=F32)
    o_ref[...] = _rms(x2 + ple * gate, gfin_ref[...])


def _ffn_ple(streams, p2d, consts, tm, seq, fc):
    T, D = streams[0].shape
    row = lambda w: pl.BlockSpec((tm, w), lambda i: (i, 0))
    halo = lambda w: pl.BlockSpec((None, HALO_ROWS, w), lambda i: (i, 0, 0))
    halos = [_conv_halo(a, tm, seq) for a in streams]
    return pl.pallas_call(
        functools.partial(_ffn_ple_kernel, fc),
        grid=(T // tm,),
        in_specs=[row(a.shape[1]) for a in streams] + [halo(a.shape[1]) for a in streams]
                 + [row(p2d.shape[1])] + [_resident(c.shape) for c in consts],
        out_specs=row(D),
        out_shape=jax.ShapeDtypeStruct((T, D), F32),
        scratch_shapes=[pltpu.VMEM((tm + HALO_ROWS, D), BF16), pltpu.VMEM((tm, D), F32)],
        compiler_params=pltpu.CompilerParams(dimension_semantics=("parallel",),
                                             vmem_limit_bytes=VMEM_LIMIT),
        name="ffn_ple",
    )(*streams, *halos, p2d, *consts)


def _conv_halo(x1, tm, seq):
    T, D = x1.shape
    nt = T // tm
    xr = x1.reshape(nt, tm, D)
    zero = jnp.zeros((1, D), x1.dtype)
    prev = jnp.concatenate([zero, xr[:-1, tm - 1, :]], axis=0)
    nxt = jnp.concatenate([xr[1:, 0, :], zero], axis=0)
    start = (jnp.arange(nt) * tm) % seq
    prev = jnp.where((start == 0)[:, None], 0.0, prev)
    nxt = jnp.where((start + tm == seq)[:, None], 0.0, nxt)
    pad = jnp.zeros((nt, HALO_ROWS - 2, D), x1.dtype)
    return jnp.concatenate([prev[:, None, :], nxt[:, None, :], pad], axis=1)


def _rope_constants():
    half = ROT_DIM // 2
    inv_freq = (np.float32(ROPE_THETA) ** (-np.arange(half, dtype=np.float32) / np.float32(half))).astype(np.float32)
    d = np.arange(LANES) % DA_HEAD_DIM
    place = np.zeros((4 * half, 2 * LANES), np.float32)
    for f in range(half):
        hit = (d < ROT_DIM) & (d % half == f)
        sign = np.where(d < half, -1.0, 1.0)
        place[f, :LANES] = place[half + f, :LANES] = hit
        place[2 * half + f, LANES:] = place[3 * half + f, LANES:] = hit * sign
    return jnp.asarray(inv_freq.reshape(half, 1)), jnp.asarray(place, dtype=BF16)


def kernel(x, p, positions, norm_mix_g, w_in, lam_q1, lam_k1, lam_q2, lam_k2, da_subln_g, hg_lb_gamma, hg_norm_g, w_out, norm_ffn_g, w_ffn_gate, w_ffn_up, ffn_conv_w, ffn_conv_b, w_ffn_down, norm_ple_g, w_ple, w_ple_gate, final_norm_g):
    B, S, D = x.shape
    T = B * S
    depth = w_in.shape[0]
    tm = min(512, S)
    tq = min(512, S)
    tk = min(1024, S)
    seg = min(2048, S)
    fc = (768, 768, 768, 512)
    row2 = lambda v: v.reshape(1, -1)
    xc = x.reshape(T, D)
    pos = positions.reshape(T // tm, 1, tm)
    rope_freq, rope_place = _rope_constants()
    for i in range(depth):
        lambda_init = 0.8 - 0.6 * math.exp(-0.3 * i)
        gam = hg_lb_gamma.reshape(-1, hg_lb_gamma.shape[-1])
        qkv, hq, bc, kin, hv, sg = _in_proj(xc, pos, rope_freq, rope_place, row2(norm_mix_g[i]), w_in[i].astype(BF16),
                                            gam, i, tm)
        lam_p = jnp.stack([lam_q1[i], lam_k1[i], lam_q2[i], lam_k2[i]], axis=0)
        o_da = _diff_attn(qkv.reshape(B, S, -1), lam_p, da_subln_g[i].reshape(-1, 1), lambda_init, tq, tk)
        y_f, y_b = _hgrn(hq.reshape(B, S, -1), hv.reshape(B, S, -1), bc.reshape(B, S, -1),
                         kin.reshape(B, S, -1), seg)
        streams = (xc, o_da.reshape(T, -1), y_f.reshape(T, -1), y_b.reshape(T, -1), sg)
        consts = (row2(hg_norm_g[i]), w_out[i].astype(BF16), row2(norm_ffn_g[i]),
                  w_ffn_gate[i].astype(BF16), w_ffn_up[i].astype(BF16), ffn_conv_w[i],
                  row2(ffn_conv_b[i]), w_ffn_down[i].astype(BF16), row2(norm_ple_g[i]),
                  w_ple_gate[i].astype(BF16), w_ple[i].astype(BF16), row2(final_norm_g))
        xc = _ffn_ple(streams, p[i].reshape(T, -1), consts, tm, S, fc)
        assert i == depth - 1, "multi-layer stacks need the final norm split out of ffn_ple"
    return xc.reshape(B, S, D)
```

```python
import functools
import math

import numpy as np
import jax
import jax.numpy as jnp
from jax import lax
from jax.experimental import pallas as pl
from jax.experimental.pallas import tpu as pltpu

F32 = jnp.float32
BF16 = jnp.bfloat16

EPS = 1e-6
ROPE_THETA = 500000.0
DA_HEADS = 4
DA_HEAD_DIM = 64
DA_V_DIM = 128
ROT_DIM = 16
HG_HEADS = 4
HG_DIM = 128
CONV_WIDTH = 3
HG_CHUNK = 64
HG_SUB = 16
HG_CHUNKS_PER_TRIP = 4
LANES = 128
SUBLANES = 8
VMEM_LIMIT = 56 * 1024 * 1024


def _rms(xf, g):
    return xf * lax.rsqrt(jnp.mean(xf * xf, axis=-1, keepdims=True) + EPS) * g


def _sigmoid(z):
    return 0.5 + 0.5 * jnp.tanh(0.5 * z)


def _gelu(x):
    return 0.5 * x * (1.0 + lax.erf(x * (2.0 ** -0.5)))


def _resident(shape):
    return pl.BlockSpec(shape, lambda *_: (0,) * len(shape), pipeline_mode=pl.Buffered(1))


def _chunk_cumsum(x, reverse):
    n = x.shape[0]
    r = lax.broadcasted_iota(jnp.int32, x.shape, 0) & (HG_CHUNK - 1)
    s = 1
    while s < SUBLANES:
        if reverse:
            x = x + jnp.where(r < HG_CHUNK - s, pltpu.roll(x, n - s, axis=0), 0.0)
        else:
            x = x + jnp.where(r >= s, pltpu.roll(x, s, axis=0), 0.0)
        s *= 2
    while s < HG_CHUNK:
        parts = []
        for c0 in range(0, n, HG_CHUNK):
            lo, hi = x[c0:c0 + HG_CHUNK - s], x[c0 + s:c0 + HG_CHUNK]
            parts += [lo + hi, x[c0 + HG_CHUNK - s:c0 + HG_CHUNK]] if reverse else [x[c0:c0 + s], hi + lo]
        x = jnp.concatenate(parts, axis=0)
        s *= 2
    return x


def _in_proj_kernel(layer, n_slots, x_ref, pos_ref, freq_ref, place_ref, g_ref, w_ref, gam_ref,
                    qkv_ref, hq_ref, bc_ref, kin_ref, hv_ref, sg_ref):
    h = _rms(x_ref[...], g_ref[...]).astype(BF16)
    sec = qkv_ref.shape[1] // 3

    proj_all = jnp.dot(h, w_ref[...], preferred_element_type=F32)

    half = ROT_DIM // 2
    ang = freq_ref[...] * pos_ref[...].astype(F32)
    parts = []
    for trig in (jnp.cos(ang), jnp.sin(ang)):
        hi = trig.astype(BF16).astype(F32)
        parts += [hi, trig - hi]
    tab = lax.dot_general(jnp.concatenate(parts, axis=0).astype(BF16), place_ref[...],
                          (((0,), (0,)), ((), ())), preferred_element_type=F32)
    lane_d = lax.broadcasted_iota(jnp.int32, (1, LANES), 1) & (DA_HEAD_DIM - 1)
    cos = tab[:, :LANES] + jnp.where(lane_d >= ROT_DIM, 1.0, 0.0)
    sin = tab[:, LANES:]
    first_half = lane_d < half

    def rotary(t, scale):
        outs = []
        for hh in range(sec // LANES):
            blk = t[:, hh * LANES:(hh + 1) * LANES]
            partner = jnp.where(first_half, pltpu.roll(blk, LANES - half, axis=1),
                                pltpu.roll(blk, half, axis=1))
            rot = blk * cos + partner * sin
            outs.append(rot * scale if scale != 1.0 else rot)
        return jnp.concatenate(outs, axis=1)

    def put_q(t):
        qkv_ref[:, 0:sec] = rotary(t, DA_HEAD_DIM ** -0.5 * math.log2(math.e)).astype(BF16)

    def put_k(t):
        qkv_ref[:, sec:2 * sec] = rotary(t, 1.0).astype(BF16)

    def put_v(t):
        qkv_ref[:, 2 * sec:3 * sec] = t.astype(BF16)

    def put_hq(t):
        hq_ref[...] = t.astype(BF16)

    def put_forget(d, z):
        gam = gam_ref[d * n_slots:(d + 1) * n_slots, :]
        e = jnp.exp(gam - jnp.max(gam, axis=0, keepdims=True))
        lb = jnp.sum(e[0:layer + 1, :], axis=0, keepdims=True) / jnp.sum(e, axis=0, keepdims=True)
        half_span = 0.5 * (1.0 - lb)
        mid = 0.5 * (1.0 + lb)
        for r0 in range(0, z.shape[0], HG_CHUNK):
            w = half_span * jnp.tanh(0.5 * z[r0:r0 + HG_CHUNK])
            bc_ref[r0:r0 + HG_CHUNK, d * sec:(d + 1) * sec] = _chunk_cumsum(jnp.log2(mid + w),
                                                                             reverse=(d == 1))
            kin_ref[r0:r0 + HG_CHUNK, d * sec:(d + 1) * sec] = (half_span - w).astype(BF16)

    def put_hv(t):
        hv_ref[...] = t.astype(BF16)

    def put_gate(t):
        sg_ref[...] = (t * _sigmoid(t)).astype(BF16)

    epilogues = (put_q, put_k, put_v, put_hq, functools.partial(put_forget, 0),
                 functools.partial(put_forget, 1), put_hv, put_gate)
    for i, epilogue in enumerate(epilogues):
        epilogue(proj_all[:, i * sec:(i + 1) * sec])


def _in_proj(x2d, pos, freq, place, g, w_bf, gam, layer, tm):
    T, D = x2d.shape
    sec = w_bf.shape[1] // 8
    n_slots = gam.shape[0] // 2
    row = lambda w: pl.BlockSpec((tm, w), lambda i: (i, 0))
    return pl.pallas_call(
        functools.partial(_in_proj_kernel, layer, n_slots),
        grid=(T // tm,),
        in_specs=[row(D), pl.BlockSpec((None, 1, tm), lambda i: (i, 0, 0)),
                  _resident(freq.shape), _resident(place.shape), _resident(g.shape),
                  _resident(w_bf.shape), _resident(gam.shape)],
        out_specs=[row(3 * sec), row(sec), row(2 * sec), row(2 * sec), row(sec), row(sec)],
        out_shape=[jax.ShapeDtypeStruct((T, 3 * sec), BF16), jax.ShapeDtypeStruct((T, sec), BF16),
                   jax.ShapeDtypeStruct((T, 2 * sec), F32), jax.ShapeDtypeStruct((T, 2 * sec), BF16),
                   jax.ShapeDtypeStruct((T, sec), BF16), jax.ShapeDtypeStruct((T, sec), BF16)],
        compiler_params=pltpu.CompilerParams(dimension_semantics=("parallel",),
                                             vmem_limit_bytes=VMEM_LIMIT),
        name="in_proj",
    )(x2d, pos, freq, place, g, w_bf, gam)


DEN_ROWS = 16
ATTN_SEED_KEYS = 16
ATTN_MAX_JUMP = 100.0
ATTN_TILES_PER_TRIP = 2


def _diff_attn_kernel(lambda_init, tq, tk, q_ref, k_ref, v_ref, lam_ref, g_ref, o_ref,
                      vt_ref, qt_ref, acc_ref, s_ref):
    S = k_ref.shape[0]
    n_kv = S // tk
    dv = v_ref.shape[1]

    feat = lax.broadcasted_iota(jnp.int32, (q_ref.shape[1], tk), 0)

    def transpose_qv(i, carry):
        r = pl.ds(pl.multiple_of(i * tk, tk), tk)
        vt_ref[0:dv, r] = v_ref[r, :].astype(F32).T.astype(BF16)
        q_t = q_ref[r, :].astype(F32).T
        qt_ref[0, :, r] = jnp.where(feat < DA_HEAD_DIM, q_t, 0.0).astype(BF16)
        qt_ref[1, :, r] = jnp.where(feat >= DA_HEAD_DIM, q_t, 0.0).astype(BF16)
        return carry

    lax.fori_loop(0, n_kv, transpose_qv, 0)
    vt_ref[dv:dv + DEN_ROWS, :] = jnp.ones((DEN_ROWS, S), BF16)

    tile_cols = lambda qi: pl.ds(pl.multiple_of(qi * tq, tq), tq)

    def scores(mp, rows, q_cols):
        return jnp.dot(k_ref[rows, :], qt_ref[mp, :, q_cols], preferred_element_type=F32)

    def one_pass_tiles(tiles):
        chunk = lambda c: slice(c * tk, (c + 1) * tk)
        cols = [tile_cols(qi) for qi, _ in tiles]
        seeds = [[jnp.max(scores(mp, slice(0, ATTN_SEED_KEYS), qc), axis=0, keepdims=True)
                  for mp in range(2)] for qc in cols]
        steps = [(t, c) for t in range(len(tiles)) for c in range(n_kv)]
        jump = jnp.zeros_like(seeds[0][0])
        s_next = [scores(mp, chunk(0), cols[0]) for mp in range(2)]
        for n, (t, c) in enumerate(steps):
            qi, acc = tiles[t]
            if c == 0:
                m = list(seeds[t])
            for mp in range(2):
                s = s_next[mp]
                m_prev = m[mp]
                m[mp] = jnp.maximum(m_prev, jnp.max(s, axis=0, keepdims=True))
                p = jnp.exp2((s - m_prev).astype(BF16))
                pv = jnp.dot(vt_ref[:, chunk(c)], p, preferred_element_type=F32)
                if n + 1 < len(steps):
                    t_next, c_next = steps[n + 1]
                    s_next[mp] = scores(mp, chunk(c_next), cols[t_next])
                rescale = jnp.exp2(m_prev - m[mp])
                acc[mp] = rescale * (pv if c == 0 else acc[mp] + pv)
                jump = jnp.maximum(jump, m[mp] - m_prev)
            if c == n_kv - 1 and t + 1 < len(tiles):
                finalize(qi, acc)
        return jnp.max(jump)

    def two_pass_tile(qi, acc):
        q_cols = tile_cols(qi)

        def chunk_step(c, m):
            rows = pl.ds(pl.multiple_of(c * tk, tk), tk)
            m_out = []
            for mp in range(2):
                s_ref[...] = scores(mp, rows, q_cols)
                m_new = jnp.maximum(m[mp], jnp.max(s_ref[...], axis=0, keepdims=True))
                p = jnp.exp2((s_ref[...] - m_new).astype(BF16))
                acc[mp] = jnp.exp2(m[mp] - m_new) * acc[mp] + jnp.dot(
                    vt_ref[:, rows], p, preferred_element_type=F32)
                m_out.append(m_new)
            return tuple(m_out)

        for mp in range(2):
            acc[mp] = jnp.zeros(acc.shape[1:], F32)
        start = jnp.full((1, tq), -jnp.inf, F32)
        lax.fori_loop(0, n_kv, chunk_step, (start, start))

    def finalize(qi, acc):
        lam_p = lam_ref[...]
        lam = (jnp.exp(jnp.sum(lam_p[0:1, :] * lam_p[1:2, :], axis=-1, keepdims=True))
               - jnp.exp(jnp.sum(lam_p[2:3, :] * lam_p[3:4, :], axis=-1, keepdims=True)) + lambda_init)
        a0 = acc[0]
        a1 = acc[1]
        o = a0[0:dv] / a0[dv:dv + 1] - lam * (a1[0:dv] / a1[dv:dv + 1])
        o = o * lax.rsqrt(jnp.mean(o * o, axis=0, keepdims=True) + EPS) * g_ref[...]
        o_ref[tile_cols(qi), :] = (o * (1.0 - lambda_init)).T.astype(o_ref.dtype)

    n_slots = acc_ref.shape[0]
    assert (S // tq) % n_slots == 0

    def query_tiles(u, carry):
        tiles = [(u * n_slots + t, acc_ref.at[t]) for t in range(n_slots)]
        worst_jump = one_pass_tiles(tiles)

        @pl.when(jnp.logical_not(worst_jump <= ATTN_MAX_JUMP))
        def _():
            for t, (qi, acc) in enumerate(tiles):
                two_pass_tile(qi, acc)
                if t + 1 < n_slots:
                    finalize(qi, acc)

        finalize(*tiles[-1])
        return carry

    lax.fori_loop(0, S // tq // n_slots, query_tiles, 0)


def _diff_attn(qkv3, lam_p, g_col, lambda_init, tq, tk):
    B, S, _ = qkv3.shape
    H = DA_HEADS
    col = lambda off: pl.BlockSpec((None, S, LANES), lambda b, h: (b, 0, off + h))
    return pl.pallas_call(
        functools.partial(_diff_attn_kernel, lambda_init, tq, tk),
        grid=(B, H),
        in_specs=[col(0), col(H), col(2 * H), _resident(lam_p.shape), _resident(g_col.shape)],
        out_specs=col(0),
        out_shape=jax.ShapeDtypeStruct((B, S, H * DA_V_DIM), BF16),
        scratch_shapes=[pltpu.VMEM((DA_V_DIM + DEN_ROWS, S), BF16),
                        pltpu.VMEM((2, LANES, S), BF16),
                        pltpu.VMEM((ATTN_TILES_PER_TRIP, 2, DA_V_DIM + DEN_ROWS, tq), F32),
                        pltpu.VMEM((tk, tq), F32)],
        compiler_params=pltpu.CompilerParams(
            dimension_semantics=("parallel", "parallel"), vmem_limit_bytes=VMEM_LIMIT),
        name="diff_attn",
    )(qkv3, qkv3, qkv3, lam_p, g_col)


def _hgrn_scores(q, kin, v, b, reverse):
    C, n_sub = HG_CHUNK, HG_CHUNK // HG_SUB
    blk = lambda j: slice(j * HG_SUB, (j + 1) * HG_SUB)
    bound = [b[j * HG_SUB:j * HG_SUB + 1, :] if reverse else b[(j + 1) * HG_SUB - 1:(j + 1) * HG_SUB, :]
             for j in range(n_sub)]
    b_far = bound[0] if reverse else bound[-1]
    decay = lambda e: jnp.exp2(e).astype(BF16)
    k_til = jnp.concatenate([kin[blk(j)] * decay(bound[j] - b[blk(j)]) for j in range(n_sub)], axis=0)
    rows = [slice(0, (j + 1) * HG_SUB) if reverse else slice(j * HG_SUB, C) for j in range(n_sub)]
    q_til = jnp.concatenate([q[rows[j]] * decay(b[rows[j]] - bound[j]) for j in range(n_sub)], axis=0)
    offs = np.cumsum([0] + [r.stop - r.start for r in rows])
    scores = lax.dot_general(q_til, k_til, (((1,), (1,)), ((), ())),
                             preferred_element_type=F32)
    increment = lax.dot_general(v, kin * decay(b_far - b), (((0,), (0,)), ((), ())),
                                preferred_element_type=F32)
    return scores, [int(o) for o in offs], q * decay(b), jnp.exp2(b_far), increment


def _hgrn_intra(scores, offs, v, reverse):
    C, n_sub = HG_CHUNK, HG_CHUNK // HG_SUB
    col = lax.broadcasted_iota(jnp.int32, (HG_SUB, C), 1)
    row_blocks = []
    for i in range(n_sub):
        js = range(i, n_sub) if reverse else range(0, i + 1)
        a_i = jnp.zeros((HG_SUB, C), F32)
        for j in js:
            r0 = offs[j] + (i * HG_SUB if reverse else (i - j) * HG_SUB)
            a_i = jnp.where(col // HG_SUB == j, scores[r0:r0 + HG_SUB, :], a_i)
        row_blocks.append(a_i)
    attn = jnp.concatenate(row_blocks, axis=0)
    t_i = lax.broadcasted_iota(jnp.int32, (C, C), 0)
    s_i = lax.broadcasted_iota(jnp.int32, (C, C), 1)
    attn = jnp.where((t_i <= s_i) if reverse else (t_i >= s_i), attn, 0.0)
    return jnp.dot(attn.astype(BF16), v, preferred_element_type=F32)


def _hgrn_kernel(q_f_ref, q_b_ref, v_f_ref, v_b_ref, bc_f_ref, bc_b_ref, kin_f_ref, kin_b_ref,
                 y_f_ref, y_b_ref, st_f_ref, st_b_ref):
    @pl.when(pl.program_id(2) == 0)
    def _():
        st_f_ref[...] = jnp.zeros_like(st_f_ref)
        st_b_ref[...] = jnp.zeros_like(st_b_ref)

    n_chunks = q_f_ref.shape[0] // HG_CHUNK
    directions = ((False, q_f_ref, v_f_ref, bc_f_ref, kin_f_ref, y_f_ref, st_f_ref),
                  (True, q_b_ref, v_b_ref, bc_b_ref, kin_b_ref, y_b_ref, st_b_ref))

    def body(u, carry):
        work = []
        for reverse, q_ref, v_ref, bc_ref, kin_ref, y_ref, st_ref in directions:
            for j in range(HG_CHUNKS_PER_TRIP):
                c = u * HG_CHUNKS_PER_TRIP + j
                cc = (n_chunks - 1 - c) if reverse else c
                rs = pl.ds(pl.multiple_of(cc * HG_CHUNK, HG_CHUNK), HG_CHUNK)
                v = v_ref[rs, :]
                stage1 = _hgrn_scores(q_ref[rs, :], kin_ref[rs, :], v, bc_ref[rs, :], reverse)
                work.append((reverse, rs, v, y_ref, st_ref, stage1))
        intra = [_hgrn_intra(w[5][0], w[5][1], w[2], w[0]) for w in work]
        states = {}
        for (reverse, rs, v, y_ref, st_ref, stage1), o_intra in zip(work, intra):
            _, _, q_state, decay, increment = stage1
            state_t = states.get(reverse)
            if state_t is None:
                state_t = st_ref[...]
            o_inter = lax.dot_general(q_state, state_t.astype(BF16), (((1,), (1,)), ((), ())),
                                      preferred_element_type=F32)
            states[reverse] = decay * state_t + increment
            y_ref[rs, :] = (o_intra + o_inter).astype(y_ref.dtype)
        for reverse, *_, st_ref in directions:
            st_ref[...] = states[reverse]
        return carry

    assert n_chunks % HG_CHUNKS_PER_TRIP == 0
    lax.fori_loop(0, n_chunks // HG_CHUNKS_PER_TRIP, body, 0)


def _hgrn(hq3, hv3, bc3, kin3, seg):
    B, S, W = hq3.shape
    H = W // HG_DIM
    n_seg = S // seg
    fwd = lambda off: pl.BlockSpec((None, seg, HG_DIM), lambda b, h, s: (b, s, off + h))
    bwd = lambda off: pl.BlockSpec((None, seg, HG_DIM), lambda b, h, s: (b, n_seg - 1 - s, off + h))
    return pl.pallas_call(
        _hgrn_kernel,
        grid=(B, H, n_seg),
        in_specs=[fwd(0), bwd(0), fwd(0), bwd(0), fwd(0), bwd(H), fwd(0), bwd(H)],
        out_specs=[fwd(0), bwd(0)],
        out_shape=[jax.ShapeDtypeStruct((B, S, W), BF16), jax.ShapeDtypeStruct((B, S, W), BF16)],
        scratch_shapes=[pltpu.VMEM((HG_DIM, HG_DIM), F32), pltpu.VMEM((HG_DIM, HG_DIM), F32)],
        compiler_params=pltpu.CompilerParams(
            dimension_semantics=("parallel", "parallel", "arbitrary"), vmem_limit_bytes=VMEM_LIMIT),
        name="hgrn2",
    )(hq3, hq3, hv3, hv3, bc3, bc3, kin3, kin3)


def _mixer_residual(x, oda, yf, yb, sg, g_hg, w_ref):
    half = oda.shape[1]
    y = yf.astype(F32) + yb.astype(F32)
    sg = sg.astype(F32)
    heads = []
    for hh in range(y.shape[1] // HG_DIM):
        cs = slice(hh * HG_DIM, (hh + 1) * HG_DIM)
        heads.append(_rms(y[:, cs], g_hg) * sg[:, cs])
    o_hg = jnp.concatenate(heads, axis=1).astype(BF16)
    return (x + jnp.dot(oda, w_ref[0:half, :], preferred_element_type=F32)
            + jnp.dot(o_hg, w_ref[half:, :], preferred_element_type=F32))


HALO_ROWS = 16


def _ffn_ple_kernel(fc, x_ref, oda_ref, yf_ref, yb_ref, sg_ref, xh_ref, odah_ref, yfh_ref, ybh_ref,
                    sgh_ref, p_ref, ghg_ref, wo_ref, gf_ref, wg_ref, wu_ref, cw_ref, cb_ref, wd_ref,
                    gp_ref, wpg_ref, wple_ref, gfin_ref, o_ref, h_ref, acc_ref):
    tm = x_ref.shape[0]
    ext = lambda tile_ref, halo_ref: jnp.concatenate([tile_ref[...], halo_ref[...]], axis=0)
    x1 = _mixer_residual(ext(x_ref, xh_ref), ext(oda_ref, odah_ref), ext(yf_ref, yfh_ref),
                         ext(yb_ref, ybh_ref), ext(sg_ref, sgh_ref), ghg_ref[...], wo_ref)
    xt = x1[0:tm]
    h_ref[...] = _rms(x1, gf_ref[...]).astype(BF16)
    assert sum(fc) == wg_ref.shape[1]
    edges = np.cumsum((0,) + tuple(fc))
    chunks = [slice(int(lo), int(hi)) for lo, hi in zip(edges[:-1], edges[1:])]

    def up_gate(cs):
        return (jnp.dot(h_ref[...], wg_ref[:, cs], preferred_element_type=F32),
                jnp.dot(h_ref[0:tm, :], wu_ref[:, cs], preferred_element_type=F32))

    nxt = up_gate(chunks[0])
    for c, cs in enumerate(chunks):
        a_ext, u = nxt
        if c + 1 < len(chunks):
            nxt = up_gate(chunks[c + 1])
        row = lax.broadcasted_iota(jnp.int32, u.shape, 0)
        a = a_ext[0:tm]
        a_prev = jnp.where(row == 0, a_ext[tm:tm + 1], pltpu.roll(a, 1, axis=0))
        a_next = jnp.where(row == tm - 1, a_ext[tm + 1:tm + 2], pltpu.roll(a, tm - 1, axis=0))
        cv = cb_ref[:, cs] + a_prev * cw_ref[0:1, cs] + a * cw_ref[1:2, cs] + a_next * cw_ref[2:3, cs]
        act = (_gelu(cv) * u).astype(BF16)
        d = jnp.dot(act, wd_ref[cs, :], preferred_element_type=F32)
        if c == 0:
            acc_ref[...] = d
        else:
            acc_ref[...] += d
    x2 = xt + acc_ref[...]
    gate = _sigmoid(jnp.dot(_rms(x2, gp_ref[...]).astype(BF16), wpg_ref[...], preferred_element_type=F32))
    ple = jnp.dot(p_ref[...].astype(BF16), wple_ref[...], preferred_element_type=F32)
    o_ref[...] = _rms(x2 + ple * gate, gfin_ref[...])


def _ffn_ple(streams, p2d, consts, tm, seq, fc):
    T, D = streams[0].shape
    row = lambda w: pl.BlockSpec((tm, w), lambda i: (i, 0))
    halo = lambda w: pl.BlockSpec((None, HALO_ROWS, w), lambda i: (i, 0, 0))
    halos = [_conv_halo(a, tm, seq) for a in streams]
    return pl.pallas_call(
        functools.partial(_ffn_ple_kernel, fc),
        grid=(T // tm,),
        in_specs=[row(a.shape[1]) for a in streams] + [halo(a.shape[1]) for a in streams]
                 + [row(p2d.shape[1])] + [_resident(c.shape) for c in consts],
        out_specs=row(D),
        out_shape=jax.ShapeDtypeStruct((T, D), F32),
        scratch_shapes=[pltpu.VMEM((tm + HALO_ROWS, D), BF16), pltpu.VMEM((tm, D), F32)],
        compiler_params=pltpu.CompilerParams(dimension_semantics=("parallel",),
                                             vmem_limit_bytes=VMEM_LIMIT),
        name="ffn_ple",
    )(*streams, *halos, p2d, *consts)


def _conv_halo(x1, tm, seq):
    T, D = x1.shape
    nt = T // tm
    xr = x1.reshape(nt, tm, D)
    zero = jnp.zeros((1, D), x1.dtype)
    prev = jnp.concatenate([zero, xr[:-1, tm - 1, :]], axis=0)
    nxt = jnp.concatenate([xr[1:, 0, :], zero], axis=0)
    start = (jnp.arange(nt) * tm) % seq
    prev = jnp.where((start == 0)[:, None], 0.0, prev)
    nxt = jnp.where((start + tm == seq)[:, None], 0.0, nxt)
    pad = jnp.zeros((nt, HALO_ROWS - 2, D), x1.dtype)
    return jnp.concatenate([prev[:, None, :], nxt[:, None, :], pad], axis=1)


def _rope_constants():
    half = ROT_DIM // 2
    inv_freq = (np.float32(ROPE_THETA) ** (-np.arange(half, dtype=np.float32) / np.float32(half))).astype(np.float32)
    d = np.arange(LANES) % DA_HEAD_DIM
    place = np.zeros((4 * half, 2 * LANES), np.float32)
    for f in range(half):
        hit = (d < ROT_DIM) & (d % half == f)
        sign = np.where(d < half, -1.0, 1.0)
        place[f, :LANES] = place[half + f, :LANES] = hit
        place[2 * half + f, LANES:] = place[3 * half + f, LANES:] = hit * sign
    return jnp.asarray(inv_freq.reshape(half, 1)), jnp.asarray(place, dtype=BF16)


def kernel(x, p, positions, norm_mix_g, w_in, lam_q1, lam_k1, lam_q2, lam_k2, da_subln_g, hg_lb_gamma, hg_norm_g, w_out, norm_ffn_g, w_ffn_gate, w_ffn_up, ffn_conv_w, ffn_conv_b, w_ffn_down, norm_ple_g, w_ple, w_ple_gate, final_norm_g):
    B, S, D = x.shape
    T = B * S
    depth = w_in.shape[0]
    tm = min(512, S)
    tq = min(512, S)
    tk = min(512, S)
    seg = min(2048, S)
    fc = (768, 768, 768, 512)
    row2 = lambda v: v.reshape(1, -1)
    xc = x.reshape(T, D)
    pos = positions.reshape(T // tm, 1, tm)
    rope_freq, rope_place = _rope_constants()
    for i in range(depth):
        lambda_init = 0.8 - 0.6 * math.exp(-0.3 * i)
        gam = hg_lb_gamma.reshape(-1, hg_lb_gamma.shape[-1])
        qkv, hq, bc, kin, hv, sg = _in_proj(xc, pos, rope_freq, rope_place, row2(norm_mix_g[i]), w_in[i].astype(BF16),
                                            gam, i, tm)
        lam_p = jnp.stack([lam_q1[i], lam_k1[i], lam_q2[i], lam_k2[i]], axis=0)
        o_da = _diff_attn(qkv.reshape(B, S, -1), lam_p, da_subln_g[i].reshape(-1, 1), lambda_init, tq, tk)
        y_f, y_b = _hgrn(hq.reshape(B, S, -1), hv.reshape(B, S, -1), bc.reshape(B, S, -1),
                         kin.reshape(B, S, -1), seg)
        streams = (xc, o_da.reshape(T, -1), y_f.reshape(T, -1), y_b.reshape(T, -1), sg)
        consts = (row2(hg_norm_g[i]), w_out[i].astype(BF16), row2(norm_ffn_g[i]),
                  w_ffn_gate[i].astype(BF16), w_ffn_up[i].astype(BF16), ffn_conv_w[i],
                  row2(ffn_conv_b[i]), w_ffn_down[i].astype(BF16), row2(norm_ple_g[i]),
                  w_ple_gate[i].astype(BF16), w_ple[i].astype(BF16), row2(final_norm_g))
        xc = _ffn_ple(streams, p[i].reshape(T, -1), consts, tm, S, fc)
        assert i == depth - 1, "multi-layer stacks need the final norm split out of ffn_ple"
    return xc.reshape(B, S, D)
```

```python
import functools
import math

import numpy as np
import jax
import jax.numpy as jnp
from jax import lax
from jax.experimental import pallas as pl
from jax.experimental.pallas import tpu as pltpu

F32 = jnp.float32
BF16 = jnp.bfloat16

EPS = 1e-6
ROPE_THETA = 500000.0
DA_HEADS = 4
DA_HEAD_DIM = 64
DA_V_DIM = 128
ROT_DIM = 16
HG_HEADS = 4
HG_DIM = 128
CONV_WIDTH = 3
HG_CHUNK = 64
HG_SUB = 16
HG_CHUNKS_PER_TRIP = 4
HG_MAX_SUB_DECAY = 100.0
LANES = 128
SUBLANES = 8
VMEM_LIMIT = 56 * 1024 * 1024


def _rms(xf, g):
    return xf * lax.rsqrt(jnp.mean(xf * xf, axis=-1, keepdims=True) + EPS) * g


def _sigmoid(z):
    return 0.5 + 0.5 * jnp.tanh(0.5 * z)


def _gelu(x):
    return 0.5 * x * (1.0 + lax.erf(x * (2.0 ** -0.5)))


def _resident(shape):
    return pl.BlockSpec(shape, lambda *_: (0,) * len(shape), pipeline_mode=pl.Buffered(1))


def _chunk_cumsum(x, reverse):
    n = x.shape[0]
    r = lax.broadcasted_iota(jnp.int32, x.shape, 0) & (HG_CHUNK - 1)
    s = 1
    while s < SUBLANES:
        if reverse:
            x = x + jnp.where(r < HG_CHUNK - s, pltpu.roll(x, n - s, axis=0), 0.0)
        else:
            x = x + jnp.where(r >= s, pltpu.roll(x, s, axis=0), 0.0)
        s *= 2
    while s < HG_CHUNK:
        parts = []
        for c0 in range(0, n, HG_CHUNK):
            lo, hi = x[c0:c0 + HG_CHUNK - s], x[c0 + s:c0 + HG_CHUNK]
            parts += [lo + hi, x[c0 + HG_CHUNK - s:c0 + HG_CHUNK]] if reverse else [x[c0:c0 + s], hi + lo]
        x = jnp.concatenate(parts, axis=0)
        s *= 2
    return x


def _in_proj_kernel(layer, n_slots, x_ref, pos_ref, freq_ref, place_ref, g_ref, w_ref, gam_ref,
                    qkv_ref, hq_ref, bc_ref, kin_ref, hv_ref, sg_ref):
    h = _rms(x_ref[...], g_ref[...]).astype(BF16)
    sec = qkv_ref.shape[1] // 3

    proj_all = jnp.dot(h, w_ref[...], preferred_element_type=F32)

    half = ROT_DIM // 2
    ang = freq_ref[...] * pos_ref[...].astype(F32)
    parts = []
    for trig in (jnp.cos(ang), jnp.sin(ang)):
        hi = trig.astype(BF16).astype(F32)
        parts += [hi, trig - hi]
    tab = lax.dot_general(jnp.concatenate(parts, axis=0).astype(BF16), place_ref[...],
                          (((0,), (0,)), ((), ())), preferred_element_type=F32)
    lane_d = lax.broadcasted_iota(jnp.int32, (1, LANES), 1) & (DA_HEAD_DIM - 1)
    cos = tab[:, :LANES] + jnp.where(lane_d >= ROT_DIM, 1.0, 0.0)
    sin = tab[:, LANES:]
    first_half = lane_d < half

    def rotary(t, scale):
        outs = []
        for hh in range(sec // LANES):
            blk = t[:, hh * LANES:(hh + 1) * LANES]
            partner = jnp.where(first_half, pltpu.roll(blk, LANES - half, axis=1),
                                pltpu.roll(blk, half, axis=1))
            rot = blk * cos + partner * sin
            outs.append(rot * scale if scale != 1.0 else rot)
        return jnp.concatenate(outs, axis=1)

    def put_q(t):
        qkv_ref[:, 0:sec] = rotary(t, DA_HEAD_DIM ** -0.5 * math.log2(math.e)).astype(BF16)

    def put_k(t):
        qkv_ref[:, sec:2 * sec] = rotary(t, 1.0).astype(BF16)

    def put_v(t):
        qkv_ref[:, 2 * sec:3 * sec] = t.astype(BF16)

    def put_hq(t):
        hq_ref[...] = t.astype(BF16)

    def put_forget(d, z):
        gam = gam_ref[d * n_slots:(d + 1) * n_slots, :]
        e = jnp.exp(gam - jnp.max(gam, axis=0, keepdims=True))
        lb = jnp.sum(e[0:layer + 1, :], axis=0, keepdims=True) / jnp.sum(e, axis=0, keepdims=True)
        half_span = 0.5 * (1.0 - lb)
        mid = 0.5 * (1.0 + lb)
        for r0 in range(0, z.shape[0], HG_CHUNK):
            w = half_span * jnp.tanh(0.5 * z[r0:r0 + HG_CHUNK])
            bc_ref[r0:r0 + HG_CHUNK, d * sec:(d + 1) * sec] = _chunk_cumsum(jnp.log2(mid + w),
                                                                             reverse=(d == 1))
            kin_ref[r0:r0 + HG_CHUNK, d * sec:(d + 1) * sec] = (half_span - w).astype(BF16)

    def put_hv(t):
        hv_ref[...] = t.astype(BF16)

    def put_gate(t):
        sg_ref[...] = (t * _sigmoid(t)).astype(BF16)

    epilogues = (put_q, put_k, put_v, put_hq, functools.partial(put_forget, 0),
                 functools.partial(put_forget, 1), put_hv, put_gate)
    for i, epilogue in enumerate(epilogues):
        epilogue(proj_all[:, i * sec:(i + 1) * sec])


def _in_proj(x2d, pos, freq, place, g, w_bf, gam, layer, tm):
    T, D = x2d.shape
    sec = w_bf.shape[1] // 8
    n_slots = gam.shape[0] // 2
    row = lambda w: pl.BlockSpec((tm, w), lambda i: (i, 0))
    return pl.pallas_call(
        functools.partial(_in_proj_kernel, layer, n_slots),
        grid=(T // tm,),
        in_specs=[row(D), pl.BlockSpec((None, 1, tm), lambda i: (i, 0, 0)),
                  _resident(freq.shape), _resident(place.shape), _resident(g.shape),
                  _resident(w_bf.shape), _resident(gam.shape)],
        out_specs=[row(3 * sec), row(sec), row(2 * sec), row(2 * sec), row(sec), row(sec)],
        out_shape=[jax.ShapeDtypeStruct((T, 3 * sec), BF16), jax.ShapeDtypeStruct((T, sec), BF16),
                   jax.ShapeDtypeStruct((T, 2 * sec), F32), jax.ShapeDtypeStruct((T, 2 * sec), BF16),
                   jax.ShapeDtypeStruct((T, sec), BF16), jax.ShapeDtypeStruct((T, sec), BF16)],
        compiler_params=pltpu.CompilerParams(dimension_semantics=("parallel",),
                                             vmem_limit_bytes=VMEM_LIMIT),
        name="in_proj",
    )(x2d, pos, freq, place, g, w_bf, gam)


DEN_ROWS = 16
ATTN_SEED_KEYS = 16
ATTN_MAX_JUMP = 100.0
ATTN_TILES_PER_TRIP = 2


def _diff_attn_kernel(lambda_init, tq, tk, q_ref, k_ref, v_ref, lam_ref, g_ref, o_ref,
                      vt_ref, qt_ref, acc_ref, s_ref):
    S = k_ref.shape[0]
    n_kv = S // tk
    dv = v_ref.shape[1]

    feat = lax.broadcasted_iota(jnp.int32, (q_ref.shape[1], tk), 0)

    def transpose_qv(i, carry):
        r = pl.ds(pl.multiple_of(i * tk, tk), tk)
        vt_ref[0:dv, r] = v_ref[r, :].astype(F32).T.astype(BF16)
        q_t = q_ref[r, :].astype(F32).T
        qt_ref[0, :, r] = jnp.where(feat < DA_HEAD_DIM, q_t, 0.0).astype(BF16)
        qt_ref[1, :, r] = jnp.where(feat >= DA_HEAD_DIM, q_t, 0.0).astype(BF16)
        return carry

    lax.fori_loop(0, n_kv, transpose_qv, 0)
    vt_ref[dv:dv + DEN_ROWS, :] = jnp.ones((DEN_ROWS, S), BF16)

    tile_cols = lambda qi: pl.ds(pl.multiple_of(qi * tq, tq), tq)

    def scores(mp, rows, q_cols):
        return jnp.dot(k_ref[rows, :], qt_ref[mp, :, q_cols], preferred_element_type=F32)

    def one_pass_tiles(tiles):
        chunk = lambda c: slice(c * tk, (c + 1) * tk)
        cols = [tile_cols(qi) for qi, _ in tiles]
        seeds = [[jnp.max(scores(mp, slice(0, ATTN_SEED_KEYS), qc), axis=0, keepdims=True)
                  for mp in range(2)] for qc in cols]
        steps = [(t, c) for t in range(len(tiles)) for c in range(n_kv)]
        jump = jnp.zeros_like(seeds[0][0])
        s_next = [scores(mp, chunk(0), cols[0]) for mp in range(2)]
        for n, (t, c) in enumerate(steps):
            qi, acc = tiles[t]
            if c == 0:
                m = list(seeds[t])
            for mp in range(2):
                s = s_next[mp]
                m_prev = m[mp]
                m[mp] = jnp.maximum(m_prev, jnp.max(s, axis=0, keepdims=True))
                p = jnp.exp2((s - m_prev).astype(BF16))
                pv = jnp.dot(vt_ref[:, chunk(c)], p, preferred_element_type=F32)
                if n + 1 < len(steps):
                    t_next, c_next = steps[n + 1]
                    s_next[mp] = scores(mp, chunk(c_next), cols[t_next])
                rescale = jnp.exp2(m_prev - m[mp])
                acc[mp] = rescale * (pv if c == 0 else acc[mp] + pv)
                jump = jnp.maximum(jump, m[mp] - m_prev)
            if c == n_kv - 1 and t + 1 < len(tiles):
                finalize(qi, acc)
        return jnp.max(jump)

    def two_pass_tile(qi, acc):
        q_cols = tile_cols(qi)

        def chunk_step(c, m):
            rows = pl.ds(pl.multiple_of(c * tk, tk), tk)
            m_out = []
            for mp in range(2):
                s_ref[...] = scores(mp, rows, q_cols)
                m_new = jnp.maximum(m[mp], jnp.max(s_ref[...], axis=0, keepdims=True))
                p = jnp.exp2((s_ref[...] - m_new).astype(BF16))
                acc[mp] = jnp.exp2(m[mp] - m_new) * acc[mp] + jnp.dot(
                    vt_ref[:, rows], p, preferred_element_type=F32)
                m_out.append(m_new)
            return tuple(m_out)

        for mp in range(2):
            acc[mp] = jnp.zeros(acc.shape[1:], F32)
        start = jnp.full((1, tq), -jnp.inf, F32)
        lax.fori_loop(0, n_kv, chunk_step, (start, start))

    def finalize(qi, acc):
        lam_p = lam_ref[...]
        lam = (jnp.exp(jnp.sum(lam_p[0:1, :] * lam_p[1:2, :], axis=-1, keepdims=True))
               - jnp.exp(jnp.sum(lam_p[2:3, :] * lam_p[3:4, :], axis=-1, keepdims=True)) + lambda_init)
        a0 = acc[0]
        a1 = acc[1]
        o = a0[0:dv] / a0[dv:dv + 1] - lam * (a1[0:dv] / a1[dv:dv + 1])
        o = o * lax.rsqrt(jnp.mean(o * o, axis=0, keepdims=True) + EPS) * g_ref[...]
        o_ref[tile_cols(qi), :] = (o * (1.0 - lambda_init)).T.astype(o_ref.dtype)

    n_slots = acc_ref.shape[0]
    assert (S // tq) % n_slots == 0

    def query_tiles(u, carry):
        tiles = [(u * n_slots + t, acc_ref.at[t]) for t in range(n_slots)]
        worst_jump = one_pass_tiles(tiles)

        @pl.when(jnp.logical_not(worst_jump <= ATTN_MAX_JUMP))
        def _():
            for t, (qi, acc) in enumerate(tiles):
                two_pass_tile(qi, acc)
                if t + 1 < n_slots:
                    finalize(qi, acc)

        finalize(*tiles[-1])
        return carry

    lax.fori_loop(0, S // tq // n_slots, query_tiles, 0)


def _diff_attn(qkv3, lam_p, g_col, lambda_init, tq, tk):
    B, S, _ = qkv3.shape
    H = DA_HEADS
    col = lambda off: pl.BlockSpec((None, S, LANES), lambda b, h: (b, 0, off + h))
    return pl.pallas_call(
        functools.partial(_diff_attn_kernel, lambda_init, tq, tk),
        grid=(B, H),
        in_specs=[col(0), col(H), col(2 * H), _resident(lam_p.shape), _resident(g_col.shape)],
        out_specs=col(0),
        out_shape=jax.ShapeDtypeStruct((B, S, H * DA_V_DIM), BF16),
        scratch_shapes=[pltpu.VMEM((DA_V_DIM + DEN_ROWS, S), BF16),
                        pltpu.VMEM((2, LANES, S), BF16),
                        pltpu.VMEM((ATTN_TILES_PER_TRIP, 2, DA_V_DIM + DEN_ROWS, tq), F32),
                        pltpu.VMEM((tk, tq), F32)],
        compiler_params=pltpu.CompilerParams(
            dimension_semantics=("parallel", "parallel"), vmem_limit_bytes=VMEM_LIMIT),
        name="diff_attn",
    )(qkv3, qkv3, qkv3, lam_p, g_col)


def _hgrn_scores(q, kin, v, b, reverse, exact_diag):
    C, n_sub = HG_CHUNK, HG_CHUNK // HG_SUB
    blk = lambda j: slice(j * HG_SUB, (j + 1) * HG_SUB)
    bound = [b[j * HG_SUB:j * HG_SUB + 1, :] if reverse else b[(j + 1) * HG_SUB - 1:(j + 1) * HG_SUB, :]
             for j in range(n_sub)]
    b_far = bound[0] if reverse else bound[-1]
    decay = lambda e: jnp.exp2(e).astype(BF16)
    k_til = jnp.concatenate([kin[blk(j)] * decay(bound[j] - b[blk(j)]) for j in range(n_sub)], axis=0)
    own = 0 if exact_diag else 1
    rows = [slice(0, (j + own) * HG_SUB) if reverse else slice((j + 1 - own) * HG_SUB, C)
            for j in range(n_sub)]
    q_til = jnp.concatenate([q[rows[j]] * decay(b[rows[j]] - bound[j]) for j in range(n_sub)
                             if rows[j].stop > rows[j].start], axis=0)
    offs = np.cumsum([0] + [r.stop - r.start for r in rows])
    scores = lax.dot_general(q_til, k_til, (((1,), (1,)), ((), ())),
                             preferred_element_type=F32)
    increment = lax.dot_general(v, kin * decay(b_far - b), (((0,), (0,)), ((), ())),
                                preferred_element_type=F32)
    return scores, [int(o) for o in offs], q * decay(b), jnp.exp2(b_far), increment


def _hgrn_intra(scores, offs, v, reverse, exact_diag, q, kin, b):
    C, n_sub = HG_CHUNK, HG_CHUNK // HG_SUB
    col = lax.broadcasted_iota(jnp.int32, (HG_SUB, C), 1)
    own = 0 if exact_diag else 1
    row_blocks = []
    for i in range(n_sub):
        js = range(i + 1 - own, n_sub) if reverse else range(0, i + own)
        a_i = jnp.zeros((HG_SUB, C), F32)
        for j in js:
            r0 = offs[j] + (i * HG_SUB if reverse else (i - j - 1 + own) * HG_SUB)
            a_i = jnp.where(col // HG_SUB == j, scores[r0:r0 + HG_SUB, :], a_i)
        if exact_diag:
            rs = slice(i * HG_SUB, (i + 1) * HG_SUB)
            q_i, k_i, b_i = q[rs].astype(F32), kin[rs].astype(F32), b[rs]
            for s in range(HG_SUB):
                w = q_i * (k_i[s:s + 1] * jnp.exp2(jnp.minimum(b_i - b_i[s:s + 1], 0.0)))
                a_i = jnp.where(col == i * HG_SUB + s, jnp.sum(w, axis=1, keepdims=True), a_i)
        row_blocks.append(a_i)
    attn = jnp.concatenate(row_blocks, axis=0)
    t_i = lax.broadcasted_iota(jnp.int32, (C, C), 0)
    s_i = lax.broadcasted_iota(jnp.int32, (C, C), 1)
    attn = jnp.where((t_i <= s_i) if reverse else (t_i >= s_i), attn, 0.0)
    return jnp.dot(attn.astype(BF16), v, preferred_element_type=F32)


def _hgrn_kernel(exact_ref, q_f_ref, q_b_ref, v_f_ref, v_b_ref, bc_f_ref, bc_b_ref, kin_f_ref, kin_b_ref,
                 y_f_ref, y_b_ref, st_f_ref, st_b_ref):
    @pl.when(pl.program_id(2) == 0)
    def _():
        st_f_ref[...] = jnp.zeros_like(st_f_ref)
        st_b_ref[...] = jnp.zeros_like(st_b_ref)

    n_chunks = q_f_ref.shape[0] // HG_CHUNK
    directions = ((False, q_f_ref, v_f_ref, bc_f_ref, kin_f_ref, y_f_ref, st_f_ref),
                  (True, q_b_ref, v_b_ref, bc_b_ref, kin_b_ref, y_b_ref, st_b_ref))

    def body(exact_diag, u, carry):
        work, intra = [], []
        for reverse, q_ref, v_ref, bc_ref, kin_ref, y_ref, st_ref in directions:
            for j in range(HG_CHUNKS_PER_TRIP):
                c = u * HG_CHUNKS_PER_TRIP + j
                cc = (n_chunks - 1 - c) if reverse else c
                rs = pl.ds(pl.multiple_of(cc * HG_CHUNK, HG_CHUNK), HG_CHUNK)
                v = v_ref[rs, :]
                stage1 = _hgrn_scores(q_ref[rs, :], kin_ref[rs, :], v, bc_ref[rs, :], reverse, exact_diag)
                work.append((reverse, rs, v, y_ref, st_ref, stage1))
        for (reverse, rs, v, y_ref, st_ref, stage1), (_, q_ref, _, bc_ref, kin_ref, _, _) in zip(
                work, [d for d in directions for _ in range(HG_CHUNKS_PER_TRIP)]):
            intra.append(_hgrn_intra(stage1[0], stage1[1], v, reverse, exact_diag,
                                     q_ref[rs, :], kin_ref[rs, :], bc_ref[rs, :]))
        states = {}
        for (reverse, rs, v, y_ref, st_ref, stage1), o_intra in zip(work, intra):
            _, _, q_state, decay, increment = stage1
            state_t = states.get(reverse)
            if state_t is None:
                state_t = st_ref[...]
            o_inter = lax.dot_general(q_state, state_t.astype(BF16), (((1,), (1,)), ((), ())),
                                      preferred_element_type=F32)
            states[reverse] = decay * state_t + increment
            y_ref[rs, :] = (o_intra + o_inter).astype(y_ref.dtype)
        for reverse, *_, st_ref in directions:
            st_ref[...] = states[reverse]
        return carry

    assert n_chunks % HG_CHUNKS_PER_TRIP == 0

    for exact_diag in (False, True):
        @pl.when((exact_ref[0] != 0) == exact_diag)
        def _():
            lax.fori_loop(0, n_chunks // HG_CHUNKS_PER_TRIP, functools.partial(body, exact_diag), 0)


def _hgrn(exact_flag, hq3, hv3, bc3, kin3, seg):
    B, S, W = hq3.shape
    H = W // HG_DIM
    n_seg = S // seg
    fwd = lambda off: pl.BlockSpec((None, seg, HG_DIM), lambda b, h, s: (b, s, off + h))
    bwd = lambda off: pl.BlockSpec((None, seg, HG_DIM), lambda b, h, s: (b, n_seg - 1 - s, off + h))
    return pl.pallas_call(
        _hgrn_kernel,
        grid=(B, H, n_seg),
        in_specs=[pl.BlockSpec(memory_space=pltpu.SMEM),
                  fwd(0), bwd(0), fwd(0), bwd(0), fwd(0), bwd(H), fwd(0), bwd(H)],
        out_specs=[fwd(0), bwd(0)],
        out_shape=[jax.ShapeDtypeStruct((B, S, W), BF16), jax.ShapeDtypeStruct((B, S, W), BF16)],
        scratch_shapes=[pltpu.VMEM((HG_DIM, HG_DIM), F32), pltpu.VMEM((HG_DIM, HG_DIM), F32)],
        compiler_params=pltpu.CompilerParams(
            dimension_semantics=("parallel", "parallel", "arbitrary"), vmem_limit_bytes=VMEM_LIMIT),
        name="hgrn2",
    )(exact_flag, hq3, hq3, hv3, hv3, bc3, bc3, kin3, kin3)


def _hgrn_needs_exact_diag(lb_gamma, layer):
    e = jax.nn.softmax(lb_gamma.astype(F32), axis=1)
    lb = jnp.sum(e[:, :layer + 1], axis=1)
    worst = (HG_SUB - 1) * jnp.max(-jnp.log2(lb))
    return jnp.logical_not(worst <= HG_MAX_SUB_DECAY).astype(jnp.int32).reshape(1)


def _mixer_residual(x, oda, yf, yb, sg, g_hg, w_ref):
    half = oda.shape[1]
    y = yf.astype(F32) + yb.astype(F32)
    sg = sg.astype(F32)
    heads = []
    for hh in range(y.shape[1] // HG_DIM):
        cs = slice(hh * HG_DIM, (hh + 1) * HG_DIM)
        heads.append(_rms(y[:, cs], g_hg) * sg[:, cs])
    o_hg = jnp.concatenate(heads, axis=1).astype(BF16)
    return (x + jnp.dot(oda, w_ref[0:half, :], preferred_element_type=F32)
            + jnp.dot(o_hg, w_ref[half:, :], preferred_element_type=F32))


HALO_ROWS = 16


def _ffn_ple_kernel(fc, x_ref, oda_ref, yf_ref, yb_ref, sg_ref, xh_ref, odah_ref, yfh_ref, ybh_ref,
                    sgh_ref, p_ref, ghg_ref, wo_ref, gf_ref, wg_ref, wu_ref, cw_ref, cb_ref, wd_ref,
                    gp_ref, wpg_ref, wple_ref, gfin_ref, o_ref, h_ref, acc_ref):
    tm = x_ref.shape[0]
    ext = lambda tile_ref, halo_ref: jnp.concatenate([tile_ref[...], halo_ref[...]], axis=0)
    x1 = _mixer_residual(ext(x_ref, xh_ref), ext(oda_ref, odah_ref), ext(yf_ref, yfh_ref),
                         ext(yb_ref, ybh_ref), ext(sg_ref, sgh_ref), ghg_ref[...], wo_ref)
    xt = x1[0:tm]
    h_ref[...] = _rms(x1, gf_ref[...]).astype(BF16)
    assert sum(fc) == wg_ref.shape[1]
    edges = np.cumsum((0,) + tuple(fc))
    chunks = [slice(int(lo), int(hi)) for lo, hi in zip(edges[:-1], edges[1:])]

    def up_gate(cs):
        return (jnp.dot(h_ref[...], wg_ref[:, cs], preferred_element_type=F32),
                jnp.dot(h_ref[0:tm, :], wu_ref[:, cs], preferred_element_type=F32))

    nxt = up_gate(chunks[0])
    for c, cs in enumerate(chunks):
        a_ext, u = nxt
        if c + 1 < len(chunks):
            nxt = up_gate(chunks[c + 1])
        row = lax.broadcasted_iota(jnp.int32, u.shape, 0)
        a = a_ext[0:tm]
        a_prev = jnp.where(row == 0, a_ext[tm:tm + 1], pltpu.roll(a, 1, axis=0))
        a_next = jnp.where(row == tm - 1, a_ext[tm + 1:tm + 2], pltpu.roll(a, tm - 1, axis=0))
        cv = cb_ref[:, cs] + a_prev * cw_ref[0:1, cs] + a * cw_ref[1:2, cs] + a_next * cw_ref[2:3, cs]
        act = (_gelu(cv) * u).astype(BF16)
        d = jnp.dot(act, wd_ref[cs, :], preferred_element_type=F32)
        if c == 0:
            acc_ref[...] = d
        else:
            acc_ref[...] += d
    x2 = xt + acc_ref[...]
    gate = _sigmoid(jnp.dot(_rms(x2, gp_ref[...]).astype(BF16), wpg_ref[...], preferred_element_type=F32))
    ple = jnp.dot(p_ref[...].astype(BF16), wple_ref[...], preferred_element_type=F32)
    o_ref[...] = _rms(x2 + ple * gate, gfin_ref[...])


def _ffn_ple(streams, p2d, consts, tm, seq, fc):
    T, D = streams[0].shape
    row = lambda w: pl.BlockSpec((tm, w), lambda i: (i, 0))
    halo = lambda w: pl.BlockSpec((None, HALO_ROWS, w), lambda i: (i, 0, 0))
    halos = [_conv_halo(a, tm, seq) for a in streams]
    return pl.pallas_call(
        functools.partial(_ffn_ple_kernel, fc),
        grid=(T // tm,),
        in_specs=[row(a.shape[1]) for a in streams] + [halo(a.shape[1]) for a in streams]
                 + [row(p2d.shape[1])] + [_resident(c.shape) for c in consts],
        out_specs=row(D),
        out_shape=jax.ShapeDtypeStruct((T, D), F32),
        scratch_shapes=[pltpu.VMEM((tm + HALO_ROWS, D), BF16), pltpu.VMEM((tm, D), F32)],
        compiler_params=pltpu.CompilerParams(dimension_semantics=("parallel",),
                                             vmem_limit_bytes=VMEM_LIMIT),
        name="ffn_ple",
    )(*streams, *halos, p2d, *consts)


def _conv_halo(x1, tm, seq):
    T, D = x1.shape
    nt = T // tm
    xr = x1.reshape(nt, tm, D)
    zero = jnp.zeros((1, D), x1.dtype)
    prev = jnp.concatenate([zero, xr[:-1, tm - 1, :]], axis=0)
    nxt = jnp.concatenate([xr[1:, 0, :], zero], axis=0)
    start = (jnp.arange(nt) * tm) % seq
    prev = jnp.where((start == 0)[:, None], 0.0, prev)
    nxt = jnp.where((start + tm == seq)[:, None], 0.0, nxt)
    pad = jnp.zeros((nt, HALO_ROWS - 2, D), x1.dtype)
    return jnp.concatenate([prev[:, None, :], nxt[:, None, :], pad], axis=1)


def _rope_constants():
    half = ROT_DIM // 2
    inv_freq = (np.float32(ROPE_THETA) ** (-np.arange(half, dtype=np.float32) / np.float32(half))).astype(np.float32)
    d = np.arange(LANES) % DA_HEAD_DIM
    place = np.zeros((4 * half, 2 * LANES), np.float32)
    for f in range(half):
        hit = (d < ROT_DIM) & (d % half == f)
        sign = np.where(d < half, -1.0, 1.0)
        place[f, :LANES] = place[half + f, :LANES] = hit
        place[2 * half + f, LANES:] = place[3 * half + f, LANES:] = hit * sign
    return jnp.asarray(inv_freq.reshape(half, 1)), jnp.asarray(place, dtype=BF16)


def kernel(x, p, positions, norm_mix_g, w_in, lam_q1, lam_k1, lam_q2, lam_k2, da_subln_g, hg_lb_gamma, hg_norm_g, w_out, norm_ffn_g, w_ffn_gate, w_ffn_up, ffn_conv_w, ffn_conv_b, w_ffn_down, norm_ple_g, w_ple, w_ple_gate, final_norm_g):
    B, S, D = x.shape
    T = B * S
    depth = w_in.shape[0]
    tm = min(512, S)
    tq = min(512, S)
    tk = min(512, S)
    seg = min(2048, S)
    fc = (768, 768, 768, 512)
    row2 = lambda v: v.reshape(1, -1)
    xc = x.reshape(T, D)
    pos = positions.reshape(T // tm, 1, tm)
    rope_freq, rope_place = _rope_constants()
    for i in range(depth):
        lambda_init = 0.8 - 0.6 * math.exp(-0.3 * i)
        gam = hg_lb_gamma.reshape(-1, hg_lb_gamma.shape[-1])
        qkv, hq, bc, kin, hv, sg = _in_proj(xc, pos, rope_freq, rope_place, row2(norm_mix_g[i]), w_in[i].astype(BF16),
                                            gam, i, tm)
        lam_p = jnp.stack([lam_q1[i], lam_k1[i], lam_q2[i], lam_k2[i]], axis=0)
        o_da = _diff_attn(qkv.reshape(B, S, -1), lam_p, da_subln_g[i].reshape(-1, 1), lambda_init, tq, tk)
        y_f, y_b = _hgrn(_hgrn_needs_exact_diag(hg_lb_gamma, i), hq.reshape(B, S, -1), hv.reshape(B, S, -1),
                         bc.reshape(B, S, -1), kin.reshape(B, S, -1), seg)
        streams = (xc, o_da.reshape(T, -1), y_f.reshape(T, -1), y_b.reshape(T, -1), sg)
        consts = (row2(hg_norm_g[i]), w_out[i].astype(BF16), row2(norm_ffn_g[i]),
                  w_ffn_gate[i].astype(BF16), w_ffn_up[i].astype(BF16), ffn_conv_w[i],
                  row2(ffn_conv_b[i]), w_ffn_down[i].astype(BF16), row2(norm_ple_g[i]),
                  w_ple_gate[i].astype(BF16), w_ple[i].astype(BF16), row2(final_norm_g))
        xc = _ffn_ple(streams, p[i].reshape(T, -1), consts, tm, S, fc)
        assert i == depth - 1, "multi-layer stacks need the final norm split out of ffn_ple"
    return xc.reshape(B, S, D)
```

```python
import functools
import math

import numpy as np
import jax
import jax.numpy as jnp
from jax import lax
from jax.experimental import pallas as pl
from jax.experimental.pallas import tpu as pltpu

F32 = jnp.float32
BF16 = jnp.bfloat16

EPS = 1e-6
ROPE_THETA = 500000.0
DA_HEADS = 4
DA_HEAD_DIM = 64
DA_V_DIM = 128
ROT_DIM = 16
HG_HEADS = 4
HG_DIM = 128
CONV_WIDTH = 3
HG_CHUNK = 64
HG_SUB = 16
HG_CHUNKS_PER_TRIP = 4
HG_MAX_SUB_DECAY = 100.0
LANES = 128
SUBLANES = 8
VMEM_LIMIT = 56 * 1024 * 1024


def _rms(xf, g):
    return xf * lax.rsqrt(jnp.mean(xf * xf, axis=-1, keepdims=True) + EPS) * g


def _sigmoid(z):
    return 0.5 + 0.5 * jnp.tanh(0.5 * z)


def _gelu(x):
    return 0.5 * x * (1.0 + lax.erf(x * (2.0 ** -0.5)))


def _resident(shape):
    return pl.BlockSpec(shape, lambda *_: (0,) * len(shape), pipeline_mode=pl.Buffered(1))


def _chunk_cumsum(x, reverse):
    n = x.shape[0]
    r = lax.broadcasted_iota(jnp.int32, x.shape, 0) & (HG_CHUNK - 1)
    s = 1
    while s < SUBLANES:
        if reverse:
            x = x + jnp.where(r < HG_CHUNK - s, pltpu.roll(x, n - s, axis=0), 0.0)
        else:
            x = x + jnp.where(r >= s, pltpu.roll(x, s, axis=0), 0.0)
        s *= 2
    while s < HG_CHUNK:
        parts = []
        for c0 in range(0, n, HG_CHUNK):
            lo, hi = x[c0:c0 + HG_CHUNK - s], x[c0 + s:c0 + HG_CHUNK]
            parts += [lo + hi, x[c0 + HG_CHUNK - s:c0 + HG_CHUNK]] if reverse else [x[c0:c0 + s], hi + lo]
        x = jnp.concatenate(parts, axis=0)
        s *= 2
    return x


def _in_proj_kernel(layer, n_slots, x_ref, pos_ref, freq_ref, place_ref, g_ref, w_ref, gam_ref,
                    qkv_ref, hq_ref, bc_ref, kin_ref, hv_ref, sg_ref):
    h = _rms(x_ref[...], g_ref[...]).astype(BF16)
    sec = qkv_ref.shape[1] // 3

    proj_all = jnp.dot(h, w_ref[...], preferred_element_type=F32)

    half = ROT_DIM // 2
    ang = freq_ref[...] * pos_ref[...].astype(F32)
    parts = []
    for trig in (jnp.cos(ang), jnp.sin(ang)):
        hi = trig.astype(BF16).astype(F32)
        parts += [hi, trig - hi]
    tab = lax.dot_general(jnp.concatenate(parts, axis=0).astype(BF16), place_ref[...],
                          (((0,), (0,)), ((), ())), preferred_element_type=F32)
    lane_d = lax.broadcasted_iota(jnp.int32, (1, LANES), 1) & (DA_HEAD_DIM - 1)
    cos = tab[:, :LANES] + jnp.where(lane_d >= ROT_DIM, 1.0, 0.0)
    sin = tab[:, LANES:]
    first_half = lane_d < half

    def rotary(t, scale):
        outs = []
        for hh in range(sec // LANES):
            blk = t[:, hh * LANES:(hh + 1) * LANES]
            partner = jnp.where(first_half, pltpu.roll(blk, LANES - half, axis=1),
                                pltpu.roll(blk, half, axis=1))
            rot = blk * cos + partner * sin
            outs.append(rot * scale if scale != 1.0 else rot)
        return jnp.concatenate(outs, axis=1)

    def put_q(t):
        qkv_ref[:, 0:sec] = rotary(t, DA_HEAD_DIM ** -0.5 * math.log2(math.e)).astype(BF16)

    def put_k(t):
        qkv_ref[:, sec:2 * sec] = rotary(t, 1.0).astype(BF16)

    def put_v(t):
        qkv_ref[:, 2 * sec:3 * sec] = t.astype(BF16)

    def put_hq(t):
        hq_ref[...] = t.astype(BF16)

    def put_forget(d, z):
        gam = gam_ref[d * n_slots:(d + 1) * n_slots, :]
        e = jnp.exp(gam - jnp.max(gam, axis=0, keepdims=True))
        lb = jnp.sum(e[0:layer + 1, :], axis=0, keepdims=True) / jnp.sum(e, axis=0, keepdims=True)
        half_span = 0.5 * (1.0 - lb)
        mid = 0.5 * (1.0 + lb)
        for r0 in range(0, z.shape[0], HG_CHUNK):
            w = half_span * jnp.tanh(0.5 * z[r0:r0 + HG_CHUNK])
            bc_ref[r0:r0 + HG_CHUNK, d * sec:(d + 1) * sec] = _chunk_cumsum(jnp.log2(mid + w),
                                                                             reverse=(d == 1))
            kin_ref[r0:r0 + HG_CHUNK, d * sec:(d + 1) * sec] = (half_span - w).astype(BF16)

    def put_hv(t):
        hv_ref[...] = t.astype(BF16)

    def put_gate(t):
        sg_ref[...] = (t * _sigmoid(t)).astype(BF16)

    epilogues = (put_q, put_k, put_v, put_hq, functools.partial(put_forget, 0),
                 functools.partial(put_forget, 1), put_hv, put_gate)
    for i, epilogue in enumerate(epilogues):
        epilogue(proj_all[:, i * sec:(i + 1) * sec])


def _in_proj(x2d, pos, freq, place, g, w_bf, gam, layer, tm):
    T, D = x2d.shape
    sec = w_bf.shape[1] // 8
    n_slots = gam.shape[0] // 2
    row = lambda w: pl.BlockSpec((tm, w), lambda i: (i, 0))
    return pl.pallas_call(
        functools.partial(_in_proj_kernel, layer, n_slots),
        grid=(T // tm,),
        in_specs=[row(D), pl.BlockSpec((None, 1, tm), lambda i: (i, 0, 0)),
                  _resident(freq.shape), _resident(place.shape), _resident(g.shape),
                  _resident(w_bf.shape), _resident(gam.shape)],
        out_specs=[row(3 * sec), row(sec), row(2 * sec), row(2 * sec), row(sec), row(sec)],
        out_shape=[jax.ShapeDtypeStruct((T, 3 * sec), BF16), jax.ShapeDtypeStruct((T, sec), BF16),
                   jax.ShapeDtypeStruct((T, 2 * sec), F32), jax.ShapeDtypeStruct((T, 2 * sec), BF16),
                   jax.ShapeDtypeStruct((T, sec), BF16), jax.ShapeDtypeStruct((T, sec), BF16)],
        compiler_params=pltpu.CompilerParams(dimension_semantics=("parallel",),
                                             vmem_limit_bytes=VMEM_LIMIT),
        name="in_proj",
    )(x2d, pos, freq, place, g, w_bf, gam)


DEN_ROWS = 16
ATTN_SEED_KEYS = 16
ATTN_MAX_JUMP = 100.0
ATTN_TILES_PER_TRIP = 2


def _diff_attn_kernel(lambda_init, tq, tk, q_ref, k_ref, v_ref, lam_ref, g_ref, o_ref,
                      vt_ref, qt_ref, acc_ref, s_ref):
    S = k_ref.shape[0]
    n_kv = S // tk
    dv = v_ref.shape[1]

    feat = lax.broadcasted_iota(jnp.int32, (q_ref.shape[1], tk), 0)

    def transpose_qv(i, carry):
        r = pl.ds(pl.multiple_of(i * tk, tk), tk)
        vt_ref[0:dv, r] = v_ref[r, :].astype(F32).T.astype(BF16)
        q_t = q_ref[r, :].astype(F32).T
        qt_ref[0, :, r] = jnp.where(feat < DA_HEAD_DIM, q_t, 0.0).astype(BF16)
        qt_ref[1, :, r] = jnp.where(feat >= DA_HEAD_DIM, q_t, 0.0).astype(BF16)
        return carry

    lax.fori_loop(0, n_kv, transpose_qv, 0)
    vt_ref[dv:dv + DEN_ROWS, :] = jnp.ones((DEN_ROWS, S), BF16)

    tile_cols = lambda qi: pl.ds(pl.multiple_of(qi * tq, tq), tq)

    def scores(mp, rows, q_cols):
        return jnp.dot(k_ref[rows, :], qt_ref[mp, :, q_cols], preferred_element_type=F32)

    def one_pass_tiles(tiles):
        chunk = lambda c: slice(c * tk, (c + 1) * tk)
        cols = [tile_cols(qi) for qi, _ in tiles]
        seeds = [[jnp.max(scores(mp, slice(0, ATTN_SEED_KEYS), qc), axis=0, keepdims=True)
                  for mp in range(2)] for qc in cols]
        steps = [(t, c) for t in range(len(tiles)) for c in range(n_kv)]
        jump = jnp.zeros_like(seeds[0][0])
        s_next = [scores(mp, chunk(0), cols[0]) for mp in range(2)]
        for n, (t, c) in enumerate(steps):
            qi, acc = tiles[t]
            if c == 0:
                m = list(seeds[t])
            for mp in range(2):
                s = s_next[mp]
                m_prev = m[mp]
                m[mp] = jnp.maximum(m_prev, jnp.max(s, axis=0, keepdims=True))
                p = jnp.exp2((s - m_prev).astype(BF16))
                pv = jnp.dot(vt_ref[:, chunk(c)], p, preferred_element_type=F32)
                if n + 1 < len(steps):
                    t_next, c_next = steps[n + 1]
                    s_next[mp] = scores(mp, chunk(c_next), cols[t_next])
                rescale = jnp.exp2(m_prev - m[mp])
                acc[mp] = rescale * (pv if c == 0 else acc[mp] + pv)
                jump = jnp.maximum(jump, m[mp] - m_prev)
            if c == n_kv - 1 and t + 1 < len(tiles):
                finalize(qi, acc)
        return jnp.max(jump)

    def two_pass_tile(qi, acc):
        q_cols = tile_cols(qi)

        def chunk_step(c, m):
            rows = pl.ds(pl.multiple_of(c * tk, tk), tk)
            m_out = []
            for mp in range(2):
                s_ref[...] = scores(mp, rows, q_cols)
                m_new = jnp.maximum(m[mp], jnp.max(s_ref[...], axis=0, keepdims=True))
                p = jnp.exp2((s_ref[...] - m_new).astype(BF16))
                acc[mp] = jnp.exp2(m[mp] - m_new) * acc[mp] + jnp.dot(
                    vt_ref[:, rows], p, preferred_element_type=F32)
                m_out.append(m_new)
            return tuple(m_out)

        for mp in range(2):
            acc[mp] = jnp.zeros(acc.shape[1:], F32)
        start = jnp.full((1, tq), -jnp.inf, F32)
        lax.fori_loop(0, n_kv, chunk_step, (start, start))

    def finalize(qi, acc):
        lam_p = lam_ref[...]
        lam = (jnp.exp(jnp.sum(lam_p[0:1, :] * lam_p[1:2, :], axis=-1, keepdims=True))
               - jnp.exp(jnp.sum(lam_p[2:3, :] * lam_p[3:4, :], axis=-1, keepdims=True)) + lambda_init)
        a0 = acc[0]
        a1 = acc[1]
        o = a0[0:dv] / a0[dv:dv + 1] - lam * (a1[0:dv] / a1[dv:dv + 1])
        o = o * lax.rsqrt(jnp.mean(o * o, axis=0, keepdims=True) + EPS) * g_ref[...]
        o_ref[tile_cols(qi), :] = (o * (1.0 - lambda_init)).T.astype(o_ref.dtype)

    n_slots = acc_ref.shape[0]
    assert (S // tq) % n_slots == 0

    def query_tiles(u, carry):
        tiles = [(u * n_slots + t, acc_ref.at[t]) for t in range(n_slots)]
        worst_jump = one_pass_tiles(tiles)

        @pl.when(jnp.logical_not(worst_jump <= ATTN_MAX_JUMP))
        def _():
            for t, (qi, acc) in enumerate(tiles):
                two_pass_tile(qi, acc)
                if t + 1 < n_slots:
                    finalize(qi, acc)

        finalize(*tiles[-1])
        return carry

    lax.fori_loop(0, S // tq // n_slots, query_tiles, 0)


def _diff_attn(qkv3, lam_p, g_col, lambda_init, tq, tk):
    B, S, _ = qkv3.shape
    H = DA_HEADS
    col = lambda off: pl.BlockSpec((None, S, LANES), lambda b, h: (b, 0, off + h))
    return pl.pallas_call(
        functools.partial(_diff_attn_kernel, lambda_init, tq, tk),
        grid=(B, H),
        in_specs=[col(0), col(H), col(2 * H), _resident(lam_p.shape), _resident(g_col.shape)],
        out_specs=col(0),
        out_shape=jax.ShapeDtypeStruct((B, S, H * DA_V_DIM), BF16),
        scratch_shapes=[pltpu.VMEM((DA_V_DIM + DEN_ROWS, S), BF16),
                        pltpu.VMEM((2, LANES, S), BF16),
                        pltpu.VMEM((ATTN_TILES_PER_TRIP, 2, DA_V_DIM + DEN_ROWS, tq), F32),
                        pltpu.VMEM((tk, tq), F32)],
        compiler_params=pltpu.CompilerParams(
            dimension_semantics=("parallel", "parallel"), vmem_limit_bytes=VMEM_LIMIT),
        name="diff_attn",
    )(qkv3, qkv3, qkv3, lam_p, g_col)


def _hgrn_scores(q, kin, v, b, reverse, exact_diag):
    C, n_sub = HG_CHUNK, HG_CHUNK // HG_SUB
    blk = lambda j: slice(j * HG_SUB, (j + 1) * HG_SUB)
    bound = [b[j * HG_SUB:j * HG_SUB + 1, :] if reverse else b[(j + 1) * HG_SUB - 1:(j + 1) * HG_SUB, :]
             for j in range(n_sub)]
    b_far = bound[0] if reverse else bound[-1]
    decay = lambda e: jnp.exp2(e).astype(BF16)
    k_til = jnp.concatenate([kin[blk(j)] * decay(bound[j] - b[blk(j)]) for j in range(n_sub)], axis=0)
    own = 0 if exact_diag else 1
    rows = [slice(0, (j + own) * HG_SUB) if reverse else slice((j + 1 - own) * HG_SUB, C)
            for j in range(n_sub)]
    q_til = jnp.concatenate([q[rows[j]] * decay(b[rows[j]] - bound[j]) for j in range(n_sub)
                             if rows[j].stop > rows[j].start], axis=0)
    offs = np.cumsum([0] + [r.stop - r.start for r in rows])
    scores = lax.dot_general(q_til, k_til, (((1,), (1,)), ((), ())),
                             preferred_element_type=F32)
    increment = lax.dot_general(v, kin * decay(b_far - b), (((0,), (0,)), ((), ())),
                                preferred_element_type=F32)
    return scores, [int(o) for o in offs], q * decay(b), jnp.exp2(b_far), increment


def _hgrn_intra(scores, offs, v, reverse, exact_diag, q, kin, b):
    C, n_sub = HG_CHUNK, HG_CHUNK // HG_SUB
    col = lax.broadcasted_iota(jnp.int32, (HG_SUB, C), 1)
    own = 0 if exact_diag else 1
    row_blocks = []
    for i in range(n_sub):
        js = range(i + 1 - own, n_sub) if reverse else range(0, i + own)
        a_i = jnp.zeros((HG_SUB, C), F32)
        for j in js:
            r0 = offs[j] + (i * HG_SUB if reverse else (i - j - 1 + own) * HG_SUB)
            a_i = jnp.where(col // HG_SUB == j, scores[r0:r0 + HG_SUB, :], a_i)
        if exact_diag:
            rs = slice(i * HG_SUB, (i + 1) * HG_SUB)
            q_i, k_i, b_i = q[rs].astype(F32), kin[rs].astype(F32), b[rs]
            for s in range(HG_SUB):
                w = q_i * (k_i[s:s + 1] * jnp.exp2(jnp.minimum(b_i - b_i[s:s + 1], 0.0)))
                a_i = jnp.where(col == i * HG_SUB + s, jnp.sum(w, axis=1, keepdims=True), a_i)
        row_blocks.append(a_i)
    attn = jnp.concatenate(row_blocks, axis=0)
    t_i = lax.broadcasted_iota(jnp.int32, (C, C), 0)
    s_i = lax.broadcasted_iota(jnp.int32, (C, C), 1)
    attn = jnp.where((t_i <= s_i) if reverse else (t_i >= s_i), attn, 0.0)
    return jnp.dot(attn.astype(BF16), v, preferred_element_type=F32)


def _hgrn_kernel(exact_ref, q_f_ref, q_b_ref, v_f_ref, v_b_ref, bc_f_ref, bc_b_ref, kin_f_ref, kin_b_ref,
                 y_f_ref, y_b_ref, st_f_ref, st_b_ref):
    @pl.when(pl.program_id(2) == 0)
    def _():
        st_f_ref[...] = jnp.zeros_like(st_f_ref)
        st_b_ref[...] = jnp.zeros_like(st_b_ref)

    n_chunks = q_f_ref.shape[0] // HG_CHUNK
    directions = ((False, q_f_ref, v_f_ref, bc_f_ref, kin_f_ref, y_f_ref, st_f_ref),
                  (True, q_b_ref, v_b_ref, bc_b_ref, kin_b_ref, y_b_ref, st_b_ref))

    def body(exact_diag, u, carry):
        work, intra = [], []
        for reverse, q_ref, v_ref, bc_ref, kin_ref, y_ref, st_ref in directions:
            for j in range(HG_CHUNKS_PER_TRIP):
                c = u * HG_CHUNKS_PER_TRIP + j
                cc = (n_chunks - 1 - c) if reverse else c
                rs = pl.ds(pl.multiple_of(cc * HG_CHUNK, HG_CHUNK), HG_CHUNK)
                v = v_ref[rs, :]
                stage1 = _hgrn_scores(q_ref[rs, :], kin_ref[rs, :], v, bc_ref[rs, :], reverse, exact_diag)
                work.append((reverse, rs, v, y_ref, st_ref, stage1))
        for (reverse, rs, v, y_ref, st_ref, stage1), (_, q_ref, _, bc_ref, kin_ref, _, _) in zip(
                work, [d for d in directions for _ in range(HG_CHUNKS_PER_TRIP)]):
            intra.append(_hgrn_intra(stage1[0], stage1[1], v, reverse, exact_diag,
                                     q_ref[rs, :], kin_ref[rs, :], bc_ref[rs, :]))
        states = {}
        for (reverse, rs, v, y_ref, st_ref, stage1), o_intra in zip(work, intra):
            _, _, q_state, decay, increment = stage1
            state_t = states.get(reverse)
            if state_t is None:
                state_t = st_ref[...]
            o_inter = lax.dot_general(q_state, state_t.astype(BF16), (((1,), (1,)), ((), ())),
                                      preferred_element_type=F32)
            states[reverse] = decay * state_t + increment
            y_ref[rs, :] = (o_intra + o_inter).astype(y_ref.dtype)
        for reverse, *_, st_ref in directions:
            st_ref[...] = states[reverse]
        return carry

    assert n_chunks % HG_CHUNKS_PER_TRIP == 0

    for exact_diag in (False, True):
        @pl.when((exact_ref[0] != 0) == exact_diag)
        def _():
            lax.fori_loop(0, n_chunks // HG_CHUNKS_PER_TRIP, functools.partial(body, exact_diag), 0)


def _hgrn(exact_flag, hq3, hv3, bc3, kin3, seg):
    B, S, W = hq3.shape
    H = W // HG_DIM
    n_seg = S // seg
    fwd = lambda off: pl.BlockSpec((None, seg, HG_DIM), lambda b, h, s: (b, s, off + h))
    bwd = lambda off: pl.BlockSpec((None, seg, HG_DIM), lambda b, h, s: (b, n_seg - 1 - s, off + h))
    return pl.pallas_call(
        _hgrn_kernel,
        grid=(B, H, n_seg),
        in_specs=[pl.BlockSpec(memory_space=pltpu.SMEM),
                  fwd(0), bwd(0), fwd(0), bwd(0), fwd(0), bwd(H), fwd(0), bwd(H)],
        out_specs=[fwd(0), bwd(0)],
        out_shape=[jax.ShapeDtypeStruct((B, S, W), BF16), jax.ShapeDtypeStruct((B, S, W), BF16)],
        scratch_shapes=[pltpu.VMEM((HG_DIM, HG_DIM), F32), pltpu.VMEM((HG_DIM, HG_DIM), F32)],
        compiler_params=pltpu.CompilerParams(
            dimension_semantics=("parallel", "parallel", "arbitrary"), vmem_limit_bytes=VMEM_LIMIT),
        name="hgrn2",
    )(exact_flag, hq3, hq3, hv3, hv3, bc3, bc3, kin3, kin3)


def _hgrn_needs_exact_diag(lb_gamma, layer):
    e = jax.nn.softmax(lb_gamma.astype(F32), axis=1)
    lb = jnp.sum(e[:, :layer + 1], axis=1)
    worst = (HG_SUB - 1) * jnp.max(-jnp.log2(lb))
    return jnp.logical_not(worst <= HG_MAX_SUB_DECAY).astype(jnp.int32).reshape(1)


def _mixer_residual(x, oda, yf, yb, sg, g_hg, w_ref):
    half = oda.shape[1]
    y = yf.astype(F32) + yb.astype(F32)
    sg = sg.astype(F32)
    heads = []
    for hh in range(y.shape[1] // HG_DIM):
        cs = slice(hh * HG_DIM, (hh + 1) * HG_DIM)
        heads.append(_rms(y[:, cs], g_hg) * sg[:, cs])
    o_hg = jnp.concatenate(heads, axis=1).astype(BF16)
    return (x + jnp.dot(oda, w_ref[0:half, :], preferred_element_type=F32)
            + jnp.dot(o_hg, w_ref[half:, :], preferred_element_type=F32))


HALO_ROWS = 16


def _ffn_ple_kernel(fc, x_ref, oda_ref, yf_ref, yb_ref, sg_ref, xh_ref, odah_ref, yfh_ref, ybh_ref,
                    sgh_ref, p_ref, ghg_ref, wo_ref, gf_ref, wg_ref, wu_ref, cw_ref, cb_ref, wd_ref,
                    gp_ref, wpg_ref, wple_ref, gfin_ref, o_ref, h_ref, acc_ref):
    tm = x_ref.shape[0]
    ext = lambda tile_ref, halo_ref: jnp.concatenate([tile_ref[...], halo_ref[...]], axis=0)
    x1 = _mixer_residual(ext(x_ref, xh_ref), ext(oda_ref, odah_ref), ext(yf_ref, yfh_ref),
                         ext(yb_ref, ybh_ref), ext(sg_ref, sgh_ref), ghg_ref[...], wo_ref)
    xt = x1[0:tm]
    h_ref[...] = _rms(x1, gf_ref[...]).astype(BF16)
    assert sum(fc) == wg_ref.shape[1]
    edges = np.cumsum((0,) + tuple(fc))
    chunks = [slice(int(lo), int(hi)) for lo, hi in zip(edges[:-1], edges[1:])]

    def up_gate(cs):
        return (jnp.dot(h_ref[...], wg_ref[:, cs], preferred_element_type=F32),
                jnp.dot(h_ref[0:tm, :], wu_ref[:, cs], preferred_element_type=F32))

    nxt = up_gate(chunks[0])
    for c, cs in enumerate(chunks):
        a_ext, u = nxt
        if c + 1 < len(chunks):
            nxt = up_gate(chunks[c + 1])
        row = lax.broadcasted_iota(jnp.int32, u.shape, 0)
        a = a_ext[0:tm]
        a_prev = jnp.where(row == 0, a_ext[tm:tm + 1], pltpu.roll(a, 1, axis=0))
        a_next = jnp.where(row == tm - 1, a_ext[tm + 1:tm + 2], pltpu.roll(a, tm - 1, axis=0))
        cv = cb_ref[:, cs] + a_prev * cw_ref[0:1, cs] + a * cw_ref[1:2, cs] + a_next * cw_ref[2:3, cs]
        act = (_gelu(cv) * u).astype(BF16)
        d = jnp.dot(act, wd_ref[cs, :], preferred_element_type=F32)
        if c == 0:
            acc_ref[...] = d
        else:
            acc_ref[...] += d
    x2 = xt + acc_ref[...]
    gate = _sigmoid(jnp.dot(_rms(x2, gp_ref[...]).astype(BF16), wpg_ref[...], preferred_element_type=F32))
    ple = jnp.dot(p_ref[...].astype(BF16), wple_ref[...], preferred_element_type=F32)
    o_ref[...] = _rms(x2 + ple * gate, gfin_ref[...])


def _ffn_ple(streams, p2d, consts, tm, seq, fc):
    T, D = streams[0].shape
    row = lambda w: pl.BlockSpec((tm, w), lambda i: (i, 0))
    halo = lambda w: pl.BlockSpec((None, HALO_ROWS, w), lambda i: (i, 0, 0))
    halos = [_conv_halo(a, tm, seq) for a in streams]
    return pl.pallas_call(
        functools.partial(_ffn_ple_kernel, fc),
        grid=(T // tm,),
        in_specs=[row(a.shape[1]) for a in streams] + [halo(a.shape[1]) for a in streams]
                 + [row(p2d.shape[1])] + [_resident(c.shape) for c in consts],
        out_specs=row(D),
        out_shape=jax.ShapeDtypeStruct((T, D), F32),
        scratch_shapes=[pltpu.VMEM((tm + HALO_ROWS, D), BF16), pltpu.VMEM((tm, D), F32)],
        compiler_params=pltpu.CompilerParams(dimension_semantics=("parallel",),
                                             vmem_limit_bytes=VMEM_LIMIT),
        name="ffn_ple",
    )(*streams, *halos, p2d, *consts)


def _conv_halo(x1, tm, seq):
    T, D = x1.shape
    nt = T // tm
    xr = x1.reshape(nt, tm, D)
    zero = jnp.zeros((1, D), x1.dtype)
    prev = jnp.concatenate([zero, xr[:-1, tm - 1, :]], axis=0)
    nxt = jnp.concatenate([xr[1:, 0, :], zero], axis=0)
    start = (jnp.arange(nt) * tm) % seq
    prev = jnp.where((start == 0)[:, None], 0.0, prev)
    nxt = jnp.where((start + tm == seq)[:, None], 0.0, nxt)
    pad = jnp.zeros((nt, HALO_ROWS - 2, D), x1.dtype)
    return jnp.concatenate([prev[:, None, :], nxt[:, None, :], pad], axis=1)


def _rope_constants():
    half = ROT_DIM // 2
    inv_freq = (np.float32(ROPE_THETA) ** (-np.arange(half, dtype=np.float32) / np.float32(half))).astype(np.float32)
    d = np.arange(LANES) % DA_HEAD_DIM
    place = np.zeros((4 * half, 2 * LANES), np.float32)
    for f in range(half):
        hit = (d < ROT_DIM) & (d % half == f)
        sign = np.where(d < half, -1.0, 1.0)
        place[f, :LANES] = place[half + f, :LANES] = hit
        place[2 * half + f, LANES:] = place[3 * half + f, LANES:] = hit * sign
    return jnp.asarray(inv_freq.reshape(half, 1)), jnp.asarray(place, dtype=BF16)


def kernel(x, p, positions, norm_mix_g, w_in, lam_q1, lam_k1, lam_q2, lam_k2, da_subln_g, hg_lb_gamma, hg_norm_g, w_out, norm_ffn_g, w_ffn_gate, w_ffn_up, ffn_conv_w, ffn_conv_b, w_ffn_down, norm_ple_g, w_ple, w_ple_gate, final_norm_g):
    B, S, D = x.shape
    T = B * S
    depth = w_in.shape[0]
    tm = min(512, S)
    tq = min(512, S)
    tk = min(512, S)
    seg = min(4096, S)
    fc = (768, 768, 768, 512)
    row2 = lambda v: v.reshape(1, -1)
    xc = x.reshape(T, D)
    pos = positions.reshape(T // tm, 1, tm)
    rope_freq, rope_place = _rope_constants()
    for i in range(depth):
        lambda_init = 0.8 - 0.6 * math.exp(-0.3 * i)
        gam = hg_lb_gamma.reshape(-1, hg_lb_gamma.shape[-1])
        qkv, hq, bc, kin, hv, sg = _in_proj(xc, pos, rope_freq, rope_place, row2(norm_mix_g[i]), w_in[i].astype(BF16),
                                            gam, i, tm)
        lam_p = jnp.stack([lam_q1[i], lam_k1[i], lam_q2[i], lam_k2[i]], axis=0)
        o_da = _diff_attn(qkv.reshape(B, S, -1), lam_p, da_subln_g[i].reshape(-1, 1), lambda_init, tq, tk)
        y_f, y_b = _hgrn(_hgrn_needs_exact_diag(hg_lb_gamma, i), hq.reshape(B, S, -1), hv.reshape(B, S, -1),
                         bc.reshape(B, S, -1), kin.reshape(B, S, -1), seg)
        streams = (xc, o_da.reshape(T, -1), y_f.reshape(T, -1), y_b.reshape(T, -1), sg)
        consts = (row2(hg_norm_g[i]), w_out[i].astype(BF16), row2(norm_ffn_g[i]),
                  w_ffn_gate[i].astype(BF16), w_ffn_up[i].astype(BF16), ffn_conv_w[i],
                  row2(ffn_conv_b[i]), w_ffn_down[i].astype(BF16), row2(norm_ple_g[i]),
                  w_ple_gate[i].astype(BF16), w_ple[i].astype(BF16), row2(final_norm_g))
        xc = _ffn_ple(streams, p[i].reshape(T, -1), consts, tm, S, fc)
        assert i == depth - 1, "multi-layer stacks need the final norm split out of ffn_ple"
    return xc.reshape(B, S, D)
```

```python
import functools
import math

import numpy as np
import jax
import jax.numpy as jnp
from jax import lax
from jax.experimental import pallas as pl
from jax.experimental.pallas import tpu as pltpu

F32 = jnp.float32
BF16 = jnp.bfloat16

EPS = 1e-6
ROPE_THETA = 500000.0
DA_HEADS = 4
DA_HEAD_DIM = 64
DA_V_DIM = 128
ROT_DIM = 16
HG_HEADS = 4
HG_DIM = 128
CONV_WIDTH = 3
HG_CHUNK = 64
HG_SUB = 16
HG_CHUNKS_PER_TRIP = 4
HG_MAX_SUB_DECAY = 100.0
LANES = 128
SUBLANES = 8
VMEM_LIMIT = 56 * 1024 * 1024


def _rms(xf, g):
    return xf * lax.rsqrt(jnp.mean(xf * xf, axis=-1, keepdims=True) + EPS) * g


def _sigmoid(z):
    return 0.5 + 0.5 * jnp.tanh(0.5 * z)


def _gelu(x):
    return 0.5 * x * (1.0 + lax.erf(x * (2.0 ** -0.5)))


def _resident(shape):
    return pl.BlockSpec(shape, lambda *_: (0,) * len(shape), pipeline_mode=pl.Buffered(1))


def _chunk_cumsum(x, reverse):
    n = x.shape[0]
    r = lax.broadcasted_iota(jnp.int32, x.shape, 0) & (HG_CHUNK - 1)
    s = 1
    while s < SUBLANES:
        if reverse:
            x = x + jnp.where(r < HG_CHUNK - s, pltpu.roll(x, n - s, axis=0), 0.0)
        else:
            x = x + jnp.where(r >= s, pltpu.roll(x, s, axis=0), 0.0)
        s *= 2
    while s < HG_CHUNK:
        parts = []
        for c0 in range(0, n, HG_CHUNK):
            lo, hi = x[c0:c0 + HG_CHUNK - s], x[c0 + s:c0 + HG_CHUNK]
            parts += [lo + hi, x[c0 + HG_CHUNK - s:c0 + HG_CHUNK]] if reverse else [x[c0:c0 + s], hi + lo]
        x = jnp.concatenate(parts, axis=0)
        s *= 2
    return x


def _in_proj_kernel(layer, n_slots, x_ref, pos_ref, freq_ref, place_ref, g_ref, w_ref, gam_ref,
                    qkv_ref, hq_ref, bc_ref, kin_ref, hv_ref, sg_ref):
    h = _rms(x_ref[...], g_ref[...]).astype(BF16)
    sec = qkv_ref.shape[1] // 3

    proj_all = jnp.dot(h, w_ref[...], preferred_element_type=F32)

    half = ROT_DIM // 2
    ang = freq_ref[...] * pos_ref[...].astype(F32)
    parts = []
    for trig in (jnp.cos(ang), jnp.sin(ang)):
        hi = trig.astype(BF16).astype(F32)
        parts += [hi, trig - hi]
    tab = lax.dot_general(jnp.concatenate(parts, axis=0).astype(BF16), place_ref[...],
                          (((0,), (0,)), ((), ())), preferred_element_type=F32)
    lane_d = lax.broadcasted_iota(jnp.int32, (1, LANES), 1) & (DA_HEAD_DIM - 1)
    cos = tab[:, :LANES] + jnp.where(lane_d >= ROT_DIM, 1.0, 0.0)
    sin = tab[:, LANES:]
    first_half = lane_d < half

    def rotary(t, scale):
        outs = []
        for hh in range(sec // LANES):
            blk = t[:, hh * LANES:(hh + 1) * LANES]
            partner = jnp.where(first_half, pltpu.roll(blk, LANES - half, axis=1),
                                pltpu.roll(blk, half, axis=1))
            rot = blk * cos + partner * sin
            outs.append(rot * scale if scale != 1.0 else rot)
        return jnp.concatenate(outs, axis=1)

    def put_q(t):
        qkv_ref[:, 0:sec] = rotary(t, DA_HEAD_DIM ** -0.5 * math.log2(math.e)).astype(BF16)

    def put_k(t):
        qkv_ref[:, sec:2 * sec] = rotary(t, 1.0).astype(BF16)

    def put_v(t):
        qkv_ref[:, 2 * sec:3 * sec] = t.astype(BF16)

    def put_hq(t):
        hq_ref[...] = t.astype(BF16)

    def put_forget(d, z):
        gam = gam_ref[d * n_slots:(d + 1) * n_slots, :]
        e = jnp.exp(gam - jnp.max(gam, axis=0, keepdims=True))
        lb = jnp.sum(e[0:layer + 1, :], axis=0, keepdims=True) / jnp.sum(e, axis=0, keepdims=True)
        half_span = 0.5 * (1.0 - lb)
        mid = 0.5 * (1.0 + lb)
        for r0 in range(0, z.shape[0], HG_CHUNK):
            w = half_span * jnp.tanh(0.5 * z[r0:r0 + HG_CHUNK])
            bc_ref[r0:r0 + HG_CHUNK, d * sec:(d + 1) * sec] = _chunk_cumsum(jnp.log2(mid + w),
                                                                             reverse=(d == 1))
            kin_ref[r0:r0 + HG_CHUNK, d * sec:(d + 1) * sec] = (half_span - w).astype(BF16)

    def put_hv(t):
        hv_ref[...] = t.astype(BF16)

    def put_gate(t):
        sg_ref[...] = (t * _sigmoid(t)).astype(BF16)

    epilogues = (put_q, put_k, put_v, put_hq, functools.partial(put_forget, 0),
                 functools.partial(put_forget, 1), put_hv, put_gate)
    for i, epilogue in enumerate(epilogues):
        epilogue(proj_all[:, i * sec:(i + 1) * sec])


def _in_proj(x2d, pos, freq, place, g, w_bf, gam, layer, tm):
    T, D = x2d.shape
    sec = w_bf.shape[1] // 8
    n_slots = gam.shape[0] // 2
    row = lambda w: pl.BlockSpec((tm, w), lambda i: (i, 0))
    return pl.pallas_call(
        functools.partial(_in_proj_kernel, layer, n_slots),
        grid=(T // tm,),
        in_specs=[row(D), pl.BlockSpec((None, 1, tm), lambda i: (i, 0, 0)),
                  _resident(freq.shape), _resident(place.shape), _resident(g.shape),
                  _resident(w_bf.shape), _resident(gam.shape)],
        out_specs=[row(3 * sec), row(sec), row(2 * sec), row(2 * sec), row(sec), row(sec)],
        out_shape=[jax.ShapeDtypeStruct((T, 3 * sec), BF16), jax.ShapeDtypeStruct((T, sec), BF16),
                   jax.ShapeDtypeStruct((T, 2 * sec), F32), jax.ShapeDtypeStruct((T, 2 * sec), BF16),
                   jax.ShapeDtypeStruct((T, sec), BF16), jax.ShapeDtypeStruct((T, sec), BF16)],
        compiler_params=pltpu.CompilerParams(dimension_semantics=("parallel",),
                                             vmem_limit_bytes=VMEM_LIMIT),
        name="in_proj",
    )(x2d, pos, freq, place, g, w_bf, gam)


DEN_ROWS = 16
ATTN_SEED_KEYS = 16
ATTN_MAX_JUMP = 100.0
ATTN_TILES_PER_TRIP = 2


def _diff_attn_kernel(lambda_init, tq, tk, q_ref, k_ref, v_ref, lam_ref, g_ref, o_ref,
                      vt_ref, qt_ref, acc_ref, s_ref):
    S = k_ref.shape[0]
    n_kv = S // tk
    dv = v_ref.shape[1]

    feat = lax.broadcasted_iota(jnp.int32, (q_ref.shape[1], tk), 0)

    def transpose_qv(i, carry):
        r = pl.ds(pl.multiple_of(i * tk, tk), tk)
        vt_ref[0:dv, r] = v_ref[r, :].astype(F32).T.astype(BF16)
        q_t = q_ref[r, :].astype(F32).T
        qt_ref[0, :, r] = jnp.where(feat < DA_HEAD_DIM, q_t, 0.0).astype(BF16)
        qt_ref[1, :, r] = jnp.where(feat >= DA_HEAD_DIM, q_t, 0.0).astype(BF16)
        return carry

    lax.fori_loop(0, n_kv, transpose_qv, 0)
    vt_ref[dv:dv + DEN_ROWS, :] = jnp.ones((DEN_ROWS, S), BF16)

    tile_cols = lambda qi: pl.ds(pl.multiple_of(qi * tq, tq), tq)

    def scores(mp, rows, q_cols):
        return jnp.dot(k_ref[rows, :], qt_ref[mp, :, q_cols], preferred_element_type=F32)

    def one_pass_tiles(tiles, after_first_chunk):
        chunk = lambda c: slice(c * tk, (c + 1) * tk)
        cols = [tile_cols(qi) for qi, _ in tiles]
        seeds = [[jnp.max(scores(mp, slice(0, ATTN_SEED_KEYS), qc), axis=0, keepdims=True)
                  for mp in range(2)] for qc in cols]
        steps = [(t, c) for t in range(len(tiles)) for c in range(n_kv)]
        jump = jnp.zeros_like(seeds[0][0])
        s_next = [scores(mp, chunk(0), cols[0]) for mp in range(2)]
        for n, (t, c) in enumerate(steps):
            qi, acc = tiles[t]
            if c == 0:
                m = list(seeds[t])
            for mp in range(2):
                s = s_next[mp]
                m_prev = m[mp]
                m[mp] = jnp.maximum(m_prev, jnp.max(s, axis=0, keepdims=True))
                p = jnp.exp2((s - m_prev).astype(BF16))
                pv = jnp.dot(vt_ref[:, chunk(c)], p, preferred_element_type=F32)
                if n + 1 < len(steps):
                    t_next, c_next = steps[n + 1]
                    s_next[mp] = scores(mp, chunk(c_next), cols[t_next])
                rescale = jnp.exp2(m_prev - m[mp])
                acc[mp] = rescale * (pv if c == 0 else acc[mp] + pv)
                jump = jnp.maximum(jump, m[mp] - m_prev)
            if n == 0:
                after_first_chunk()
            if c == n_kv - 1 and t + 1 < len(tiles):
                finalize(qi, acc)
        return jnp.max(jump)

    def two_pass_tile(qi, acc):
        q_cols = tile_cols(qi)

        def chunk_step(c, m):
            rows = pl.ds(pl.multiple_of(c * tk, tk), tk)
            m_out = []
            for mp in range(2):
                s_ref[...] = scores(mp, rows, q_cols)
                m_new = jnp.maximum(m[mp], jnp.max(s_ref[...], axis=0, keepdims=True))
                p = jnp.exp2((s_ref[...] - m_new).astype(BF16))
                acc[mp] = jnp.exp2(m[mp] - m_new) * acc[mp] + jnp.dot(
                    vt_ref[:, rows], p, preferred_element_type=F32)
                m_out.append(m_new)
            return tuple(m_out)

        for mp in range(2):
            acc[mp] = jnp.zeros(acc.shape[1:], F32)
        start = jnp.full((1, tq), -jnp.inf, F32)
        lax.fori_loop(0, n_kv, chunk_step, (start, start))

    def finalize(qi, acc):
        lam_p = lam_ref[...]
        lam = (jnp.exp(jnp.sum(lam_p[0:1, :] * lam_p[1:2, :], axis=-1, keepdims=True))
               - jnp.exp(jnp.sum(lam_p[2:3, :] * lam_p[3:4, :], axis=-1, keepdims=True)) + lambda_init)
        a0 = acc[0]
        a1 = acc[1]
        o = a0[0:dv] / a0[dv:dv + 1] - lam * (a1[0:dv] / a1[dv:dv + 1])
        o = o * lax.rsqrt(jnp.mean(o * o, axis=0, keepdims=True) + EPS) * g_ref[...]
        o_ref[tile_cols(qi), :] = (o * (1.0 - lambda_init)).T.astype(o_ref.dtype)

    n_slots = acc_ref.shape[1]
    n_trips = S // tq // n_slots
    assert S // tq == n_trips * n_slots

    acc_ref[1, n_slots - 1] = jnp.ones(acc_ref.shape[2:], F32)

    def query_tiles(u, carry):
        bank = lax.rem(u, 2)
        tiles = [(u * n_slots + t, acc_ref.at[bank, t]) for t in range(n_slots)]
        worst_jump = one_pass_tiles(
            tiles, after_first_chunk=lambda: finalize(jnp.maximum(u * n_slots - 1, 0),
                                                      acc_ref.at[1 - bank, n_slots - 1]))

        @pl.when(jnp.logical_not(worst_jump <= ATTN_MAX_JUMP))
        def _():
            for t, (qi, acc) in enumerate(tiles):
                two_pass_tile(qi, acc)
                if t + 1 < n_slots:
                    finalize(qi, acc)

        return carry

    lax.fori_loop(0, n_trips, query_tiles, 0)
    finalize(S // tq - 1, acc_ref.at[(n_trips - 1) % 2, n_slots - 1])


def _diff_attn(qkv3, lam_p, g_col, lambda_init, tq, tk):
    B, S, _ = qkv3.shape
    H = DA_HEADS
    col = lambda off: pl.BlockSpec((None, S, LANES), lambda b, h: (b, 0, off + h))
    return pl.pallas_call(
        functools.partial(_diff_attn_kernel, lambda_init, tq, tk),
        grid=(B, H),
        in_specs=[col(0), col(H), col(2 * H), _resident(lam_p.shape), _resident(g_col.shape)],
        out_specs=col(0),
        out_shape=jax.ShapeDtypeStruct((B, S, H * DA_V_DIM), BF16),
        scratch_shapes=[pltpu.VMEM((DA_V_DIM + DEN_ROWS, S), BF16),
                        pltpu.VMEM((2, LANES, S), BF16),
                        pltpu.VMEM((2, ATTN_TILES_PER_TRIP, 2, DA_V_DIM + DEN_ROWS, tq), F32),
                        pltpu.VMEM((tk, tq), F32)],
        compiler_params=pltpu.CompilerParams(
            dimension_semantics=("parallel", "parallel"), vmem_limit_bytes=VMEM_LIMIT),
        name="diff_attn",
    )(qkv3, qkv3, qkv3, lam_p, g_col)


def _hgrn_scores(q, kin, v, b, reverse, exact_diag):
    C, n_sub = HG_CHUNK, HG_CHUNK // HG_SUB
    blk = lambda j: slice(j * HG_SUB, (j + 1) * HG_SUB)
    bound = [b[j * HG_SUB:j * HG_SUB + 1, :] if reverse else b[(j + 1) * HG_SUB - 1:(j + 1) * HG_SUB, :]
             for j in range(n_sub)]
    b_far = bound[0] if reverse else bound[-1]
    decay = lambda e: jnp.exp2(e).astype(BF16)
    k_til = jnp.concatenate([kin[blk(j)] * decay(bound[j] - b[blk(j)]) for j in range(n_sub)], axis=0)
    own = 0 if exact_diag else 1
    rows = [slice(0, (j + own) * HG_SUB) if reverse else slice((j + 1 - own) * HG_SUB, C)
            for j in range(n_sub)]
    q_til = jnp.concatenate([q[rows[j]] * decay(b[rows[j]] - bound[j]) for j in range(n_sub)
                             if rows[j].stop > rows[j].start], axis=0)
    offs = np.cumsum([0] + [r.stop - r.start for r in rows])
    scores = lax.dot_general(q_til, k_til, (((1,), (1,)), ((), ())),
                             preferred_element_type=F32)
    increment = lax.dot_general(v, kin * decay(b_far - b), (((0,), (0,)), ((), ())),
                                preferred_element_type=F32)
    return scores, [int(o) for o in offs], q * decay(b), jnp.exp2(b_far), increment


def _hgrn_intra(scores, offs, v, reverse, exact_diag, q, kin, b):
    C, n_sub = HG_CHUNK, HG_CHUNK // HG_SUB
    col = lax.broadcasted_iota(jnp.int32, (HG_SUB, C), 1)
    own = 0 if exact_diag else 1
    row_blocks = []
    for i in range(n_sub):
        js = range(i + 1 - own, n_sub) if reverse else range(0, i + own)
        a_i = jnp.zeros((HG_SUB, C), F32)
        for j in js:
            r0 = offs[j] + (i * HG_SUB if reverse else (i - j - 1 + own) * HG_SUB)
            a_i = jnp.where(col // HG_SUB == j, scores[r0:r0 + HG_SUB, :], a_i)
        if exact_diag:
            rs = slice(i * HG_SUB, (i + 1) * HG_SUB)
            q_i, k_i, b_i = q[rs].astype(F32), kin[rs].astype(F32), b[rs]
            for s in range(HG_SUB):
                w = q_i * (k_i[s:s + 1] * jnp.exp2(jnp.minimum(b_i - b_i[s:s + 1], 0.0)))
                a_i = jnp.where(col == i * HG_SUB + s, jnp.sum(w, axis=1, keepdims=True), a_i)
        row_blocks.append(a_i)
    attn = jnp.concatenate(row_blocks, axis=0)
    t_i = lax.broadcasted_iota(jnp.int32, (C, C), 0)
    s_i = lax.broadcasted_iota(jnp.int32, (C, C), 1)
    attn = jnp.where((t_i <= s_i) if reverse else (t_i >= s_i), attn, 0.0)
    return jnp.dot(attn.astype(BF16), v, preferred_element_type=F32)


def _hgrn_kernel(exact_ref, q_f_ref, q_b_ref, v_f_ref, v_b_ref, bc_f_ref, bc_b_ref, kin_f_ref, kin_b_ref,
                 y_f_ref, y_b_ref, st_f_ref, st_b_ref):
    @pl.when(pl.program_id(2) == 0)
    def _():
        st_f_ref[...] = jnp.zeros_like(st_f_ref)
        st_b_ref[...] = jnp.zeros_like(st_b_ref)

    n_chunks = q_f_ref.shape[0] // HG_CHUNK
    directions = ((False, q_f_ref, v_f_ref, bc_f_ref, kin_f_ref, y_f_ref, st_f_ref),
                  (True, q_b_ref, v_b_ref, bc_b_ref, kin_b_ref, y_b_ref, st_b_ref))

    def body(exact_diag, u, carry):
        work, intra = [], []
        for reverse, q_ref, v_ref, bc_ref, kin_ref, y_ref, st_ref in directions:
            for j in range(HG_CHUNKS_PER_TRIP):
                c = u * HG_CHUNKS_PER_TRIP + j
                cc = (n_chunks - 1 - c) if reverse else c
                rs = pl.ds(pl.multiple_of(cc * HG_CHUNK, HG_CHUNK), HG_CHUNK)
                v = v_ref[rs, :]
                stage1 = _hgrn_scores(q_ref[rs, :], kin_ref[rs, :], v, bc_ref[rs, :], reverse, exact_diag)
                work.append((reverse, rs, v, y_ref, st_ref, stage1))
        for (reverse, rs, v, y_ref, st_ref, stage1), (_, q_ref, _, bc_ref, kin_ref, _, _) in zip(
                work, [d for d in directions for _ in range(HG_CHUNKS_PER_TRIP)]):
            intra.append(_hgrn_intra(stage1[0], stage1[1], v, reverse, exact_diag,
                                     q_ref[rs, :], kin_ref[rs, :], bc_ref[rs, :]))
        states = {}
        for (reverse, rs, v, y_ref, st_ref, stage1), o_intra in zip(work, intra):
            _, _, q_state, decay, increment = stage1
            state_t = states.get(reverse)
            if state_t is None:
                state_t = st_ref[...]
            o_inter = lax.dot_general(q_state, state_t.astype(BF16), (((1,), (1,)), ((), ())),
                                      preferred_element_type=F32)
            states[reverse] = decay * state_t + increment
            y_ref[rs, :] = (o_intra + o_inter).astype(y_ref.dtype)
        for reverse, *_, st_ref in directions:
            st_ref[...] = states[reverse]
        return carry

    assert n_chunks % HG_CHUNKS_PER_TRIP == 0

    for exact_diag in (False, True):
        @pl.when((exact_ref[0] != 0) == exact_diag)
        def _():
            lax.fori_loop(0, n_chunks // HG_CHUNKS_PER_TRIP, functools.partial(body, exact_diag), 0)


def _hgrn(exact_flag, hq3, hv3, bc3, kin3, seg):
    B, S, W = hq3.shape
    H = W // HG_DIM
    n_seg = S // seg
    fwd = lambda off: pl.BlockSpec((None, seg, HG_DIM), lambda b, h, s: (b, s, off + h))
    bwd = lambda off: pl.BlockSpec((None, seg, HG_DIM), lambda b, h, s: (b, n_seg - 1 - s, off + h))
    return pl.pallas_call(
        _hgrn_kernel,
        grid=(B, H, n_seg),
        in_specs=[pl.BlockSpec(memory_space=pltpu.SMEM),
                  fwd(0), bwd(0), fwd(0), bwd(0), fwd(0), bwd(H), fwd(0), bwd(H)],
        out_specs=[fwd(0), bwd(0)],
        out_shape=[jax.ShapeDtypeStruct((B, S, W), BF16), jax.ShapeDtypeStruct((B, S, W), BF16)],
        scratch_shapes=[pltpu.VMEM((HG_DIM, HG_DIM), F32), pltpu.VMEM((HG_DIM, HG_DIM), F32)],
        compiler_params=pltpu.CompilerParams(
            dimension_semantics=("parallel", "parallel", "arbitrary"), vmem_limit_bytes=VMEM_LIMIT),
        name="hgrn2",
    )(exact_flag, hq3, hq3, hv3, hv3, bc3, bc3, kin3, kin3)


def _hgrn_needs_exact_diag(lb_gamma, layer):
    e = jax.nn.softmax(lb_gamma.astype(F32), axis=1)
    lb = jnp.sum(e[:, :layer + 1], axis=1)
    worst = (HG_SUB - 1) * jnp.max(-jnp.log2(lb))
    return jnp.logical_not(worst <= HG_MAX_SUB_DECAY).astype(jnp.int32).reshape(1)


def _mixer_residual(x, oda, yf, yb, sg, g_hg, w_ref):
    half = oda.shape[1]
    y = yf.astype(F32) + yb.astype(F32)
    sg = sg.astype(F32)
    heads = []
    for hh in range(y.shape[1] // HG_DIM):
        cs = slice(hh * HG_DIM, (hh + 1) * HG_DIM)
        heads.append(_rms(y[:, cs], g_hg) * sg[:, cs])
    o_hg = jnp.concatenate(heads, axis=1).astype(BF16)
    return (x + jnp.dot(oda, w_ref[0:half, :], preferred_element_type=F32)
            + jnp.dot(o_hg, w_ref[half:, :], preferred_element_type=F32))


HALO_ROWS = 16


def _ffn_ple_kernel(fc, x_ref, oda_ref, yf_ref, yb_ref, sg_ref, xh_ref, odah_ref, yfh_ref, ybh_ref,
                    sgh_ref, p_ref, ghg_ref, wo_ref, gf_ref, wg_ref, wu_ref, cw_ref, cb_ref, wd_ref,
                    gp_ref, wpg_ref, wple_ref, gfin_ref, o_ref, h_ref, acc_ref):
    tm = x_ref.shape[0]
    ext = lambda tile_ref, halo_ref: jnp.concatenate([tile_ref[...], halo_ref[...]], axis=0)
    x1 = _mixer_residual(ext(x_ref, xh_ref), ext(oda_ref, odah_ref), ext(yf_ref, yfh_ref),
                         ext(yb_ref, ybh_ref), ext(sg_ref, sgh_ref), ghg_ref[...], wo_ref)
    xt = x1[0:tm]
    h_ref[...] = _rms(x1, gf_ref[...]).astype(BF16)
    assert sum(fc) == wg_ref.shape[1]
    edges = np.cumsum((0,) + tuple(fc))
    chunks = [slice(int(lo), int(hi)) for lo, hi in zip(edges[:-1], edges[1:])]

    def up_gate(cs):
        return (jnp.dot(h_ref[...], wg_ref[:, cs], preferred_element_type=F32),
                jnp.dot(h_ref[0:tm, :], wu_ref[:, cs], preferred_element_type=F32))

    nxt = up_gate(chunks[0])
    for c, cs in enumerate(chunks):
        a_ext, u = nxt
        if c + 1 < len(chunks):
            nxt = up_gate(chunks[c + 1])
        row = lax.broadcasted_iota(jnp.int32, u.shape, 0)
        a = a_ext[0:tm]
        a_prev = jnp.where(row == 0, a_ext[tm:tm + 1], pltpu.roll(a, 1, axis=0))
        a_next = jnp.where(row == tm - 1, a_ext[tm + 1:tm + 2], pltpu.roll(a, tm - 1, axis=0))
        cv = cb_ref[:, cs] + a_prev * cw_ref[0:1, cs] + a * cw_ref[1:2, cs] + a_next * cw_ref[2:3, cs]
        act = (_gelu(cv) * u).astype(BF16)
        d = jnp.dot(act, wd_ref[cs, :], preferred_element_type=F32)
        if c == 0:
            acc_ref[...] = d
        else:
            acc_ref[...] += d
    x2 = xt + acc_ref[...]
    gate = _sigmoid(jnp.dot(_rms(x2, gp_ref[...]).astype(BF16), wpg_ref[...], preferred_element_type=F32))
    ple = jnp.dot(p_ref[...].astype(BF16), wple_ref[...], preferred_element_type=F32)
    o_ref[...] = _rms(x2 + ple * gate, gfin_ref[...])


def _ffn_ple(streams, p2d, consts, tm, seq, fc):
    T, D = streams[0].shape
    row = lambda w: pl.BlockSpec((tm, w), lambda i: (i, 0))
    halo = lambda w: pl.BlockSpec((None, HALO_ROWS, w), lambda i: (i, 0, 0))
    halos = [_conv_halo(a, tm, seq) for a in streams]
    return pl.pallas_call(
        functools.partial(_ffn_ple_kernel, fc),
        grid=(T // tm,),
        in_specs=[row(a.shape[1]) for a in streams] + [halo(a.shape[1]) for a in streams]
                 + [row(p2d.shape[1])] + [_resident(c.shape) for c in consts],
        out_specs=row(D),
        out_shape=jax.ShapeDtypeStruct((T, D), F32),
        scratch_shapes=[pltpu.VMEM((tm + HALO_ROWS, D), BF16), pltpu.VMEM((tm, D), F32)],
        compiler_params=pltpu.CompilerParams(dimension_semantics=("parallel",),
                                             vmem_limit_bytes=VMEM_LIMIT),
        name="ffn_ple",
    )(*streams, *halos, p2d, *consts)


def _conv_halo(x1, tm, seq):
    T, D = x1.shape
    nt = T // tm
    xr = x1.reshape(nt, tm, D)
    zero = jnp.zeros((1, D), x1.dtype)
    prev = jnp.concatenate([zero, xr[:-1, tm - 1, :]], axis=0)
    nxt = jnp.concatenate([xr[1:, 0, :], zero], axis=0)
    start = (jnp.arange(nt) * tm) % seq
    prev = jnp.where((start == 0)[:, None], 0.0, prev)
    nxt = jnp.where((start + tm == seq)[:, None], 0.0, nxt)
    pad = jnp.zeros((nt, HALO_ROWS - 2, D), x1.dtype)
    return jnp.concatenate([prev[:, None, :], nxt[:, None, :], pad], axis=1)


def _rope_constants():
    half = ROT_DIM // 2
    inv_freq = (np.float32(ROPE_THETA) ** (-np.arange(half, dtype=np.float32) / np.float32(half))).astype(np.float32)
    d = np.arange(LANES) % DA_HEAD_DIM
    place = np.zeros((4 * half, 2 * LANES), np.float32)
    for f in range(half):
        hit = (d < ROT_DIM) & (d % half == f)
        sign = np.where(d < half, -1.0, 1.0)
        place[f, :LANES] = place[half + f, :LANES] = hit
        place[2 * half + f, LANES:] = place[3 * half + f, LANES:] = hit * sign
    return jnp.asarray(inv_freq.reshape(half, 1)), jnp.asarray(place, dtype=BF16)


def kernel(x, p, positions, norm_mix_g, w_in, lam_q1, lam_k1, lam_q2, lam_k2, da_subln_g, hg_lb_gamma, hg_norm_g, w_out, norm_ffn_g, w_ffn_gate, w_ffn_up, ffn_conv_w, ffn_conv_b, w_ffn_down, norm_ple_g, w_ple, w_ple_gate, final_norm_g):
    B, S, D = x.shape
    T = B * S
    depth = w_in.shape[0]
    tm = min(512, S)
    tq = min(512, S)
    tk = min(512, S)
    seg = min(4096, S)
    fc = (768, 768, 768, 512)
    row2 = lambda v: v.reshape(1, -1)
    xc = x.reshape(T, D)
    pos = positions.reshape(T // tm, 1, tm)
    rope_freq, rope_place = _rope_constants()
    for i in range(depth):
        lambda_init = 0.8 - 0.6 * math.exp(-0.3 * i)
        gam = hg_lb_gamma.reshape(-1, hg_lb_gamma.shape[-1])
        qkv, hq, bc, kin, hv, sg = _in_proj(xc, pos, rope_freq, rope_place, row2(norm_mix_g[i]), w_in[i].astype(BF16),
                                            gam, i, tm)
        lam_p = jnp.stack([lam_q1[i], lam_k1[i], lam_q2[i], lam_k2[i]], axis=0)
        o_da = _diff_attn(qkv.reshape(B, S, -1), lam_p, da_subln_g[i].reshape(-1, 1), lambda_init, tq, tk)
        y_f, y_b = _hgrn(_hgrn_needs_exact_diag(hg_lb_gamma, i), hq.reshape(B, S, -1), hv.reshape(B, S, -1),
                         bc.reshape(B, S, -1), kin.reshape(B, S, -1), seg)
        streams = (xc, o_da.reshape(T, -1), y_f.reshape(T, -1), y_b.reshape(T, -1), sg)
        consts = (row2(hg_norm_g[i]), w_out[i].astype(BF16), row2(norm_ffn_g[i]),
                  w_ffn_gate[i].astype(BF16), w_ffn_up[i].astype(BF16), ffn_conv_w[i],
                  row2(ffn_conv_b[i]), w_ffn_down[i].astype(BF16), row2(norm_ple_g[i]),
                  w_ple_gate[i].astype(BF16), w_ple[i].astype(BF16), row2(final_norm_g))
        xc = _ffn_ple(streams, p[i].reshape(T, -1), consts, tm, S, fc)
        assert i == depth - 1, "multi-layer stacks need the final norm split out of ffn_ple"
    return xc.reshape(B, S, D)
```

```python
import functools
import math

import numpy as np
import jax
import jax.numpy as jnp
from jax import lax
from jax.experimental import pallas as pl
from jax.experimental.pallas import tpu as pltpu

F32 = jnp.float32
BF16 = jnp.bfloat16

EPS = 1e-6
ROPE_THETA = 500000.0
DA_HEADS = 4
DA_HEAD_DIM = 64
DA_V_DIM = 128
ROT_DIM = 16
HG_HEADS = 4
HG_DIM = 128
CONV_WIDTH = 3
HG_CHUNK = 64
HG_SUB = 16
HG_CHUNKS_PER_TRIP = 4
HG_MAX_SUB_DECAY = 100.0
LANES = 128
SUBLANES = 8
VMEM_LIMIT = 56 * 1024 * 1024


def _rms(xf, g):
    return xf * lax.rsqrt(jnp.mean(xf * xf, axis=-1, keepdims=True) + EPS) * g


def _sigmoid(z):
    return 0.5 + 0.5 * jnp.tanh(0.5 * z)


def _gelu(x):
    return 0.5 * x * (1.0 + lax.erf(x * (2.0 ** -0.5)))


def _resident(shape):
    return pl.BlockSpec(shape, lambda *_: (0,) * len(shape), pipeline_mode=pl.Buffered(1))


def _chunk_cumsum(x, reverse):
    n = x.shape[0]
    r = lax.broadcasted_iota(jnp.int32, x.shape, 0) & (HG_CHUNK - 1)
    s = 1
    while s < SUBLANES:
        if reverse:
            x = x + jnp.where(r < HG_CHUNK - s, pltpu.roll(x, n - s, axis=0), 0.0)
        else:
            x = x + jnp.where(r >= s, pltpu.roll(x, s, axis=0), 0.0)
        s *= 2
    while s < HG_CHUNK:
        parts = []
        for c0 in range(0, n, HG_CHUNK):
            lo, hi = x[c0:c0 + HG_CHUNK - s], x[c0 + s:c0 + HG_CHUNK]
            parts += [lo + hi, x[c0 + HG_CHUNK - s:c0 + HG_CHUNK]] if reverse else [x[c0:c0 + s], hi + lo]
        x = jnp.concatenate(parts, axis=0)
        s *= 2
    return x


def _in_proj_kernel(layer, n_slots, x_ref, pos_ref, freq_ref, place_ref, g_ref, w_ref, gam_ref,
                    qkv_ref, hq_ref, bc_ref, kin_ref, hv_ref, sg_ref):
    h = _rms(x_ref[...], g_ref[...]).astype(BF16)
    sec = qkv_ref.shape[1] // 3

    proj_all = jnp.dot(h, w_ref[...], preferred_element_type=F32)

    half = ROT_DIM // 2
    ang = freq_ref[...] * pos_ref[...].astype(F32)
    parts = []
    for trig in (jnp.cos(ang), jnp.sin(ang)):
        hi = trig.astype(BF16).astype(F32)
        parts += [hi, trig - hi]
    tab = lax.dot_general(jnp.concatenate(parts, axis=0).astype(BF16), place_ref[...],
                          (((0,), (0,)), ((), ())), preferred_element_type=F32)
    lane_d = lax.broadcasted_iota(jnp.int32, (1, LANES), 1) & (DA_HEAD_DIM - 1)
    cos = tab[:, :LANES] + jnp.where(lane_d >= ROT_DIM, 1.0, 0.0)
    sin = tab[:, LANES:]
    first_half = lane_d < half

    def rotary(t, scale):
        outs = []
        for hh in range(sec // LANES):
            blk = t[:, hh * LANES:(hh + 1) * LANES]
            partner = jnp.where(first_half, pltpu.roll(blk, LANES - half, axis=1),
                                pltpu.roll(blk, half, axis=1))
            rot = blk * cos + partner * sin
            outs.append(rot * scale if scale != 1.0 else rot)
        return jnp.concatenate(outs, axis=1)

    def put_q(t):
        qkv_ref[:, 0:sec] = rotary(t, DA_HEAD_DIM ** -0.5 * math.log2(math.e)).astype(BF16)

    def put_k(t):
        qkv_ref[:, sec:2 * sec] = rotary(t, 1.0).astype(BF16)

    def put_v(t):
        qkv_ref[:, 2 * sec:3 * sec] = t.astype(BF16)

    n_heads = sec // LANES

    def put_heads(ref, first, rows, t):
        for hh in range(n_heads):
            ref[first + hh, rows, :] = t[:, hh * LANES:(hh + 1) * LANES]

    def put_hq(t):
        put_heads(hq_ref, 0, slice(None), t.astype(BF16))

    def put_forget(d, z):
        gam = gam_ref[d * n_slots:(d + 1) * n_slots, :]
        e = jnp.exp(gam - jnp.max(gam, axis=0, keepdims=True))
        lb = jnp.sum(e[0:layer + 1, :], axis=0, keepdims=True) / jnp.sum(e, axis=0, keepdims=True)
        half_span = 0.5 * (1.0 - lb)
        mid = 0.5 * (1.0 + lb)
        for r0 in range(0, z.shape[0], HG_CHUNK):
            w = half_span * jnp.tanh(0.5 * z[r0:r0 + HG_CHUNK])
            rows = slice(r0, r0 + HG_CHUNK)
            put_heads(bc_ref, d * n_heads, rows, _chunk_cumsum(jnp.log2(mid + w), reverse=(d == 1)))
            put_heads(kin_ref, d * n_heads, rows, (half_span - w).astype(BF16))

    def put_hv(t):
        put_heads(hv_ref, 0, slice(None), t.astype(BF16))

    def put_gate(t):
        sg_ref[...] = (t * _sigmoid(t)).astype(BF16)

    epilogues = (put_q, put_k, put_v, put_hq, functools.partial(put_forget, 0),
                 functools.partial(put_forget, 1), put_hv, put_gate)
    for i, epilogue in enumerate(epilogues):
        epilogue(proj_all[:, i * sec:(i + 1) * sec])


def _in_proj(x2d, pos, freq, place, g, w_bf, gam, layer, tm, seq):
    T, D = x2d.shape
    sec = w_bf.shape[1] // 8
    n_slots = gam.shape[0] // 2
    H = sec // LANES
    tiles_per_seq = seq // tm
    row = lambda w: pl.BlockSpec((tm, w), lambda i: (i, 0))
    heads = lambda n: pl.BlockSpec((None, n, tm, LANES), lambda i: (i // tiles_per_seq, 0, i % tiles_per_seq, 0))
    head_major = lambda n, dt: jax.ShapeDtypeStruct((T // seq, n, seq, LANES), dt)
    return pl.pallas_call(
        functools.partial(_in_proj_kernel, layer, n_slots),
        grid=(T // tm,),
        in_specs=[row(D), pl.BlockSpec((None, 1, tm), lambda i: (i, 0, 0)),
                  _resident(freq.shape), _resident(place.shape), _resident(g.shape),
                  _resident(w_bf.shape), _resident(gam.shape)],
        out_specs=[row(3 * sec), heads(H), heads(2 * H), heads(2 * H), heads(H), row(sec)],
        out_shape=[jax.ShapeDtypeStruct((T, 3 * sec), BF16), head_major(H, BF16),
                   head_major(2 * H, F32), head_major(2 * H, BF16),
                   head_major(H, BF16), jax.ShapeDtypeStruct((T, sec), BF16)],
        compiler_params=pltpu.CompilerParams(dimension_semantics=("parallel",),
                                             vmem_limit_bytes=VMEM_LIMIT),
        name="in_proj",
    )(x2d, pos, freq, place, g, w_bf, gam)


DEN_ROWS = 16
ATTN_SEED_KEYS = 16
ATTN_MAX_JUMP = 100.0
ATTN_TILES_PER_TRIP = 2


def _diff_attn_kernel(lambda_init, tq, tk, q_ref, k_ref, v_ref, lam_ref, g_ref, o_ref,
                      vt_ref, qt_ref, acc_ref, s_ref):
    S = k_ref.shape[0]
    n_kv = S // tk
    dv = v_ref.shape[1]

    feat = lax.broadcasted_iota(jnp.int32, (q_ref.shape[1], tk), 0)

    def transpose_qv(i, carry):
        r = pl.ds(pl.multiple_of(i * tk, tk), tk)
        vt_ref[0:dv, r] = v_ref[r, :].T
        q_t = q_ref[r, :].T
        qt_ref[0, :, r] = jnp.where(feat < DA_HEAD_DIM, q_t, jnp.zeros_like(q_t))
        qt_ref[1, :, r] = jnp.where(feat >= DA_HEAD_DIM, q_t, jnp.zeros_like(q_t))
        return carry

    lax.fori_loop(0, n_kv, transpose_qv, 0)
    vt_ref[dv:dv + DEN_ROWS, :] = jnp.ones((DEN_ROWS, S), BF16)

    tile_cols = lambda qi: pl.ds(pl.multiple_of(qi * tq, tq), tq)

    def scores(mp, rows, q_cols):
        return jnp.dot(k_ref[rows, :], qt_ref[mp, :, q_cols], preferred_element_type=F32)

    def one_pass_tiles(tiles, after_first_chunk):
        chunk = lambda c: slice(c * tk, (c + 1) * tk)
        cols = [tile_cols(qi) for qi, _ in tiles]
        seeds = [[jnp.max(scores(mp, slice(0, ATTN_SEED_KEYS), qc), axis=0, keepdims=True)
                  for mp in range(2)] for qc in cols]
        steps = [(t, c) for t in range(len(tiles)) for c in range(n_kv)]
        jump = jnp.zeros_like(seeds[0][0])
        s_next = [scores(mp, chunk(0), cols[0]) for mp in range(2)]
        for n, (t, c) in enumerate(steps):
            qi, acc = tiles[t]
            if c == 0:
                m = list(seeds[t])
            for mp in range(2):
                s = s_next[mp]
                m_prev = m[mp]
                m[mp] = jnp.maximum(m_prev, jnp.max(s, axis=0, keepdims=True))
                p = jnp.exp2((s - m_prev).astype(BF16))
                pv = jnp.dot(vt_ref[:, chunk(c)], p, preferred_element_type=F32)
                if n + 1 < len(steps):
                    t_next, c_next = steps[n + 1]
                    s_next[mp] = scores(mp, chunk(c_next), cols[t_next])
                rescale = jnp.exp2(m_prev - m[mp])
                acc[mp] = rescale * (pv if c == 0 else acc[mp] + pv)
                jump = jnp.maximum(jump, m[mp] - m_prev)
            if n == 0:
                after_first_chunk()
            if c == n_kv - 1 and t + 1 < len(tiles):
                finalize(qi, acc)
        return jnp.max(jump)

    def two_pass_tile(qi, acc):
        q_cols = tile_cols(qi)

        def chunk_step(c, m):
            rows = pl.ds(pl.multiple_of(c * tk, tk), tk)
            m_out = []
            for mp in range(2):
                s_ref[...] = scores(mp, rows, q_cols)
                m_new = jnp.maximum(m[mp], jnp.max(s_ref[...], axis=0, keepdims=True))
                p = jnp.exp2((s_ref[...] - m_new).astype(BF16))
                acc[mp] = jnp.exp2(m[mp] - m_new) * acc[mp] + jnp.dot(
                    vt_ref[:, rows], p, preferred_element_type=F32)
                m_out.append(m_new)
            return tuple(m_out)

        for mp in range(2):
            acc[mp] = jnp.zeros(acc.shape[1:], F32)
        start = jnp.full((1, tq), -jnp.inf, F32)
        lax.fori_loop(0, n_kv, chunk_step, (start, start))

    def finalize(qi, acc):
        lam_p = lam_ref[...]
        lam = (jnp.exp(jnp.sum(lam_p[0:1, :] * lam_p[1:2, :], axis=-1, keepdims=True))
               - jnp.exp(jnp.sum(lam_p[2:3, :] * lam_p[3:4, :], axis=-1, keepdims=True)) + lambda_init)
        a0 = acc[0]
        a1 = acc[1]
        o = a0[0:dv] / a0[dv:dv + 1] - lam * (a1[0:dv] / a1[dv:dv + 1])
        o = o * lax.rsqrt(jnp.mean(o * o, axis=0, keepdims=True) + EPS) * g_ref[...]
        o_ref[tile_cols(qi), :] = (o * (1.0 - lambda_init)).T.astype(o_ref.dtype)

    n_slots = acc_ref.shape[1]
    n_trips = S // tq // n_slots
    assert S // tq == n_trips * n_slots

    acc_ref[1, n_slots - 1] = jnp.ones(acc_ref.shape[2:], F32)

    def query_tiles(u, carry):
        bank = lax.rem(u, 2)
        tiles = [(u * n_slots + t, acc_ref.at[bank, t]) for t in range(n_slots)]
        worst_jump = one_pass_tiles(
            tiles, after_first_chunk=lambda: finalize(jnp.maximum(u * n_slots - 1, 0),
                                                      acc_ref.at[1 - bank, n_slots - 1]))

        @pl.when(jnp.logical_not(worst_jump <= ATTN_MAX_JUMP))
        def _():
            for t, (qi, acc) in enumerate(tiles):
                two_pass_tile(qi, acc)
                if t + 1 < n_slots:
                    finalize(qi, acc)

        return carry

    lax.fori_loop(0, n_trips, query_tiles, 0)
    finalize(S // tq - 1, acc_ref.at[(n_trips - 1) % 2, n_slots - 1])


def _diff_attn(qkv3, lam_p, g_col, lambda_init, tq, tk):
    B, S, _ = qkv3.shape
    H = DA_HEADS
    col = lambda off: pl.BlockSpec((None, S, LANES), lambda b, h: (b, 0, off + h))
    return pl.pallas_call(
        functools.partial(_diff_attn_kernel, lambda_init, tq, tk),
        grid=(B, H),
        in_specs=[col(0), col(H), col(2 * H), _resident(lam_p.shape), _resident(g_col.shape)],
        out_specs=col(0),
        out_shape=jax.ShapeDtypeStruct((B, S, H * DA_V_DIM), BF16),
        scratch_shapes=[pltpu.VMEM((DA_V_DIM + DEN_ROWS, S), BF16),
                        pltpu.VMEM((2, LANES, S), BF16),
                        pltpu.VMEM((2, ATTN_TILES_PER_TRIP, 2, DA_V_DIM + DEN_ROWS, tq), F32),
                        pltpu.VMEM((tk, tq), F32)],
        compiler_params=pltpu.CompilerParams(
            dimension_semantics=("parallel", "parallel"), vmem_limit_bytes=VMEM_LIMIT),
        name="diff_attn",
    )(qkv3, qkv3, qkv3, lam_p, g_col)


def _hgrn_scores(q, kin, v, b, reverse, exact_diag):
    C, n_sub = HG_CHUNK, HG_CHUNK // HG_SUB
    blk = lambda j: slice(j * HG_SUB, (j + 1) * HG_SUB)
    bound = [b[j * HG_SUB:j * HG_SUB + 1, :] if reverse else b[(j + 1) * HG_SUB - 1:(j + 1) * HG_SUB, :]
             for j in range(n_sub)]
    b_far = bound[0] if reverse else bound[-1]
    decay = lambda e: jnp.exp2(e).astype(BF16)
    k_til = jnp.concatenate([kin[blk(j)] * decay(bound[j] - b[blk(j)]) for j in range(n_sub)], axis=0)
    own = 0 if exact_diag else 1
    rows = [slice(0, (j + own) * HG_SUB) if reverse else slice((j + 1 - own) * HG_SUB, C)
            for j in range(n_sub)]
    q_til = jnp.concatenate([q[rows[j]] * decay(b[rows[j]] - bound[j]) for j in range(n_sub)
                             if rows[j].stop > rows[j].start], axis=0)
    offs = np.cumsum([0] + [r.stop - r.start for r in rows])
    scores = lax.dot_general(q_til, k_til, (((1,), (1,)), ((), ())),
                             preferred_element_type=F32)
    increment = lax.dot_general(v, kin * decay(b_far - b), (((0,), (0,)), ((), ())),
                                preferred_element_type=F32)
    return scores, [int(o) for o in offs], q * decay(b), jnp.exp2(b_far), increment


def _hgrn_intra(scores, offs, v, reverse, exact_diag, q, kin, b):
    C, n_sub = HG_CHUNK, HG_CHUNK // HG_SUB
    col = lax.broadcasted_iota(jnp.int32, (HG_SUB, C), 1)
    own = 0 if exact_diag else 1
    row_blocks = []
    for i in range(n_sub):
        js = range(i + 1 - own, n_sub) if reverse else range(0, i + own)
        a_i = jnp.zeros((HG_SUB, C), F32)
        for j in js:
            r0 = offs[j] + (i * HG_SUB if reverse else (i - j - 1 + own) * HG_SUB)
            a_i = jnp.where(col // HG_SUB == j, scores[r0:r0 + HG_SUB, :], a_i)
        if exact_diag:
            rs = slice(i * HG_SUB, (i + 1) * HG_SUB)
            q_i, k_i, b_i = q[rs].astype(F32), kin[rs].astype(F32), b[rs]
            for s in range(HG_SUB):
                w = q_i * (k_i[s:s + 1] * jnp.exp2(jnp.minimum(b_i - b_i[s:s + 1], 0.0)))
                a_i = jnp.where(col == i * HG_SUB + s, jnp.sum(w, axis=1, keepdims=True), a_i)
        row_blocks.append(a_i)
    attn = jnp.concatenate(row_blocks, axis=0)
    t_i = lax.broadcasted_iota(jnp.int32, (C, C), 0)
    s_i = lax.broadcasted_iota(jnp.int32, (C, C), 1)
    attn = jnp.where((t_i <= s_i) if reverse else (t_i >= s_i), attn, 0.0)
    return jnp.dot(attn.astype(BF16), v, preferred_element_type=F32)


def _hgrn_kernel(exact_ref, q_f_ref, q_b_ref, v_f_ref, v_b_ref, bc_f_ref, bc_b_ref, kin_f_ref, kin_b_ref,
                 y_f_ref, y_b_ref, st_f_ref, st_b_ref):
    @pl.when(pl.program_id(2) == 0)
    def _():
        st_f_ref[...] = jnp.zeros_like(st_f_ref)
        st_b_ref[...] = jnp.zeros_like(st_b_ref)

    n_chunks = q_f_ref.shape[0] // HG_CHUNK
    directions = ((False, q_f_ref, v_f_ref, bc_f_ref, kin_f_ref, y_f_ref, st_f_ref),
                  (True, q_b_ref, v_b_ref, bc_b_ref, kin_b_ref, y_b_ref, st_b_ref))

    def body(exact_diag, u, carry):
        work, intra = [], []
        for reverse, q_ref, v_ref, bc_ref, kin_ref, y_ref, st_ref in directions:
            for j in range(HG_CHUNKS_PER_TRIP):
                c = u * HG_CHUNKS_PER_TRIP + j
                cc = (n_chunks - 1 - c) if reverse else c
                rs = pl.ds(pl.multiple_of(cc * HG_CHUNK, HG_CHUNK), HG_CHUNK)
                v = v_ref[rs, :]
                stage1 = _hgrn_scores(q_ref[rs, :], kin_ref[rs, :], v, bc_ref[rs, :], reverse, exact_diag)
                work.append((reverse, rs, v, y_ref, st_ref, stage1))
        for (reverse, rs, v, y_ref, st_ref, stage1), (_, q_ref, _, bc_ref, kin_ref, _, _) in zip(
                work, [d for d in directions for _ in range(HG_CHUNKS_PER_TRIP)]):
            intra.append(_hgrn_intra(stage1[0], stage1[1], v, reverse, exact_diag,
                                     q_ref[rs, :], kin_ref[rs, :], bc_ref[rs, :]))
        states = {}
        for (reverse, rs, v, y_ref, st_ref, stage1), o_intra in zip(work, intra):
            _, _, q_state, decay, increment = stage1
            state_t = states.get(reverse)
            if state_t is None:
                state_t = st_ref[...]
            o_inter = lax.dot_general(q_state, state_t.astype(BF16), (((1,), (1,)), ((), ())),
                                      preferred_element_type=F32)
            states[reverse] = decay * state_t + increment
            y_ref[rs, :] = (o_intra + o_inter).astype(y_ref.dtype)
        for reverse, *_, st_ref in directions:
            st_ref[...] = states[reverse]
        return carry

    assert n_chunks % HG_CHUNKS_PER_TRIP == 0

    for exact_diag in (False, True):
        @pl.when((exact_ref[0] != 0) == exact_diag)
        def _():
            lax.fori_loop(0, n_chunks // HG_CHUNKS_PER_TRIP, functools.partial(body, exact_diag), 0)


def _hgrn(exact_flag, hq4, hv4, bc4, kin4, seg):
    B, H, S, _ = hq4.shape
    n_seg = S // seg
    fwd = lambda off: pl.BlockSpec((None, None, seg, HG_DIM), lambda b, h, s: (b, off + h, s, 0))
    bwd = lambda off: pl.BlockSpec((None, None, seg, HG_DIM), lambda b, h, s: (b, off + h, n_seg - 1 - s, 0))
    return pl.pallas_call(
        _hgrn_kernel,
        grid=(B, H, n_seg),
        in_specs=[pl.BlockSpec(memory_space=pltpu.SMEM),
                  fwd(0), bwd(0), fwd(0), bwd(0), fwd(0), bwd(H), fwd(0), bwd(H)],
        out_specs=[pl.BlockSpec((None, seg, HG_DIM), lambda b, h, s: (b, s, h)),
                   pl.BlockSpec((None, seg, HG_DIM), lambda b, h, s: (b, n_seg - 1 - s, h))],
        out_shape=[jax.ShapeDtypeStruct((B, S, H * HG_DIM), BF16),
                   jax.ShapeDtypeStruct((B, S, H * HG_DIM), BF16)],
        scratch_shapes=[pltpu.VMEM((HG_DIM, HG_DIM), F32), pltpu.VMEM((HG_DIM, HG_DIM), F32)],
        compiler_params=pltpu.CompilerParams(
            dimension_semantics=("parallel", "parallel", "arbitrary"), vmem_limit_bytes=VMEM_LIMIT),
        name="hgrn2",
    )(exact_flag, hq4, hq4, hv4, hv4, bc4, bc4, kin4, kin4)


def _hgrn_needs_exact_diag(lb_gamma, layer):
    e = jax.nn.softmax(lb_gamma.astype(F32), axis=1)
    lb = jnp.sum(e[:, :layer + 1], axis=1)
    worst = (HG_SUB - 1) * jnp.max(-jnp.log2(lb))
    return jnp.logical_not(worst <= HG_MAX_SUB_DECAY).astype(jnp.int32).reshape(1)


def _mixer_residual(x, oda, yf, yb, sg, g_hg, w_ref):
    half = oda.shape[1]
    y = yf.astype(F32) + yb.astype(F32)
    sg = sg.astype(F32)
    heads = []
    for hh in range(y.shape[1] // HG_DIM):
        cs = slice(hh * HG_DIM, (hh + 1) * HG_DIM)
        heads.append(_rms(y[:, cs], g_hg) * sg[:, cs])
    o_hg = jnp.concatenate(heads, axis=1).astype(BF16)
    return (x + jnp.dot(oda, w_ref[0:half, :], preferred_element_type=F32)
            + jnp.dot(o_hg, w_ref[half:, :], preferred_element_type=F32))


HALO_ROWS = 16


def _ffn_ple_kernel(fc, x_ref, oda_ref, yf_ref, yb_ref, sg_ref, xh_ref, odah_ref, yfh_ref, ybh_ref,
                    sgh_ref, p_ref, ghg_ref, wo_ref, gf_ref, wg_ref, wu_ref, cw_ref, cb_ref, wd_ref,
                    gp_ref, wpg_ref, wple_ref, gfin_ref, o_ref, h_ref, acc_ref):
    tm = x_ref.shape[0]
    ext = lambda tile_ref, halo_ref: jnp.concatenate([tile_ref[...], halo_ref[...]], axis=0)
    x1 = _mixer_residual(ext(x_ref, xh_ref), ext(oda_ref, odah_ref), ext(yf_ref, yfh_ref),
                         ext(yb_ref, ybh_ref), ext(sg_ref, sgh_ref), ghg_ref[...], wo_ref)
    xt = x1[0:tm]
    h_ref[...] = _rms(x1, gf_ref[...]).astype(BF16)
    assert sum(fc) == wg_ref.shape[1]
    edges = np.cumsum((0,) + tuple(fc))
    chunks = [slice(int(lo), int(hi)) for lo, hi in zip(edges[:-1], edges[1:])]

    def up_gate(cs):
        return (jnp.dot(h_ref[...], wg_ref[:, cs], preferred_element_type=F32),
                jnp.dot(h_ref[0:tm, :], wu_ref[:, cs], preferred_element_type=F32))

    nxt = up_gate(chunks[0])
    for c, cs in enumerate(chunks):
        a_ext, u = nxt
        if c + 1 < len(chunks):
            nxt = up_gate(chunks[c + 1])
        row = lax.broadcasted_iota(jnp.int32, u.shape, 0)
        a = a_ext[0:tm]
        a_prev = jnp.where(row == 0, a_ext[tm:tm + 1], pltpu.roll(a, 1, axis=0))
        a_next = jnp.where(row == tm - 1, a_ext[tm + 1:tm + 2], pltpu.roll(a, tm - 1, axis=0))
        cv = cb_ref[:, cs] + a_prev * cw_ref[0:1, cs] + a * cw_ref[1:2, cs] + a_next * cw_ref[2:3, cs]
        act = (_gelu(cv) * u).astype(BF16)
        d = jnp.dot(act, wd_ref[cs, :], preferred_element_type=F32)
        if c == 0:
            acc_ref[...] = d
        else:
            acc_ref[...] += d
    x2 = xt + acc_ref[...]
    gate = _sigmoid(jnp.dot(_rms(x2, gp_ref[...]).astype(BF16), wpg_ref[...], preferred_element_type=F32))
    ple = jnp.dot(p_ref[...].astype(BF16), wple_ref[...], preferred_element_type=F32)
    o_ref[...] = _rms(x2 + ple * gate, gfin_ref[...])


def _ffn_ple(streams, p2d, consts, tm, seq, fc):
    T, D = streams[0].shape
    row = lambda w: pl.BlockSpec((tm, w), lambda i: (i, 0))
    halo = lambda w: pl.BlockSpec((None, HALO_ROWS, w), lambda i: (i, 0, 0))
    halos = [_conv_halo(a, tm, seq) for a in streams]
    return pl.pallas_call(
        functools.partial(_ffn_ple_kernel, fc),
        grid=(T // tm,),
        in_specs=[row(a.shape[1]) for a in streams] + [halo(a.shape[1]) for a in streams]
                 + [row(p2d.shape[1])] + [_resident(c.shape) for c in consts],
        out_specs=row(D),
        out_shape=jax.ShapeDtypeStruct((T, D), F32),
        scratch_shapes=[pltpu.VMEM((tm + HALO_ROWS, D), BF16), pltpu.VMEM((tm, D), F32)],
        compiler_params=pltpu.CompilerParams(dimension_semantics=("parallel",),
                                             vmem_limit_bytes=VMEM_LIMIT),
        name="ffn_ple",
    )(*streams, *halos, p2d, *consts)


def _conv_halo(x1, tm, seq):
    T, D = x1.shape
    nt = T // tm
    xr = x1.reshape(nt, tm, D)
    zero = jnp.zeros((1, D), x1.dtype)
    prev = jnp.concatenate([zero, xr[:-1, tm - 1, :]], axis=0)
    nxt = jnp.concatenate([xr[1:, 0, :], zero], axis=0)
    start = (jnp.arange(nt) * tm) % seq
    prev = jnp.where((start == 0)[:, None], 0.0, prev)
    nxt = jnp.where((start + tm == seq)[:, None], 0.0, nxt)
    pad = jnp.zeros((nt, HALO_ROWS - 2, D), x1.dtype)
    return jnp.concatenate([prev[:, None, :], nxt[:, None, :], pad], axis=1)


def _rope_constants():
    half = ROT_DIM // 2
    inv_freq = (np.float32(ROPE_THETA) ** (-np.arange(half, dtype=np.float32) / np.float32(half))).astype(np.float32)
    d = np.arange(LANES) % DA_HEAD_DIM
    place = np.zeros((4 * half, 2 * LANES), np.float32)
    for f in range(half):
        hit = (d < ROT_DIM) & (d % half == f)
        sign = np.where(d < half, -1.0, 1.0)
        place[f, :LANES] = place[half + f, :LANES] = hit
        place[2 * half + f, LANES:] = place[3 * half + f, LANES:] = hit * sign
    return jnp.asarray(inv_freq.reshape(half, 1)), jnp.asarray(place, dtype=BF16)


def kernel(x, p, positions, norm_mix_g, w_in, lam_q1, lam_k1, lam_q2, lam_k2, da_subln_g, hg_lb_gamma, hg_norm_g, w_out, norm_ffn_g, w_ffn_gate, w_ffn_up, ffn_conv_w, ffn_conv_b, w_ffn_down, norm_ple_g, w_ple, w_ple_gate, final_norm_g):
    B, S, D = x.shape
    T = B * S
    depth = w_in.shape[0]
    tm = min(512, S)
    tq = min(512, S)
    tk = min(512, S)
    seg = min(4096, S)
    fc = (768, 768, 768, 512)
    row2 = lambda v: v.reshape(1, -1)
    xc = x.reshape(T, D)
    pos = positions.reshape(T // tm, 1, tm)
    rope_freq, rope_place = _rope_constants()
    for i in range(depth):
        lambda_init = 0.8 - 0.6 * math.exp(-0.3 * i)
        gam = hg_lb_gamma.reshape(-1, hg_lb_gamma.shape[-1])
        qkv, hq, bc, kin, hv, sg = _in_proj(xc, pos, rope_freq, rope_place, row2(norm_mix_g[i]), w_in[i].astype(BF16),
                                            gam, i, tm, S)
        lam_p = jnp.stack([lam_q1[i], lam_k1[i], lam_q2[i], lam_k2[i]], axis=0)
        o_da = _diff_attn(qkv.reshape(B, S, -1), lam_p, da_subln_g[i].reshape(-1, 1), lambda_init, tq, tk)
        y_f, y_b = _hgrn(_hgrn_needs_exact_diag(hg_lb_gamma, i), hq, hv, bc, kin, seg)
        streams = (xc, o_da.reshape(T, -1), y_f.reshape(T, -1), y_b.reshape(T, -1), sg)
        consts = (row2(hg_norm_g[i]), w_out[i].astype(BF16), row2(norm_ffn_g[i]),
                  w_ffn_gate[i].astype(BF16), w_ffn_up[i].astype(BF16), ffn_conv_w[i],
                  row2(ffn_conv_b[i]), w_ffn_down[i].astype(BF16), row2(norm_ple_g[i]),
                  w_ple_gate[i].astype(BF16), w_ple[i].astype(BF16), row2(final_norm_g))
        xc = _ffn_ple(streams, p[i].reshape(T, -1), consts, tm, S, fc)
        assert i == depth - 1, "multi-layer stacks need the final norm split out of ffn_ple"
    return xc.reshape(B, S, D)
```

```python
import functools
import math

import numpy as np
import jax
import jax.numpy as jnp
from jax import lax
from jax.experimental import pallas as pl
from jax.experimental.pallas import tpu as pltpu

F32 = jnp.float32
BF16 = jnp.bfloat16

EPS = 1e-6
ROPE_THETA = 500000.0
DA_HEADS = 4
DA_HEAD_DIM = 64
DA_V_DIM = 128
ROT_DIM = 16
HG_HEADS = 4
HG_DIM = 128
CONV_WIDTH = 3
HG_CHUNK = 64
HG_SUB = 16
HG_CHUNKS_PER_TRIP = 4
HG_MAX_SUB_DECAY = 100.0
LANES = 128
SUBLANES = 8
VMEM_LIMIT = 56 * 1024 * 1024


def _rms(xf, g):
    return xf * lax.rsqrt(jnp.mean(xf * xf, axis=-1, keepdims=True) + EPS) * g


def _sigmoid(z):
    return 0.5 + 0.5 * jnp.tanh(0.5 * z)


def _gelu(x):
    return 0.5 * x * (1.0 + lax.erf(x * (2.0 ** -0.5)))


def _resident(shape):
    return pl.BlockSpec(shape, lambda *_: (0,) * len(shape), pipeline_mode=pl.Buffered(1))


def _chunk_cumsum(x, reverse):
    n = x.shape[0]
    r = lax.broadcasted_iota(jnp.int32, x.shape, 0) & (HG_CHUNK - 1)
    s = 1
    while s < SUBLANES:
        if reverse:
            x = x + jnp.where(r < HG_CHUNK - s, pltpu.roll(x, n - s, axis=0), 0.0)
        else:
            x = x + jnp.where(r >= s, pltpu.roll(x, s, axis=0), 0.0)
        s *= 2
    while s < HG_CHUNK:
        parts = []
        for c0 in range(0, n, HG_CHUNK):
            lo, hi = x[c0:c0 + HG_CHUNK - s], x[c0 + s:c0 + HG_CHUNK]
            parts += [lo + hi, x[c0 + HG_CHUNK - s:c0 + HG_CHUNK]] if reverse else [x[c0:c0 + s], hi + lo]
        x = jnp.concatenate(parts, axis=0)
        s *= 2
    return x


def _in_proj_kernel(layer, n_slots, x_ref, pos_ref, freq_ref, place_ref, g_ref, w_ref, gam_ref,
                    qkv_ref, hq_ref, bc_ref, kin_ref, hv_ref, sg_ref):
    h = _rms(x_ref[...], g_ref[...]).astype(BF16)
    sec = qkv_ref.shape[1] // 3

    proj_all = jnp.dot(h, w_ref[...], preferred_element_type=F32)

    half = ROT_DIM // 2
    ang = freq_ref[...] * pos_ref[...].astype(F32)
    parts = []
    for trig in (jnp.cos(ang), jnp.sin(ang)):
        hi = trig.astype(BF16).astype(F32)
        parts += [hi, trig - hi]
    tab = lax.dot_general(jnp.concatenate(parts, axis=0).astype(BF16), place_ref[...],
                          (((0,), (0,)), ((), ())), preferred_element_type=F32)
    lane_d = lax.broadcasted_iota(jnp.int32, (1, LANES), 1) & (DA_HEAD_DIM - 1)
    cos = tab[:, :LANES] + jnp.where(lane_d >= ROT_DIM, 1.0, 0.0)
    sin = tab[:, LANES:]
    first_half = lane_d < half

    def rotary(t, scale):
        outs = []
        for hh in range(sec // LANES):
            blk = t[:, hh * LANES:(hh + 1) * LANES]
            partner = jnp.where(first_half, pltpu.roll(blk, LANES - half, axis=1),
                                pltpu.roll(blk, half, axis=1))
            rot = blk * cos + partner * sin
            outs.append(rot * scale if scale != 1.0 else rot)
        return jnp.concatenate(outs, axis=1)

    def put_q(t):
        qkv_ref[:, 0:sec] = rotary(t, DA_HEAD_DIM ** -0.5 * math.log2(math.e)).astype(BF16)

    def put_k(t):
        qkv_ref[:, sec:2 * sec] = rotary(t, 1.0).astype(BF16)

    def put_v(t):
        qkv_ref[:, 2 * sec:3 * sec] = t.astype(BF16)

    def put_hq(t):
        hq_ref[...] = t.astype(BF16)

    def put_forget(d, z):
        gam = gam_ref[d * n_slots:(d + 1) * n_slots, :]
        e = jnp.exp(gam - jnp.max(gam, axis=0, keepdims=True))
        lb = jnp.sum(e[0:layer + 1, :], axis=0, keepdims=True) / jnp.sum(e, axis=0, keepdims=True)
        half_span = 0.5 * (1.0 - lb)
        mid = 0.5 * (1.0 + lb)
        for r0 in range(0, z.shape[0], HG_CHUNK):
            w = half_span * jnp.tanh(0.5 * z[r0:r0 + HG_CHUNK])
            bc_ref[r0:r0 + HG_CHUNK, d * sec:(d + 1) * sec] = _chunk_cumsum(jnp.log2(mid + w),
                                                                             reverse=(d == 1))
            kin_ref[r0:r0 + HG_CHUNK, d * sec:(d + 1) * sec] = (half_span - w).astype(BF16)

    def put_hv(t):
        hv_ref[...] = t.astype(BF16)

    def put_gate(t):
        sg_ref[...] = (t * _sigmoid(t)).astype(BF16)

    epilogues = (put_q, put_k, put_v, put_hq, functools.partial(put_forget, 0),
                 functools.partial(put_forget, 1), put_hv, put_gate)
    for i, epilogue in enumerate(epilogues):
        epilogue(proj_all[:, i * sec:(i + 1) * sec])


def _in_proj(x2d, pos, freq, place, g, w_bf, gam, layer, tm):
    T, D = x2d.shape
    sec = w_bf.shape[1] // 8
    n_slots = gam.shape[0] // 2
    row = lambda w: pl.BlockSpec((tm, w), lambda i: (i, 0))
    return pl.pallas_call(
        functools.partial(_in_proj_kernel, layer, n_slots),
        grid=(T // tm,),
        in_specs=[row(D), pl.BlockSpec((None, 1, tm), lambda i: (i, 0, 0)),
                  _resident(freq.shape), _resident(place.shape), _resident(g.shape),
                  _resident(w_bf.shape), _resident(gam.shape)],
        out_specs=[row(3 * sec), row(sec), row(2 * sec), row(2 * sec), row(sec), row(sec)],
        out_shape=[jax.ShapeDtypeStruct((T, 3 * sec), BF16), jax.ShapeDtypeStruct((T, sec), BF16),
                   jax.ShapeDtypeStruct((T, 2 * sec), F32), jax.ShapeDtypeStruct((T, 2 * sec), BF16),
                   jax.ShapeDtypeStruct((T, sec), BF16), jax.ShapeDtypeStruct((T, sec), BF16)],
        compiler_params=pltpu.CompilerParams(dimension_semantics=("parallel",),
                                             vmem_limit_bytes=VMEM_LIMIT),
        name="in_proj",
    )(x2d, pos, freq, place, g, w_bf, gam)


DEN_ROWS = 16
ATTN_SEED_KEYS = 16
ATTN_MAX_JUMP = 100.0
ATTN_TILES_PER_TRIP = 2


def _diff_attn_kernel(lambda_init, tq, tk, q_ref, k_ref, v_ref, lam_ref, g_ref, o_ref,
                      vt_ref, qt_ref, acc_ref, s_ref):
    S = k_ref.shape[0]
    n_kv = S // tk
    dv = v_ref.shape[1]

    feat = lax.broadcasted_iota(jnp.int32, (q_ref.shape[1], tk), 0)

    def transpose_qv(i, carry):
        r = pl.ds(pl.multiple_of(i * tk, tk), tk)
        vt_ref[0:dv, r] = v_ref[r, :].T
        q_t = q_ref[r, :].T
        qt_ref[0, :, r] = jnp.where(feat < DA_HEAD_DIM, q_t, jnp.zeros_like(q_t))
        qt_ref[1, :, r] = jnp.where(feat >= DA_HEAD_DIM, q_t, jnp.zeros_like(q_t))
        return carry

    lax.fori_loop(0, n_kv, transpose_qv, 0)
    vt_ref[dv:dv + DEN_ROWS, :] = jnp.ones((DEN_ROWS, S), BF16)

    tile_cols = lambda qi: pl.ds(pl.multiple_of(qi * tq, tq), tq)

    def scores(mp, rows, q_cols):
        return jnp.dot(k_ref[rows, :], qt_ref[mp, :, q_cols], preferred_element_type=F32)

    def one_pass_tiles(tiles, after_first_chunk):
        chunk = lambda c: slice(c * tk, (c + 1) * tk)
        cols = [tile_cols(qi) for qi, _ in tiles]
        seeds = [[jnp.max(scores(mp, slice(0, ATTN_SEED_KEYS), qc), axis=0, keepdims=True)
                  for mp in range(2)] for qc in cols]
        steps = [(t, c) for t in range(len(tiles)) for c in range(n_kv)]
        jump = jnp.zeros_like(seeds[0][0])
        s_next = [scores(mp, chunk(0), cols[0]) for mp in range(2)]
        for n, (t, c) in enumerate(steps):
            qi, acc = tiles[t]
            if c == 0:
                m = list(seeds[t])
            for mp in range(2):
                s = s_next[mp]
                m_prev = m[mp]
                m[mp] = jnp.maximum(m_prev, jnp.max(s, axis=0, keepdims=True))
                p = jnp.exp2((s - m_prev).astype(BF16))
                pv = jnp.dot(vt_ref[:, chunk(c)], p, preferred_element_type=F32)
                if n + 1 < len(steps):
                    t_next, c_next = steps[n + 1]
                    s_next[mp] = scores(mp, chunk(c_next), cols[t_next])
                rescale = jnp.exp2(m_prev - m[mp])
                acc[mp] = rescale * (pv if c == 0 else acc[mp] + pv)
                jump = jnp.maximum(jump, m[mp] - m_prev)
            if n == 0:
                after_first_chunk()
            if c == n_kv - 1 and t + 1 < len(tiles):
                finalize(qi, acc)
        return jnp.max(jump)

    def two_pass_tile(qi, acc):
        q_cols = tile_cols(qi)

        def chunk_step(c, m):
            rows = pl.ds(pl.multiple_of(c * tk, tk), tk)
            m_out = []
            for mp in range(2):
                s_ref[...] = scores(mp, rows, q_cols)
                m_new = jnp.maximum(m[mp], jnp.max(s_ref[...], axis=0, keepdims=True))
                p = jnp.exp2((s_ref[...] - m_new).astype(BF16))
                acc[mp] = jnp.exp2(m[mp] - m_new) * acc[mp] + jnp.dot(
                    vt_ref[:, rows], p, preferred_element_type=F32)
                m_out.append(m_new)
            return tuple(m_out)

        for mp in range(2):
            acc[mp] = jnp.zeros(acc.shape[1:], F32)
        start = jnp.full((1, tq), -jnp.inf, F32)
        lax.fori_loop(0, n_kv, chunk_step, (start, start))

    def finalize(qi, acc):
        lam_p = lam_ref[...]
        lam = (jnp.exp(jnp.sum(lam_p[0:1, :] * lam_p[1:2, :], axis=-1, keepdims=True))
               - jnp.exp(jnp.sum(lam_p[2:3, :] * lam_p[3:4, :], axis=-1, keepdims=True)) + lambda_init)
        a0 = acc[0]
        a1 = acc[1]
        o = a0[0:dv] / a0[dv:dv + 1] - lam * (a1[0:dv] / a1[dv:dv + 1])
        o = o * lax.rsqrt(jnp.mean(o * o, axis=0, keepdims=True) + EPS) * g_ref[...]
        o_ref[tile_cols(qi), :] = (o * (1.0 - lambda_init)).T.astype(o_ref.dtype)

    n_slots = acc_ref.shape[1]
    n_trips = S // tq // n_slots
    assert S // tq == n_trips * n_slots

    acc_ref[1, n_slots - 1] = jnp.ones(acc_ref.shape[2:], F32)

    def query_tiles(u, carry):
        bank = lax.rem(u, 2)
        tiles = [(u * n_slots + t, acc_ref.at[bank, t]) for t in range(n_slots)]
        worst_jump = one_pass_tiles(
            tiles, after_first_chunk=lambda: finalize(jnp.maximum(u * n_slots - 1, 0),
                                                      acc_ref.at[1 - bank, n_slots - 1]))

        @pl.when(jnp.logical_not(worst_jump <= ATTN_MAX_JUMP))
        def _():
            for t, (qi, acc) in enumerate(tiles):
                two_pass_tile(qi, acc)
                if t + 1 < n_slots:
                    finalize(qi, acc)

        return carry

    lax.fori_loop(0, n_trips, query_tiles, 0)
    finalize(S // tq - 1, acc_ref.at[(n_trips - 1) % 2, n_slots - 1])


def _diff_attn(qkv3, lam_p, g_col, lambda_init, tq, tk):
    B, S, _ = qkv3.shape
    H = DA_HEADS
    col = lambda off: pl.BlockSpec((None, S, LANES), lambda b, h: (b, 0, off + h))
    return pl.pallas_call(
        functools.partial(_diff_attn_kernel, lambda_init, tq, tk),
        grid=(B, H),
        in_specs=[col(0), col(H), col(2 * H), _resident(lam_p.shape), _resident(g_col.shape)],
        out_specs=col(0),
        out_shape=jax.ShapeDtypeStruct((B, S, H * DA_V_DIM), BF16),
        scratch_shapes=[pltpu.VMEM((DA_V_DIM + DEN_ROWS, S), BF16),
                        pltpu.VMEM((2, LANES, S), BF16),
                        pltpu.VMEM((2, ATTN_TILES_PER_TRIP, 2, DA_V_DIM + DEN_ROWS, tq), F32),
                        pltpu.VMEM((tk, tq), F32)],
        compiler_params=pltpu.CompilerParams(
            dimension_semantics=("parallel", "parallel"), vmem_limit_bytes=VMEM_LIMIT),
        name="diff_attn",
    )(qkv3, qkv3, qkv3, lam_p, g_col)


def _hgrn_scores(q, kin, v, b, reverse, exact_diag):
    C, n_sub = HG_CHUNK, HG_CHUNK // HG_SUB
    blk = lambda j: slice(j * HG_SUB, (j + 1) * HG_SUB)
    bound = [b[j * HG_SUB:j * HG_SUB + 1, :] if reverse else b[(j + 1) * HG_SUB - 1:(j + 1) * HG_SUB, :]
             for j in range(n_sub)]
    b_far = bound[0] if reverse else bound[-1]
    decay = lambda e: jnp.exp2(e).astype(BF16)
    k_til = jnp.concatenate([kin[blk(j)] * decay(bound[j] - b[blk(j)]) for j in range(n_sub)], axis=0)
    own = 0 if exact_diag else 1
    rows = [slice(0, (j + own) * HG_SUB) if reverse else slice((j + 1 - own) * HG_SUB, C)
            for j in range(n_sub)]
    q_til = jnp.concatenate([q[rows[j]] * decay(b[rows[j]] - bound[j]) for j in range(n_sub)
                             if rows[j].stop > rows[j].start], axis=0)
    offs = np.cumsum([0] + [r.stop - r.start for r in rows])
    scores = lax.dot_general(q_til, k_til, (((1,), (1,)), ((), ())),
                             preferred_element_type=F32)
    increment = lax.dot_general(v, kin * decay(b_far - b), (((0,), (0,)), ((), ())),
                                preferred_element_type=F32)
    return scores, [int(o) for o in offs], q * decay(b), jnp.exp2(b_far), increment


def _hgrn_intra(scores, offs, v, reverse, exact_diag, q, kin, b):
    C, n_sub = HG_CHUNK, HG_CHUNK // HG_SUB
    col = lax.broadcasted_iota(jnp.int32, (HG_SUB, C), 1)
    own = 0 if exact_diag else 1
    row_blocks = []
    for i in range(n_sub):
        js = range(i + 1 - own, n_sub) if reverse else range(0, i + own)
        a_i = jnp.zeros((HG_SUB, C), F32)
        for j in js:
            r0 = offs[j] + (i * HG_SUB if reverse else (i - j - 1 + own) * HG_SUB)
            a_i = jnp.where(col // HG_SUB == j, scores[r0:r0 + HG_SUB, :], a_i)
        if exact_diag:
            rs = slice(i * HG_SUB, (i + 1) * HG_SUB)
            q_i, k_i, b_i = q[rs].astype(F32), kin[rs].astype(F32), b[rs]
            for s in range(HG_SUB):
                w = q_i * (k_i[s:s + 1] * jnp.exp2(jnp.minimum(b_i - b_i[s:s + 1], 0.0)))
                a_i = jnp.where(col == i * HG_SUB + s, jnp.sum(w, axis=1, keepdims=True), a_i)
        row_blocks.append(a_i)
    attn = jnp.concatenate(row_blocks, axis=0)
    t_i = lax.broadcasted_iota(jnp.int32, (C, C), 0)
    s_i = lax.broadcasted_iota(jnp.int32, (C, C), 1)
    attn = jnp.where((t_i <= s_i) if reverse else (t_i >= s_i), attn, 0.0)
    return jnp.dot(attn.astype(BF16), v, preferred_element_type=F32)


def _hgrn_kernel(exact_ref, q_f_ref, q_b_ref, v_f_ref, v_b_ref, bc_f_ref, bc_b_ref, kin_f_ref, kin_b_ref,
                 y_f_ref, y_b_ref, st_f_ref, st_b_ref):
    @pl.when(pl.program_id(2) == 0)
    def _():
        st_f_ref[...] = jnp.zeros_like(st_f_ref)
        st_b_ref[...] = jnp.zeros_like(st_b_ref)

    n_chunks = q_f_ref.shape[0] // HG_CHUNK
    directions = ((False, q_f_ref, v_f_ref, bc_f_ref, kin_f_ref, y_f_ref, st_f_ref),
                  (True, q_b_ref, v_b_ref, bc_b_ref, kin_b_ref, y_b_ref, st_b_ref))

    def body(exact_diag, u, carry):
        work, intra = [], []
        for reverse, q_ref, v_ref, bc_ref, kin_ref, y_ref, st_ref in directions:
            for j in range(HG_CHUNKS_PER_TRIP):
                c = u * HG_CHUNKS_PER_TRIP + j
                cc = (n_chunks - 1 - c) if reverse else c
                rs = pl.ds(pl.multiple_of(cc * HG_CHUNK, HG_CHUNK), HG_CHUNK)
                v = v_ref[rs, :]
                stage1 = _hgrn_scores(q_ref[rs, :], kin_ref[rs, :], v, bc_ref[rs, :], reverse, exact_diag)
                work.append((reverse, rs, v, y_ref, st_ref, stage1))
        for (reverse, rs, v, y_ref, st_ref, stage1), (_, q_ref, _, bc_ref, kin_ref, _, _) in zip(
                work, [d for d in directions for _ in range(HG_CHUNKS_PER_TRIP)]):
            intra.append(_hgrn_intra(stage1[0], stage1[1], v, reverse, exact_diag,
                                     q_ref[rs, :], kin_ref[rs, :], bc_ref[rs, :]))
        states = {}
        for (reverse, rs, v, y_ref, st_ref, stage1), o_intra in zip(work, intra):
            _, _, q_state, decay, increment = stage1
            state_t = states.get(reverse)
            if state_t is None:
                state_t = st_ref[...]
            o_inter = lax.dot_general(q_state, state_t.astype(BF16), (((1,), (1,)), ((), ())),
                                      preferred_element_type=F32)
            states[reverse] = decay * state_t + increment
            y_ref[rs, :] = (o_intra + o_inter).astype(y_ref.dtype)
        for reverse, *_, st_ref in directions:
            st_ref[...] = states[reverse]
        return carry

    assert n_chunks % HG_CHUNKS_PER_TRIP == 0

    for exact_diag in (False, True):
        @pl.when((exact_ref[0] != 0) == exact_diag)
        def _():
            lax.fori_loop(0, n_chunks // HG_CHUNKS_PER_TRIP, functools.partial(body, exact_diag), 0)


def _hgrn(exact_flag, hq3, hv3, bc3, kin3, seg):
    B, S, W = hq3.shape
    H = W // HG_DIM
    n_seg = S // seg
    fwd = lambda off: pl.BlockSpec((None, seg, HG_DIM), lambda b, h, s: (b, s, off + h))
    bwd = lambda off: pl.BlockSpec((None, seg, HG_DIM), lambda b, h, s: (b, n_seg - 1 - s, off + h))
    return pl.pallas_call(
        _hgrn_kernel,
        grid=(B, H, n_seg),
        in_specs=[pl.BlockSpec(memory_space=pltpu.SMEM),
                  fwd(0), bwd(0), fwd(0), bwd(0), fwd(0), bwd(H), fwd(0), bwd(H)],
        out_specs=[fwd(0), bwd(0)],
        out_shape=[jax.ShapeDtypeStruct((B, S, W), BF16), jax.ShapeDtypeStruct((B, S, W), BF16)],
        scratch_shapes=[pltpu.VMEM((HG_DIM, HG_DIM), F32), pltpu.VMEM((HG_DIM, HG_DIM), F32)],
        compiler_params=pltpu.CompilerParams(
            dimension_semantics=("parallel", "parallel", "arbitrary"), vmem_limit_bytes=VMEM_LIMIT),
        name="hgrn2",
    )(exact_flag, hq3, hq3, hv3, hv3, bc3, bc3, kin3, kin3)


def _hgrn_needs_exact_diag(lb_gamma, layer):
    e = jax.nn.softmax(lb_gamma.astype(F32), axis=1)
    lb = jnp.sum(e[:, :layer + 1], axis=1)
    worst = (HG_SUB - 1) * jnp.max(-jnp.log2(lb))
    return jnp.logical_not(worst <= HG_MAX_SUB_DECAY).astype(jnp.int32).reshape(1)


def _mixer_residual(x, oda, yf, yb, sg, g_hg, w_ref):
    half = oda.shape[1]
    y = yf.astype(F32) + yb.astype(F32)
    sg = sg.astype(F32)
    heads = []
    for hh in range(y.shape[1] // HG_DIM):
        cs = slice(hh * HG_DIM, (hh + 1) * HG_DIM)
        heads.append(_rms(y[:, cs], g_hg) * sg[:, cs])
    o_hg = jnp.concatenate(heads, axis=1).astype(BF16)
    return (x + jnp.dot(oda, w_ref[0:half, :], preferred_element_type=F32)
            + jnp.dot(o_hg, w_ref[half:, :], preferred_element_type=F32))


HALO_ROWS = 16


def _ffn_ple_kernel(fc, x_ref, oda_ref, yf_ref, yb_ref, sg_ref, xh_ref, odah_ref, yfh_ref, ybh_ref,
                    sgh_ref, p_ref, ghg_ref, wo_ref, gf_ref, wg_ref, wu_ref, cw_ref, cb_ref, wd_ref,
                    gp_ref, wpg_ref, wple_ref, gfin_ref, o_ref, h_ref, acc_ref):
    tm = x_ref.shape[0]
    ext = lambda tile_ref, halo_ref: jnp.concatenate([tile_ref[...], halo_ref[...]], axis=0)
    x1 = _mixer_residual(ext(x_ref, xh_ref), ext(oda_ref, odah_ref), ext(yf_ref, yfh_ref),
                         ext(yb_ref, ybh_ref), ext(sg_ref, sgh_ref), ghg_ref[...], wo_ref)
    xt = x1[0:tm]
    o_ref[...] = jnp.dot(p_ref[...].astype(BF16), wple_ref[...], preferred_element_type=F32)
    h_ref[...] = _rms(x1, gf_ref[...]).astype(BF16)
    assert sum(fc) == wg_ref.shape[1]
    edges = np.cumsum((0,) + tuple(fc))
    chunks = [slice(int(lo), int(hi)) for lo, hi in zip(edges[:-1], edges[1:])]

    def up_gate(cs):
        return (jnp.dot(h_ref[...], wg_ref[:, cs], preferred_element_type=F32),
                jnp.dot(h_ref[0:tm, :], wu_ref[:, cs], preferred_element_type=F32))

    nxt = up_gate(chunks[0])
    for c, cs in enumerate(chunks):
        a_ext, u = nxt
        if c + 1 < len(chunks):
            nxt = up_gate(chunks[c + 1])
        row = lax.broadcasted_iota(jnp.int32, u.shape, 0)
        a = a_ext[0:tm]
        a_prev = jnp.where(row == 0, a_ext[tm:tm + 1], pltpu.roll(a, 1, axis=0))
        a_next = jnp.where(row == tm - 1, a_ext[tm + 1:tm + 2], pltpu.roll(a, tm - 1, axis=0))
        cv = cb_ref[:, cs] + a_prev * cw_ref[0:1, cs] + a * cw_ref[1:2, cs] + a_next * cw_ref[2:3, cs]
        act = (_gelu(cv) * u).astype(BF16)
        d = jnp.dot(act, wd_ref[cs, :], preferred_element_type=F32)
        if c == 0:
            acc_ref[...] = d
        else:
            acc_ref[...] += d
    x2 = xt + acc_ref[...]
    gate = _sigmoid(jnp.dot(_rms(x2, gp_ref[...]).astype(BF16), wpg_ref[...], preferred_element_type=F32))
    o_ref[...] = _rms(x2 + o_ref[...] * gate, gfin_ref[...])


def _ffn_ple(streams, p2d, consts, tm, seq, fc):
    T, D = streams[0].shape
    row = lambda w: pl.BlockSpec((tm, w), lambda i: (i, 0))
    halo = lambda w: pl.BlockSpec((None, HALO_ROWS, w), lambda i: (i, 0, 0))
    halos = [_conv_halo(a, tm, seq) for a in streams]
    return pl.pallas_call(
        functools.partial(_ffn_ple_kernel, fc),
        grid=(T // tm,),
        in_specs=[row(a.shape[1]) for a in streams] + [halo(a.shape[1]) for a in streams]
                 + [row(p2d.shape[1])] + [_resident(c.shape) for c in consts],
        out_specs=row(D),
        out_shape=jax.ShapeDtypeStruct((T, D), F32),
        scratch_shapes=[pltpu.VMEM((tm + HALO_ROWS, D), BF16), pltpu.VMEM((tm, D), F32)],
        compiler_params=pltpu.CompilerParams(dimension_semantics=("parallel",),
                                             vmem_limit_bytes=VMEM_LIMIT),
        name="ffn_ple",
    )(*streams, *halos, p2d, *consts)


def _conv_halo(x1, tm, seq):
    T, D = x1.shape
    nt = T // tm
    xr = x1.reshape(nt, tm, D)
    zero = jnp.zeros((1, D), x1.dtype)
    prev = jnp.concatenate([zero, xr[:-1, tm - 1, :]], axis=0)
    nxt = jnp.concatenate([xr[1:, 0, :], zero], axis=0)
    start = (jnp.arange(nt) * tm) % seq
    prev = jnp.where((start == 0)[:, None], 0.0, prev)
    nxt = jnp.where((start + tm == seq)[:, None], 0.0, nxt)
    pad = jnp.zeros((nt, HALO_ROWS - 2, D), x1.dtype)
    return jnp.concatenate([prev[:, None, :], nxt[:, None, :], pad], axis=1)


def _rope_constants():
    half = ROT_DIM // 2
    inv_freq = (np.float32(ROPE_THETA) ** (-np.arange(half, dtype=np.float32) / np.float32(half))).astype(np.float32)
    d = np.arange(LANES) % DA_HEAD_DIM
    place = np.zeros((4 * half, 2 * LANES), np.float32)
    for f in range(half):
        hit = (d < ROT_DIM) & (d % half == f)
        sign = np.where(d < half, -1.0, 1.0)
        place[f, :LANES] = place[half + f, :LANES] = hit
        place[2 * half + f, LANES:] = place[3 * half + f, LANES:] = hit * sign
    return jnp.asarray(inv_freq.reshape(half, 1)), jnp.asarray(place, dtype=BF16)


def kernel(x, p, positions, norm_mix_g, w_in, lam_q1, lam_k1, lam_q2, lam_k2, da_subln_g, hg_lb_gamma, hg_norm_g, w_out, norm_ffn_g, w_ffn_gate, w_ffn_up, ffn_conv_w, ffn_conv_b, w_ffn_down, norm_ple_g, w_ple, w_ple_gate, final_norm_g):
    B, S, D = x.shape
    T = B * S
    depth = w_in.shape[0]
    tm = min(512, S)
    tq = min(512, S)
    tk = min(512, S)
    seg = min(4096, S)
    fc = (768, 768, 768, 512)
    row2 = lambda v: v.reshape(1, -1)
    xc = x.reshape(T, D)
    pos = positions.reshape(T // tm, 1, tm)
    rope_freq, rope_place = _rope_constants()
    for i in range(depth):
        lambda_init = 0.8 - 0.6 * math.exp(-0.3 * i)
        gam = hg_lb_gamma.reshape(-1, hg_lb_gamma.shape[-1])
        qkv, hq, bc, kin, hv, sg = _in_proj(xc, pos, rope_freq, rope_place, row2(norm_mix_g[i]), w_in[i].astype(BF16),
                                            gam, i, tm)
        lam_p = jnp.stack([lam_q1[i], lam_k1[i], lam_q2[i], lam_k2[i]], axis=0)
        o_da = _diff_attn(qkv.reshape(B, S, -1), lam_p, da_subln_g[i].reshape(-1, 1), lambda_init, tq, tk)
        y_f, y_b = _hgrn(_hgrn_needs_exact_diag(hg_lb_gamma, i), hq.reshape(B, S, -1), hv.reshape(B, S, -1),
                         bc.reshape(B, S, -1), kin.reshape(B, S, -1), seg)
        streams = (xc, o_da.reshape(T, -1), y_f.reshape(T, -1), y_b.reshape(T, -1), sg)
        consts = (row2(hg_norm_g[i]), w_out[i].astype(BF16), row2(norm_ffn_g[i]),
                  w_ffn_gate[i].astype(BF16), w_ffn_up[i].astype(BF16), ffn_conv_w[i],
                  row2(ffn_conv_b[i]), w_ffn_down[i].astype(BF16), row2(norm_ple_g[i]),
                  w_ple_gate[i].astype(BF16), w_ple[i].astype(BF16), row2(final_norm_g))
        xc = _ffn_ple(streams, p[i].reshape(T, -1), consts, tm, S, fc)
        assert i == depth - 1, "multi-layer stacks need the final norm split out of ffn_ple"
    return xc.reshape(B, S, D)
```
